```python
import math
import jax
import jax.numpy as jnp
from jax import lax
import numpy as np

D_MODEL = 1024
BATCH = 2
SEQ = 8192
DEPTH = 2

CTX_LEN = 256
GRID_W = 64
N_MOD = 6
EPS = 1e-6
ATTN_W = D_MODEL // 2
ATTN_HD = 64
ATTN_VD = 2 * ATTN_HD
ATTN_HEADS = ATTN_W // ATTN_VD
Q_BLOCK = 128
ROPE_BASE = 10000.0
SSM_W = D_MODEL // 4
SSM_GROUP = 16
SSM_GROUPS = SSM_W // SSM_GROUP
SSM_STATE = 64
POOL_W = D_MODEL // 4
POOL_WINDOWS = (2, 4, 8, 16)
POOL_GROUP = POOL_W // len(POOL_WINDOWS)
MIX_W = ATTN_W + SSM_W + POOL_W
SSM_OFF = 3 * ATTN_W
POOL_OFF = SSM_OFF + SSM_W
IN_W = POOL_OFF + POOL_W
PEER_HEADS = 8
PEER_NKEYS = 128
PEER_EXPERTS = PEER_NKEYS * PEER_NKEYS
PEER_QDIM = 256
PEER_KDIM = PEER_QDIM // 2
PEER_TOPK = 16
PEER_CHUNK = 128

kernel_name = 'hybrid_diffattn_s5_pool_peer_block'


def rmsnorm(x, g):
    xf = x.astype(jnp.float32)
    y = xf * lax.rsqrt(jnp.mean(xf * xf, axis=-1, keepdims=True) + EPS)
    return (y * g.astype(jnp.float32)).astype(x.dtype)


def modulate(h, shift, scale):
    return h * (1.0 + scale) + shift


def axial_rope_tables(rows):
    r = jnp.repeat(jnp.arange(rows), GRID_W)
    col = jnp.tile(jnp.arange(GRID_W), rows)
    pos = jnp.stack([r, col], axis=-1).astype(jnp.float32)
    nf = ATTN_HD // 4
    inv = 1.0 / (ROPE_BASE ** (jnp.arange(nf, dtype=jnp.float32) / nf))
    ang = pos[:, :, None] * inv
    return jnp.cos(ang), jnp.sin(ang)


def apply_rope(x, cos, sin):
    nf = x.shape[-1] // 4
    xr = x.reshape(x.shape[:-1] + (2, 2, nf))
    x1 = xr[..., 0, :]
    x2 = xr[..., 1, :]
    cb = cos[:, None, None]
    sb = sin[:, None, None]
    out = jnp.stack([x1 * cb - x2 * sb, x2 * cb + x1 * sb], axis=-2)
    return out.reshape(x.shape).astype(x.dtype)


def split_qkv(p):
    B, L = p.shape[0], p.shape[1]
    q = p[..., :ATTN_W].reshape(B, L, ATTN_HEADS, 2, ATTN_HD)
    k = p[..., ATTN_W:2 * ATTN_W].reshape(B, L, ATTN_HEADS, 2, ATTN_HD)
    v = p[..., 2 * ATTN_W:SSM_OFF].reshape(B, L, ATTN_HEADS, ATTN_VD)
    return q, k, v


def diff_attend(q, k, v, lam):
    s = jnp.einsum('bqhsd,bkhsd->bhsqk', q, k).astype(jnp.float32)
    p = jax.nn.softmax(s, axis=-1)
    a = (p[:, :, 0] - lam * p[:, :, 1]).astype(v.dtype)
    return jnp.einsum('bhqk,bkhe->bqhe', a, v)


def diff_attn_latent(q, k, v, kc, vc, lam, cos, sin):
    B, L = q.shape[0], q.shape[1]
    q = apply_rope(q, cos, sin) * (ATTN_HD ** -0.5)
    k_all = jnp.concatenate([apply_rope(k, cos, sin), kc], axis=1)
    v_all = jnp.concatenate([v, vc], axis=1)
    nb = L // Q_BLOCK
    qb = jnp.moveaxis(q.reshape((B, nb, Q_BLOCK) + q.shape[2:]), 1, 0)
    o = lax.map(lambda qi: diff_attend(qi, k_all, v_all, lam), qb)
    return jnp.moveaxis(o, 0, 1).reshape(B, L, ATTN_HEADS, ATTN_VD)


def attn_head_out(o, g, lam_init):
    B, L = o.shape[0], o.shape[1]
    return (rmsnorm(o, g) * (1.0 - lam_init)).reshape(B, L, ATTN_W)


def zoh_discretise(lre, lim, log_step, bre, bim):
    step = jnp.exp(log_step)[:, None]
    er = jnp.exp(lre * step)
    ar = er * jnp.cos(lim * step)
    ai = er * jnp.sin(lim * step)
    nr = ar - 1.0
    den = lre * lre + lim * lim
    cr = (nr * lre + ai * lim) / den
    ci = (ai * lre - nr * lim) / den
    bbr = cr[..., None] * bre - ci[..., None] * bim
    bbi = cr[..., None] * bim + ci[..., None] * bre
    return ar, ai, bbr, bbi


def complex_affine_combine(e1, e2):
    a1r, a1i, b1r, b1i = e1
    a2r, a2i, b2r, b2i = e2
    return (a2r * a1r - a2i * a1i, a2r * a1i + a2i * a1r,
            a2r * b1r - a2i * b1i + b2r, a2r * b1i + a2i * b1r + b2i)


def ssm_scan(u, ar, ai, bbr, bbi, cr, ci, h0r, h0i, reverse, with_output):
    if reverse:
        u = jnp.flip(u, axis=1)
    bur = jnp.einsum('gph,blgh->blgp', bbr, u)
    bui = jnp.einsum('gph,blgh->blgp', bbi, u)
    bur = bur.at[:, 0].add(ar * h0r - ai * h0i)
    bui = bui.at[:, 0].add(ar * h0i + ai * h0r)
    a_r = jnp.broadcast_to(ar, bur.shape)
    a_i = jnp.broadcast_to(ai, bui.shape)
    _, _, hr, hi = lax.associative_scan(complex_affine_combine, (a_r, a_i, bur, bui), axis=1)
    last = (hr[:, -1], hi[:, -1])
    if not with_output:
        return None, last
    y = jnp.einsum('ghp,blgp->blgh', cr, hr) - jnp.einsum('ghp,blgp->blgh', ci, hi)
    if reverse:
        y = jnp.flip(y, axis=1)
    return y, last


def s5_glu(y, w_glu):
    B, L = y.shape[0], y.shape[1]
    yg = jax.nn.gelu(y.reshape(B, L, SSM_W), approximate=False)
    return yg * jax.nn.sigmoid(yg @ w_glu.astype(jnp.float32))


def s5_mixer(u_lat, u_ctx, lam_re, lam_im, log_step, b_re, b_im, c_re, c_im, d, w_glu, ctx_out):
    f32 = jnp.float32
    B, L = u_lat.shape[0], u_lat.shape[1]
    Lc = u_ctx.shape[1]
    ul = u_lat.astype(f32).reshape(B, L, SSM_GROUPS, SSM_GROUP)
    uc = u_ctx.astype(f32).reshape(B, Lc, SSM_GROUPS, SSM_GROUP)
    dg = d.astype(f32).reshape(SSM_GROUPS, SSM_GROUP)
    y_lat = dg * ul
    y_ctx = dg * uc if ctx_out else None
    zero = jnp.zeros((B, SSM_GROUPS, SSM_STATE), f32)
    for dr in range(2):
        rev = dr == 1
        ar, ai, bbr, bbi = zoh_discretise(lam_re[dr].astype(f32), lam_im[dr].astype(f32),
                                          log_step[dr].astype(f32), b_re[dr].astype(f32), b_im[dr].astype(f32))
        cr = c_re[dr].astype(f32)
        ci = c_im[dr].astype(f32)
        yc, (hcr, hci) = ssm_scan(uc, ar, ai, bbr, bbi, cr, ci, zero, zero, rev, ctx_out)
        yl, _ = ssm_scan(ul, ar, ai, bbr, bbi, cr, ci, hcr, hci, rev, True)
        y_lat = y_lat + yl
        if ctx_out:
            y_ctx = y_ctx + yc
    out_lat = s5_glu(y_lat, w_glu).astype(u_lat.dtype)
    if not ctx_out:
        return out_lat, None
    return out_lat, s5_glu(y_ctx, w_glu).astype(u_ctx.dtype)


def pool_mix(u, w, scale):
    B, L = u.shape[0], u.shape[1]
    uf = u.astype(jnp.float32)
    csum = jnp.concatenate([jnp.zeros((B, 1, POOL_W), jnp.float32), jnp.cumsum(uf, axis=1)], axis=1)
    t = jnp.arange(L)
    outs = []
    for gi, win in enumerate(POOL_WINDOWS):
        sl = slice(gi * POOL_GROUP, (gi + 1) * POOL_GROUP)
        lo = jnp.clip(t - win // 2, 0, L)
        hi = jnp.clip(t + win // 2, 0, L)
        cg = csum[..., sl]
        mean = (jnp.take(cg, hi, axis=1) - jnp.take(cg, lo, axis=1)) / (hi - lo).astype(jnp.float32)[:, None]
        outs.append(jnp.einsum('blc,ce->ble', mean - uf[..., sl], w[gi].astype(jnp.float32)))
    return (jnp.concatenate(outs, axis=-1) * scale.astype(jnp.float32)).astype(u.dtype)


def peer_ffn(h, wq, keys, U, V):
    shp = h.shape
    x = h.reshape(-1, shp[-1])
    T = x.shape[0]
    q = (x @ wq).reshape(T, PEER_HEADS, 2, PEER_KDIM)
    s = jnp.einsum('thcd,hcnd->thcn', q, keys).astype(jnp.float32)
    s1, i1 = lax.top_k(s[:, :, 0], PEER_TOPK)
    s2, i2 = lax.top_k(s[:, :, 1], PEER_TOPK)
    cs = (s1[..., :, None] + s2[..., None, :]).reshape(T, PEER_HEADS, PEER_TOPK * PEER_TOPK)
    cidx = (i1[..., :, None] * PEER_NKEYS + i2[..., None, :]).reshape(T, PEER_HEADS, PEER_TOPK * PEER_TOPK)
    best, pos = lax.top_k(cs, PEER_TOPK)
    idx = jnp.take_along_axis(cidx, pos, axis=-1)
    g = jax.nn.softmax(best, axis=-1).astype(h.dtype)
    nc = T // PEER_CHUNK

    def chunk(args):
        xc, ic, gc = args
        a = jax.nn.gelu(jnp.einsum('cd,chkd->chk', xc, U[ic]), approximate=False) * gc
        return jnp.einsum('chk,chkd->cd', a, V[ic])

    y = lax.map(chunk, (x.reshape(nc, PEER_CHUNK, shp[-1]),
                        idx.reshape(nc, PEER_CHUNK, PEER_HEADS, PEER_TOPK),
                        g.reshape(nc, PEER_CHUNK, PEER_HEADS, PEER_TOPK)))
    return y.reshape(shp)


def setup_inputs(seed: int = 0) -> dict:
    key = jax.random.key(seed)
    ks = jax.random.split(key, 32)
    f = jnp.float32
    D = D_MODEL
    G, P, H = SSM_GROUPS, SSM_STATE, SSM_GROUP

    def nrm(k, shape, s):
        return jax.random.normal(k, shape, f) * s

    n_idx = jnp.arange(P, dtype=f)
    return {
        'x': nrm(ks[0], (BATCH, SEQ, D), 1.0),
        'c': nrm(ks[1], (BATCH, D), 1.0),
        'ctx': nrm(ks[2], (BATCH, CTX_LEN, D), 1.0),
        'c_ctx': nrm(ks[3], (D,), 1.0),
        'w_mod': nrm(ks[4], (DEPTH, D, N_MOD * D), 0.5 * D ** -0.5),
        'b_mod': nrm(ks[5], (DEPTH, N_MOD * D), 0.01),
        'norm1_g': 1.0 + nrm(ks[6], (DEPTH, D), 0.01),
        'norm2_g': 1.0 + nrm(ks[7], (DEPTH, D), 0.01),
        'w_in': nrm(ks[8], (DEPTH, D, IN_W), D ** -0.5),
        'w_out': nrm(ks[9], (DEPTH, MIX_W, D), MIX_W ** -0.5),
        'lam_q1': nrm(ks[10], (DEPTH, ATTN_HD), 0.1),
        'lam_k1': nrm(ks[11], (DEPTH, ATTN_HD), 0.1),
        'lam_q2': nrm(ks[12], (DEPTH, ATTN_HD), 0.1),
        'lam_k2': nrm(ks[13], (DEPTH, ATTN_HD), 0.1),
        'subln_g': 1.0 + nrm(ks[14], (DEPTH, ATTN_VD), 0.01),
        'ssm_lambda_re': -0.5 + nrm(ks[15], (DEPTH, 2, G, P), 0.01),
        'ssm_lambda_im': math.pi * n_idx + nrm(ks[16], (DEPTH, 2, G, P), 0.01),
        'ssm_log_step': jax.random.uniform(ks[17], (DEPTH, 2, G), f, math.log(1e-3), math.log(1e-1)),
        'ssm_b_re': nrm(ks[18], (DEPTH, 2, G, P, H), (2 * H) ** -0.5),
        'ssm_b_im': nrm(ks[19], (DEPTH, 2, G, P, H), (2 * H) ** -0.5),
        'ssm_c_re': nrm(ks[20], (DEPTH, 2, G, H, P), P ** -0.5),
        'ssm_c_im': nrm(ks[21], (DEPTH, 2, G, H, P), P ** -0.5),
        'ssm_d': nrm(ks[22], (DEPTH, SSM_W), 1.0),
        'ssm_w_glu': nrm(ks[23], (DEPTH, SSM_W, SSM_W), SSM_W ** -0.5),
        'pool_w': nrm(ks[24], (DEPTH, len(POOL_WINDOWS), POOL_GROUP, POOL_GROUP), POOL_GROUP ** -0.5),
        'pool_scale': 1.0 + nrm(ks[25], (DEPTH, POOL_W), 0.1),
        'peer_wq': nrm(ks[26], (DEPTH, D, PEER_HEADS * PEER_QDIM), D ** -0.5),
        'peer_keys': nrm(ks[27], (DEPTH, PEER_HEADS, 2, PEER_NKEYS, PEER_KDIM), PEER_KDIM ** -0.5),
        'peer_u': nrm(ks[28], (DEPTH, PEER_EXPERTS, D), D ** -0.5),
        'peer_v': nrm(ks[29], (DEPTH, PEER_EXPERTS, D), 0.5),
        'final_g': 1.0 + nrm(ks[30], (D,), 0.01),
    }


def reference(x, c, ctx, c_ctx, w_mod, b_mod, norm1_g, norm2_g, w_in, w_out,
              lam_q1, lam_k1, lam_q2, lam_k2, subln_g,
              ssm_lambda_re, ssm_lambda_im, ssm_log_step, ssm_b_re, ssm_b_im, ssm_c_re, ssm_c_im,
              ssm_d, ssm_w_glu, pool_w, pool_scale, peer_wq, peer_keys, peer_u, peer_v, final_g):
    B, L, D = x.shape
    rows = L // GRID_W
    cos, sin = axial_rope_tables(rows)
    sc = jax.nn.silu(c)
    scc = jax.nn.silu(c_ctx)
    xc = ctx
    for l in range(DEPTH):
        last = l == DEPTH - 1
        mod = (sc @ w_mod[l] + b_mod[l]).reshape(B, N_MOD, 1, D)
        modc = (scc @ w_mod[l] + b_mod[l]).reshape(N_MOD, D)
        h = modulate(rmsnorm(x, norm1_g[l]), mod[:, 0], mod[:, 1])
        hc = modulate(rmsnorm(xc, norm1_g[l]), modc[0], modc[1])
        pl = h @ w_in[l]
        pc = hc @ w_in[l]
        lam_init = 0.8 - 0.6 * math.exp(-0.3 * l)
        lam = (jnp.exp(jnp.sum(lam_q1[l].astype(jnp.float32) * lam_k1[l].astype(jnp.float32)))
               - jnp.exp(jnp.sum(lam_q2[l].astype(jnp.float32) * lam_k2[l].astype(jnp.float32)))
               + lam_init)
        ql, kl, vl = split_qkv(pl)
        qc, kc, vc = split_qkv(pc)
        attn_l = attn_head_out(diff_attn_latent(ql, kl, vl, kc, vc, lam, cos, sin), subln_g[l], lam_init)
        ssm_l, ssm_c = s5_mixer(pl[..., SSM_OFF:POOL_OFF], pc[..., SSM_OFF:POOL_OFF],
                                ssm_lambda_re[l], ssm_lambda_im[l], ssm_log_step[l],
                                ssm_b_re[l], ssm_b_im[l], ssm_c_re[l], ssm_c_im[l],
                                ssm_d[l], ssm_w_glu[l], not last)
        pool_l = pool_mix(pl[..., POOL_OFF:], pool_w[l], pool_scale[l])
        x = x + mod[:, 2] * (jnp.concatenate([attn_l, ssm_l, pool_l], axis=-1) @ w_out[l])
        if not last:
            attn_c = attn_head_out(diff_attend(qc * (ATTN_HD ** -0.5), kc, vc, lam), subln_g[l], lam_init)
            pool_c = pool_mix(pc[..., POOL_OFF:], pool_w[l], pool_scale[l])
            xc = xc + modc[2] * (jnp.concatenate([attn_c, ssm_c, pool_c], axis=-1) @ w_out[l])
        h = modulate(rmsnorm(x, norm2_g[l]), mod[:, 3], mod[:, 4])
        x = x + mod[:, 5] * peer_ffn(h, peer_wq[l], peer_keys[l], peer_u[l], peer_v[l])
        if not last:
            hc = modulate(rmsnorm(xc, norm2_g[l]), modc[3], modc[4])
            xc = xc + modc[5] * peer_ffn(hc, peer_wq[l], peer_keys[l], peer_u[l], peer_v[l])
    return rmsnorm(x, final_g)
```

```python
import functools
import math

import numpy as np
import jax
import jax.numpy as jnp
from jax import lax
from jax.experimental import pallas as pl
from jax.experimental.pallas import tpu as pltpu

F32 = jnp.float32
BF16 = jnp.bfloat16

EPS = 1e-6
GRID_W = 64
N_MOD = 6
ATTN_HD = 64
ATTN_VD = 128
ATTN_HEADS = 4
ATTN_W = 512
ROPE_BASE = 10000.0
SSM_W = 256
SSM_GROUP = 16
SSM_GROUPS = 16
SSM_STATE = 64
SSM_CHUNK = 16
POOL_W = 256
POOL_WINDOWS = (2, 4, 8, 16)
POOL_GROUP = 64
POOL_HALO = 8
SSM_OFF = 3 * ATTN_W
POOL_OFF = SSM_OFF + SSM_W
IN_W = POOL_OFF + POOL_W
PEER_HEADS = 8
PEER_NKEYS = 128
PEER_KDIM = 128
PEER_TOPK = 16
NEG_BIG = -3.0e38
SQRT_HALF = float(np.sqrt(0.5).astype(np.float32))

LANE = 128
ROW_TILE = 256
VMEM_LIMIT = 56 * 1024 * 1024


def _cparams(sem):
    return pltpu.CompilerParams(dimension_semantics=sem, vmem_limit_bytes=VMEM_LIMIT)


def _gelu(x):
    return 0.5 * x * (1.0 + lax.erf(x * SQRT_HALF))


def _norm_mod(x, g, shift, scale):
    ms = jnp.mean(x * x, axis=-1, keepdims=True)
    y = x * lax.rsqrt(ms + EPS) * g
    return y * (1.0 + scale) + shift


def _mod_kernel(s_ref, w_ref, b_ref, o_ref):
    s = s_ref[...]
    s = s * jax.nn.sigmoid(s)
    o_ref[0] = jnp.dot(s.astype(BF16), w_ref[0].astype(BF16), preferred_element_type=F32) + b_ref[0]


def _mod_vectors(cs, w_mod, b_mod):
    depth, d, n = w_mod.shape
    tn = 1536
    return pl.pallas_call(
        _mod_kernel,
        grid=(depth, n // tn),
        in_specs=[pl.BlockSpec((8, d), lambda l, j: (0, 0)),
                  pl.BlockSpec((1, d, tn), lambda l, j: (l, 0, j)),
                  pl.BlockSpec((1, 1, tn), lambda l, j: (l, 0, j))],
        out_specs=pl.BlockSpec((1, 8, tn), lambda l, j: (l, 0, j)),
        out_shape=jax.ShapeDtypeStruct((depth, 8, n), F32),
        compiler_params=_cparams(("arbitrary", "arbitrary")),
        name="mod_vectors",
    )(cs, w_mod, b_mod.reshape(depth, 1, n))


def _inproj_kernel(x_ref, g_ref, mod_ref, w_ref, cos_ref, sin_ref,
                   q_ref, k_ref, v_ref, us_ref, up_ref):
    tm = x_ref.shape[0]
    h = _norm_mod(x_ref[...], g_ref[...], mod_ref[0, 0:1, :], mod_ref[0, 1:2, :])
    p = jnp.dot(h.astype(BF16), w_ref[...], preferred_element_type=F32)
    c = cos_ref[...]
    s = sin_ref[...]
    lane = lax.broadcasted_iota(jnp.int32, (tm, LANE), 1)
    first = (lane % 32) < 16
    for off, ref, sc in ((0, q_ref, ATTN_HD ** -0.5), (ATTN_W, k_ref, 1.0)):
        for blk in range(ATTN_W // LANE):
            xb = p[:, off + LANE * blk: off + LANE * (blk + 1)]
            partner = jnp.where(first, pltpu.roll(xb, LANE - 16, 1), pltpu.roll(xb, 16, 1))
            ref[:, LANE * blk: LANE * (blk + 1)] = ((xb * c + partner * s) * sc).astype(BF16)
    v_ref[...] = p[:, 2 * ATTN_W:SSM_OFF].astype(BF16)
    us_ref[...] = p[:, SSM_OFF:POOL_OFF]
    up_ref[...] = p[:, POOL_OFF:IN_W]


def _seg_map(tiles_per_batch, nb):
    def seg(i):
        return jnp.minimum(i // tiles_per_batch, nb)
    return seg


def _inproj(x, g, mods, w, cos_t, sin_t, tiles_per_batch, nb):
    t, d = x.shape
    tm = ROW_TILE
    seg = _seg_map(tiles_per_batch, nb)
    row = lambda i: (i, 0)
    return pl.pallas_call(
        _inproj_kernel,
        grid=(t // tm,),
        in_specs=[pl.BlockSpec((tm, d), row),
                  pl.BlockSpec((1, d), lambda i: (0, 0)),
                  pl.BlockSpec((1, N_MOD, d), lambda i: (seg(i), 0, 0)),
                  pl.BlockSpec((d, IN_W), lambda i: (0, 0)),
                  pl.BlockSpec((tm, LANE), row),
                  pl.BlockSpec((tm, LANE), row)],
        out_specs=[pl.BlockSpec((tm, ATTN_W), row),
                   pl.BlockSpec((tm, ATTN_W), row),
                   pl.BlockSpec((tm, ATTN_W), row),
                   pl.BlockSpec((tm, SSM_W), row),
                   pl.BlockSpec((tm, POOL_W), row)],
        out_shape=[jax.ShapeDtypeStruct((t, ATTN_W), BF16),
                   jax.ShapeDtypeStruct((t, ATTN_W), BF16),
                   jax.ShapeDtypeStruct((t, ATTN_W), BF16),
                   jax.ShapeDtypeStruct((t, SSM_W), F32),
                   jax.ShapeDtypeStruct((t, POOL_W), F32)],
        compiler_params=_cparams(("arbitrary",)),
        name="inproj",
    )(x, g, mods, w, cos_t, sin_t)


def _attn_kernel(lam_ref, q_ref, kl_ref, kc_ref, vl_ref, vc_ref, g_ref, o_ref,
                 *, n_lat_q, n_lat_k, tk, out_scale):
    tq = q_ref.shape[0]
    i = pl.program_id(2)
    q = q_ref[...]
    lane = lax.broadcasted_iota(jnp.int32, (tq, LANE), 1)
    zero = jnp.zeros_like(q)
    qs = jnp.concatenate([jnp.where(lane < ATTN_HD, q, zero),
                          jnp.where(lane >= ATTN_HD, q, zero)], axis=0)

    def update(carry, kb, vb):
        m, l, acc = carry
        s = lax.dot_general(qs, kb, (((1,), (1,)), ((), ())), preferred_element_type=F32)
        m_new = jnp.maximum(m, jnp.max(s, axis=-1, keepdims=True))
        alpha = jnp.exp(m - m_new)
        p = jnp.exp(s - m_new)
        l = alpha * l + jnp.sum(p, axis=-1, keepdims=True)
        acc = alpha * acc + jnp.dot(p.astype(BF16), vb, preferred_element_type=F32)
        return m_new, l, acc

    carry = (jnp.full((2 * tq, 1), NEG_BIG, F32), jnp.zeros((2 * tq, 1), F32),
             jnp.zeros((2 * tq, ATTN_VD), F32))
    carry = update(carry, kc_ref[...], vc_ref[...])

    def body(j, carry):
        start = pl.multiple_of(j * tk, tk)
        return update(carry, kl_ref[pl.ds(start, tk), :], vl_ref[pl.ds(start, tk), :])

    n_steps = jnp.where(i < n_lat_q, n_lat_k, 0)
    m, l, acc = lax.fori_loop(0, n_steps, body, carry)
    o = acc / l
    o = o[:tq] - lam_ref[0] * o[tq:]
    ms = jnp.mean(o * o, axis=-1, keepdims=True)
    o_ref[...] = (o * lax.rsqrt(ms + EPS) * g_ref[...] * out_scale).astype(BF16)


def _attention(q, k, v, lam, g, nb, l_lat, l_ctx, with_ctx_queries, out_scale):
    tq = ROW_TILE
    tk = 512 if l_lat % 512 == 0 else l_lat
    assert l_ctx == tq and l_lat % tq == 0
    n_lat_q = l_lat // tq
    nq = n_lat_q + (1 if with_ctx_queries else 0)
    ctx_blk0 = nb * l_lat // l_ctx

    def q_map(b, h, i):
        return (jnp.where(i < n_lat_q, b * n_lat_q + i, ctx_blk0 + b), h)

    lat_map = lambda b, h, i: (b, h)
    ctx_map = lambda b, h, i: (ctx_blk0 + b, h)
    t_out = nb * l_lat + (nb * l_ctx if with_ctx_queries else 0)
    kern = functools.partial(_attn_kernel, n_lat_q=n_lat_q, n_lat_k=l_lat // tk, tk=tk,
                             out_scale=out_scale)
    return pl.pallas_call(
        kern,
        grid=(nb, ATTN_HEADS, nq),
        in_specs=[pl.BlockSpec(memory_space=pltpu.SMEM),
                  pl.BlockSpec((tq, LANE), q_map),
                  pl.BlockSpec((l_lat, LANE), lat_map),
                  pl.BlockSpec((l_ctx, LANE), ctx_map),
                  pl.BlockSpec((l_lat, LANE), lat_map),
                  pl.BlockSpec((l_ctx, LANE), ctx_map),
                  pl.BlockSpec((1, LANE), lambda b, h, i: (0, 0))],
        out_specs=pl.BlockSpec((tq, LANE), q_map),
        out_shape=jax.ShapeDtypeStruct((t_out, ATTN_W), BF16),
        compiler_params=_cparams(("arbitrary", "arbitrary", "arbitrary")),
        name="diff_attention",
    )(lam, q, k, k, v, v, g)


def _ssm_kernel(ul_ref, uc_ref, m_ref, bm_ref, cm_ref, a_ref, d_ref, yl_ref, yc_ref,
                s_scr, h_scr, *, ctx_out):
    nl = ul_ref.shape[2]
    nc = uc_ref.shape[2]
    half = s_scr.shape[-1] // 2
    ul = ul_ref[0, 0]
    uc = uc_ref[0, 0]
    ulb = ul.astype(BF16)
    ucb = uc.astype(BF16)
    d = d_ref[0]
    yl = jnp.dot(ulb, m_ref[0], preferred_element_type=F32) + ul * d
    if ctx_out:
        yc = jnp.dot(ucb, m_ref[0], preferred_element_type=F32) + uc * d
    for dr in range(2):
        s_scr[dr, 0:nc, :] = jnp.dot(ucb, bm_ref[dr, 0], preferred_element_type=F32)
        s_scr[dr, nc:nc + nl, :] = jnp.dot(ulb, bm_ref[dr, 0], preferred_element_type=F32)
    a = a_ref[0]
    afr, afi, arr, ari = a[0:1], a[1:2], a[2:3], a[3:4]

    def make_body(base, n):
        def body(t, carry):
            fr, fi, rr, ri = carry
            rf = base + t
            rv = base + n - 1 - t
            h_scr[0, pl.ds(rf, 1), :] = jnp.concatenate([fr, fi], axis=-1)
            h_scr[1, pl.ds(rv, 1), :] = jnp.concatenate([rr, ri], axis=-1)
            sf = s_scr[0, pl.ds(rf, 1), :]
            sv = s_scr[1, pl.ds(rv, 1), :]
            nfr = afr * fr - afi * fi + sf[:, :half]
            nfi = afr * fi + afi * fr + sf[:, half:]
            nrr = arr * rr - ari * ri + sv[:, :half]
            nri = arr * ri + ari * rr + sv[:, half:]
            return nfr, nfi, nrr, nri
        return body

    z = jnp.zeros((1, half), F32)
    carry = lax.fori_loop(0, nc, make_body(0, nc), (z, z, z, z))
    lax.fori_loop(0, nl, make_body(nc, nl), carry)
    for dr in range(2):
        yl = yl + jnp.dot(h_scr[dr, nc:nc + nl, :].astype(BF16), cm_ref[dr, 0],
                          preferred_element_type=F32)
        if ctx_out:
            yc = yc + jnp.dot(h_scr[dr, 0:nc, :].astype(BF16), cm_ref[dr, 0],
                              preferred_element_type=F32)
    yl_ref[0, 0] = yl
    if ctx_out:
        yc_ref[0, 0] = yc
    else:
        yc_ref[0, 0] = jnp.zeros_like(uc)


def _ssm_matrices(lam_re, lam_im, log_step, b_re, b_im, c_re, c_im, d):
    tc = SSM_CHUNK
    g, p, hh = SSM_GROUPS, SSM_STATE, SSM_GROUP
    npair = g // 2
    step = jnp.exp(log_step)[..., None]
    den = lam_re * lam_re + lam_im * lam_im

    def power(k):
        er = jnp.exp(lam_re * step * k)
        return er * jnp.cos(lam_im * step * k), er * jnp.sin(lam_im * step * k)

    ar, ai = power(1.0)
    nr = ar - 1.0
    cr_ = (nr * lam_re + ai * lam_im) / den
    ci_ = (ai * lam_re - nr * lam_im) / den
    bbr = cr_[..., None] * b_re - ci_[..., None] * b_im
    bbi = cr_[..., None] * b_im + ci_[..., None] * b_re
    ks = jnp.arange(tc + 1, dtype=F32)
    pw = jax.vmap(power)(ks)
    pwr, pwi = pw
    cpr = c_re[None] * pwr[:, :, :, None, :] - c_im[None] * pwi[:, :, :, None, :]
    cpi = c_re[None] * pwi[:, :, :, None, :] + c_im[None] * pwr[:, :, :, None, :]
    hp = lax.Precision.HIGHEST
    kk = (jnp.einsum('kdgop,dgpi->kdgoi', cpr[:tc], bbr, precision=hp)
          - jnp.einsum('kdgop,dgpi->kdgoi', cpi[:tc], bbi, precision=hp))
    s_idx = jnp.arange(tc)[:, None]
    t_idx = jnp.arange(tc)[None, :]
    lag_f = t_idx - s_idx
    lag_r = s_idx - t_idx
    mf = jnp.where((lag_f >= 0)[:, :, None, None, None], kk[jnp.clip(lag_f, 0, tc - 1), 0], 0.0)
    mr = jnp.where((lag_r >= 0)[:, :, None, None, None], kk[jnp.clip(lag_r, 0, tc - 1), 1], 0.0)
    mm = mf + mr
    mm = jnp.transpose(mm, (2, 0, 4, 1, 3)).reshape(g, tc * hh, tc * hh)
    zero = jnp.zeros_like(mm[0::2])
    m2 = jnp.concatenate([jnp.concatenate([mm[0::2], zero], -1),
                          jnp.concatenate([zero, mm[1::2]], -1)], -2)
    def bmat(dr, exps):
        pr = pwr[exps, dr]
        pi = pwi[exps, dr]
        re = pr[..., None] * bbr[dr][None] - pi[..., None] * bbi[dr][None]
        im = pr[..., None] * bbi[dr][None] + pi[..., None] * bbr[dr][None]
        re = jnp.transpose(re, (1, 0, 3, 2)).reshape(g, tc * hh, p)
        im = jnp.transpose(im, (1, 0, 3, 2)).reshape(g, tc * hh, p)
        z = jnp.zeros_like(re[0::2])
        top = jnp.concatenate([re[0::2], z, im[0::2], z], -1)
        bot = jnp.concatenate([z, re[1::2], z, im[1::2]], -1)
        return jnp.concatenate([top, bot], -2)
    bm = jnp.stack([bmat(0, jnp.arange(tc - 1, -1, -1)), bmat(1, jnp.arange(tc))])
    def cmat(dr, exps):
        re = jnp.transpose(cpr[exps, dr], (1, 3, 0, 2)).reshape(g, p, tc * hh)
        im = jnp.transpose(cpi[exps, dr], (1, 3, 0, 2)).reshape(g, p, tc * hh)
        z = jnp.zeros_like(re[0::2])
        return jnp.concatenate([jnp.concatenate([re[0::2], z], -1),
                                jnp.concatenate([z, re[1::2]], -1),
                                jnp.concatenate([-im[0::2], z], -1),
                                jnp.concatenate([z, -im[1::2]], -1)], -2)
    cm = jnp.stack([cmat(0, jnp.arange(1, tc + 1)), cmat(1, jnp.arange(tc, 0, -1))])
    a16r = pwr[tc].reshape(2, npair, 2 * p)
    a16i = pwi[tc].reshape(2, npair, 2 * p)
    a16 = jnp.stack([a16r[0], a16i[0], a16r[1], a16i[1]], axis=1)
    dd = jnp.broadcast_to(d.reshape(npair, 2, 1, hh), (npair, 2, tc, hh)).reshape(npair, 1, 2 * tc * hh)
    return m2.astype(BF16), bm.astype(BF16), cm.astype(BF16), a16, dd


def _ssm(us, mats, nb, l_lat, l_ctx, ctx_out):
    m2, bm, cm, a16, dd = mats
    tc, hh = SSM_CHUNK, SSM_GROUP
    npair = SSM_GROUPS // 2
    w = 2 * tc * hh

    def fold(u, n):
        u = u.reshape(nb, n // tc, tc, npair, 2, hh)
        return jnp.transpose(u, (0, 3, 1, 4, 2, 5)).reshape(nb, npair, n // tc, w)

    def unfold(y, n):
        y = y.reshape(nb, npair, n // tc, 2, tc, hh)
        return jnp.transpose(y, (0, 2, 4, 1, 3, 5)).reshape(nb * n, SSM_W)

    ul = fold(us[:nb * l_lat], l_lat)
    uc = fold(us[nb * l_lat:], l_ctx)
    nl, nc = l_lat // tc, l_ctx // tc
    yl, yc = pl.pallas_call(
        functools.partial(_ssm_kernel, ctx_out=ctx_out),
        grid=(nb, npair),
        in_specs=[pl.BlockSpec((1, 1, nl, w), lambda b, q: (b, q, 0, 0)),
                  pl.BlockSpec((1, 1, nc, w), lambda b, q: (b, q, 0, 0)),
                  pl.BlockSpec((1, w, w), lambda b, q: (q, 0, 0)),
                  pl.BlockSpec((2, 1, w, 4 * SSM_STATE), lambda b, q: (0, q, 0, 0)),
                  pl.BlockSpec((2, 1, 4 * SSM_STATE, w), lambda b, q: (0, q, 0, 0)),
                  pl.BlockSpec((1, 4, 2 * SSM_STATE), lambda b, q: (q, 0, 0)),
                  pl.BlockSpec((1, 1, w), lambda b, q: (q, 0, 0))],
        out_specs=[pl.BlockSpec((1, 1, nl, w), lambda b, q: (b, q, 0, 0)),
                   pl.BlockSpec((1, 1, nc, w), lambda b, q: (b, q, 0, 0))],
        out_shape=[jax.ShapeDtypeStruct((nb, npair, nl, w), F32),
                   jax.ShapeDtypeStruct((nb, npair, nc, w), F32)],
        scratch_shapes=[pltpu.VMEM((2, nc + nl, 4 * SSM_STATE), F32),
                        pltpu.VMEM((2, nc + nl, 4 * SSM_STATE), F32)],
        compiler_params=_cparams(("arbitrary", "arbitrary")),
        name="s5_scan",
    )(ul, uc, m2, bm, cm, a16, dd)
    y = unfold(yl, l_lat)
    if ctx_out:
        y = jnp.concatenate([y, unfold(yc, l_ctx)], axis=0)
    return y


def _pool_kernel(prev_ref, cur_ref, next_ref, w_ref, scale_ref, o_ref, ext,
                 *, n_lat_tiles, tiles_per_lat, l_lat, l_ctx):
    r = cur_ref.shape[0]
    hl = POOL_HALO
    i = pl.program_id(0)
    is_lat = i < n_lat_tiles
    pos = jnp.where(is_lat, i % tiles_per_lat, 0)
    n_tiles = jnp.where(is_lat, tiles_per_lat, l_ctx // r)
    seq_len = jnp.where(is_lat, l_lat, l_ctx)
    zeros = jnp.zeros((hl, POOL_W), F32)
    ext[0:hl, :] = jnp.where(pos > 0, prev_ref[...], zeros)
    ext[hl:hl + r, :] = cur_ref[...]
    ext[hl + r:hl + r + hl, :] = jnp.where(pos < n_tiles - 1, next_ref[...], zeros)
    u = cur_ref[...]
    t = pos * r + lax.broadcasted_iota(jnp.int32, (r, 1), 0)
    lane = lax.broadcasted_iota(jnp.int32, (r, POOL_W), 1)

    def win(k):
        return ext[hl + k:hl + k + r, :]

    acc = win(-1) + u
    mean = jnp.zeros((r, POOL_W), F32)
    lo_k, hi_k = -1, 0
    for gi, wn in enumerate(POOL_WINDOWS):
        hw = wn // 2
        while lo_k > -hw:
            lo_k -= 1
            acc = acc + win(lo_k)
        while hi_k < hw - 1:
            hi_k += 1
            acc = acc + win(hi_k)
        cnt = (jnp.minimum(t + hw, seq_len) - jnp.maximum(t - hw, 0)).astype(F32)
        sel = (lane >= gi * POOL_GROUP) & (lane < (gi + 1) * POOL_GROUP)
        mean = jnp.where(sel, acc / cnt, mean)
    dlt = (mean - u).astype(BF16)
    o_ref[...] = jnp.dot(dlt, w_ref[...], preferred_element_type=F32) * scale_ref[...]


def _pool(up, w_blk, scale, nb, l_lat, l_ctx, n_rows):
    r = ROW_TILE
    hl = POOL_HALO
    n_tiles = n_rows // r
    last8 = up.shape[0] // hl - 1
    kern = functools.partial(_pool_kernel, n_lat_tiles=nb * l_lat // r, tiles_per_lat=l_lat // r,
                             l_lat=l_lat, l_ctx=l_ctx)
    return pl.pallas_call(
        kern,
        grid=(n_tiles,),
        in_specs=[pl.BlockSpec((hl, POOL_W), lambda i: (jnp.maximum(i * (r // hl) - 1, 0), 0)),
                  pl.BlockSpec((r, POOL_W), lambda i: (i, 0)),
                  pl.BlockSpec((hl, POOL_W), lambda i: (jnp.minimum((i + 1) * (r // hl), last8), 0)),
                  pl.BlockSpec((POOL_W, POOL_W), lambda i: (0, 0)),
                  pl.BlockSpec((1, POOL_W), lambda i: (0, 0))],
        out_specs=pl.BlockSpec((r, POOL_W), lambda i: (i, 0)),
        out_shape=jax.ShapeDtypeStruct((n_rows, POOL_W), F32),
        scratch_shapes=[pltpu.VMEM((r + 2 * hl, POOL_W), F32)],
        compiler_params=_cparams(("arbitrary",)),
        name="pool_mix",
    )(up, up, up, w_blk, scale)


def _mixout_kernel(x_ref, attn_ref, y_ref, pool_ref, mod_ref, wglu_ref, wout_ref, o_ref):
    yg = _gelu(y_ref[...])
    z = yg * jax.nn.sigmoid(jnp.dot(yg.astype(BF16), wglu_ref[...], preferred_element_type=F32))
    r = jnp.dot(attn_ref[...], wout_ref[0:ATTN_W, :], preferred_element_type=F32)
    r = r + jnp.dot(z.astype(BF16), wout_ref[ATTN_W:ATTN_W + SSM_W, :], preferred_element_type=F32)
    r = r + jnp.dot(pool_ref[...].astype(BF16), wout_ref[ATTN_W + SSM_W:, :],
                    preferred_element_type=F32)
    o_ref[...] = x_ref[...] + mod_ref[0, 2:3, :] * r


def _mixout(x, attn, y, pool, mods, wglu, wout, tiles_per_batch, nb, n_rows):
    d = x.shape[1]
    tm = ROW_TILE
    seg = _seg_map(tiles_per_batch, nb)
    row = lambda i: (i, 0)
    return pl.pallas_call(
        _mixout_kernel,
        grid=(n_rows // tm,),
        in_specs=[pl.BlockSpec((tm, d), row),
                  pl.BlockSpec((tm, ATTN_W), row),
                  pl.BlockSpec((tm, SSM_W), row),
                  pl.BlockSpec((tm, POOL_W), row),
                  pl.BlockSpec((1, N_MOD, d), lambda i: (seg(i), 0, 0)),
                  pl.BlockSpec((SSM_W, SSM_W), lambda i: (0, 0)),
                  pl.BlockSpec((d, d), lambda i: (0, 0))],
        out_specs=pl.BlockSpec((tm, d), row),
        out_shape=jax.ShapeDtypeStruct((n_rows, d), F32),
        compiler_params=_cparams(("arbitrary",)),
        name="mix_out",
    )(x, attn, y, pool, mods, wglu, wout)


def _top_sorted(s, k):
    rows = []
    for _ in range(k):
        m = jnp.max(s, axis=0, keepdims=True)
        rows.append(m)
        s = jnp.where(s == m, NEG_BIG, s)
    return rows


def _peer_score_kernel(x_ref, g_ref, mod_ref, wq_ref, keys_ref,
                       h_ref, s1_ref, s2_ref, e1_ref, e2_ref, tau_ref, q_scr):
    tm = x_ref.shape[0]
    h = _norm_mod(x_ref[...], g_ref[...], mod_ref[0, 3:4, :], mod_ref[0, 4:5, :]).astype(BF16)
    h_ref[...] = h
    q = jnp.dot(h, wq_ref[...], preferred_element_type=F32)
    for hc in range(2 * PEER_HEADS):
        q_scr[hc] = q[:, hc * PEER_KDIM:(hc + 1) * PEER_KDIM].astype(BF16)
    row8 = lax.broadcasted_iota(jnp.int32, (8, tm), 0)

    def head(hd, _):
        nt = (((1,), (1,)), ((), ()))
        s1 = lax.dot_general(keys_ref[hd, 0], q_scr[2 * hd], nt,
                             preferred_element_type=F32)
        s2 = lax.dot_general(keys_ref[hd, 1], q_scr[2 * hd + 1], nt,
                             preferred_element_type=F32)
        a = _top_sorted(s1, PEER_TOPK)
        b = _top_sorted(s2, PEER_TOPK)
        acat = jnp.concatenate(a, axis=0)
        bcat = jnp.concatenate(b, axis=0)
        pieces = [a[0] + bcat]
        for i in range(2, 9):
            piece = a[i - 1] + bcat[0:8]
            n_valid = PEER_TOPK // i
            pieces.append(piece if n_valid >= 8 else jnp.where(row8 < n_valid, piece, NEG_BIG))
        pieces.append(acat[8:16] + b[0])
        cand = jnp.concatenate(pieces, axis=0)
        work = cand
        cum = jnp.zeros((1, tm), F32)
        tau = jnp.full((1, tm), NEG_BIG, F32)
        for _ in range(PEER_TOPK):
            m = jnp.max(work, axis=0, keepdims=True)
            eq = work == m
            new = cum + jnp.sum(eq.astype(F32), axis=0, keepdims=True)
            tau = jnp.where((cum < PEER_TOPK) & (new >= PEER_TOPK), m, tau)
            work = jnp.where(eq, NEG_BIG, work)
            cum = new
        top = a[0] + b[0]
        z = jnp.sum(jnp.where(cand >= tau, jnp.exp(cand - top), 0.0), axis=0, keepdims=True)
        s1_ref[hd] = s1
        s2_ref[hd] = s2
        e1_ref[hd] = jnp.exp(s1 - a[0])
        e2_ref[hd] = jnp.exp(s2 - b[0]) * (1.0 / z)
        tau_ref[pl.ds(hd, 1), :] = tau
        return 0

    lax.fori_loop(0, PEER_HEADS, head, 0)


def _peer_scores(x, g, mods, wq, keys, tiles_per_batch, nb, n_rows):
    d = x.shape[1]
    tm = ROW_TILE
    seg = _seg_map(tiles_per_batch, nb)
    nq = wq.shape[1]
    col = lambda i: (0, 0, i)
    big = jax.ShapeDtypeStruct((PEER_HEADS, PEER_NKEYS, n_rows), F32)
    return pl.pallas_call(
        _peer_score_kernel,
        grid=(n_rows // tm,),
        in_specs=[pl.BlockSpec((tm, d), lambda i: (i, 0)),
                  pl.BlockSpec((1, d), lambda i: (0, 0)),
                  pl.BlockSpec((1, N_MOD, d), lambda i: (seg(i), 0, 0)),
                  pl.BlockSpec((d, nq), lambda i: (0, 0)),
                  pl.BlockSpec((PEER_HEADS, 2, PEER_NKEYS, PEER_KDIM), lambda i: (0, 0, 0, 0))],
        out_specs=[pl.BlockSpec((tm, d), lambda i: (i, 0)),
                   pl.BlockSpec((PEER_HEADS, PEER_NKEYS, tm), col),
                   pl.BlockSpec((PEER_HEADS, PEER_NKEYS, tm), col),
                   pl.BlockSpec((PEER_HEADS, PEER_NKEYS, tm), col),
                   pl.BlockSpec((PEER_HEADS, PEER_NKEYS, tm), col),
                   pl.BlockSpec((PEER_HEADS, tm), lambda i: (0, i))],
        out_shape=[jax.ShapeDtypeStruct((n_rows, d), BF16), big, big, big, big,
                   jax.ShapeDtypeStruct((PEER_HEADS, n_rows), F32)],
        scratch_shapes=[pltpu.VMEM((2 * PEER_HEADS, tm, PEER_KDIM), BF16)],
        compiler_params=_cparams(("arbitrary",)),
        name="peer_scores",
    )(x, g, mods, wq, keys)


def _peer_dense_kernel(h_ref, x_ref, mod_ref, u_ref, vt_ref, s1_ref, s2_ref, e1_ref, e2_ref,
                       tau_ref, fg_ref, o_ref, acc_ref, a_scr, w_scr, *, n_i, final_norm):
    tm = h_ref.shape[0]
    e = pl.program_id(1)

    @pl.when(e == 0)
    def _():
        acc_ref[...] = jnp.zeros_like(acc_ref)

    a_scr[...] = lax.dot_general(u_ref[...], h_ref[...], (((1,), (1,)), ((), ())),
                                 preferred_element_type=F32)

    i0 = pl.multiple_of(e * n_i, n_i)
    for lb in range(tm // LANE):
        ls = slice(lb * LANE, (lb + 1) * LANE)
        blocks = [(s1_ref[hd, pl.ds(i0, n_i), ls], e1_ref[hd, pl.ds(i0, n_i), ls],
                   tau_ref[hd:hd + 1, ls]) for hd in range(PEER_HEADS)]
        for ii in range(n_i):
            for jb in range(PEER_NKEYS // 16):
                js = slice(jb * 16, (jb + 1) * 16)
                gsum = jnp.zeros((16, LANE), F32)
                for hd in range(PEER_HEADS):
                    s1b, e1b, taur = blocks[hd]
                    cs = s1b[ii:ii + 1] + s2_ref[hd, js, ls]
                    gsum = gsum + jnp.where(cs >= taur, e2_ref[hd, js, ls] * e1b[ii:ii + 1], 0.0)
                rs = slice(ii * PEER_NKEYS + jb * 16, ii * PEER_NKEYS + (jb + 1) * 16)
                w_scr[rs, ls] = (_gelu(a_scr[rs, ls]) * gsum).astype(BF16)
    acc_ref[...] += jnp.dot(vt_ref[...], w_scr[...], preferred_element_type=F32)

    @pl.when(e == pl.num_programs(1) - 1)
    def _():
        out = x_ref[...] + mod_ref[0, 5:6, :] * acc_ref[...].T
        if final_norm:
            ms = jnp.mean(out * out, axis=-1, keepdims=True)
            out = out * lax.rsqrt(ms + EPS) * fg_ref[...]
        o_ref[...] = out


def _peer_dense(h, x, mods, u, vt, s1, s2, e1, e2, tau, fg, tiles_per_batch, nb, n_rows, final_norm):
    d = x.shape[1]
    tm = ROW_TILE
    n_i = 8
    ne = n_i * PEER_NKEYS
    n_exp = u.shape[0]
    seg = lambda i: jnp.minimum(i // tiles_per_batch, nb)
    col = lambda i, e: (0, 0, i)
    kern = functools.partial(_peer_dense_kernel, n_i=n_i, final_norm=final_norm)
    return pl.pallas_call(
        kern,
        grid=(n_rows // tm, n_exp // ne),
        in_specs=[pl.BlockSpec((tm, d), lambda i, e: (i, 0)),
                  pl.BlockSpec((tm, d), lambda i, e: (i, 0)),
                  pl.BlockSpec((1, N_MOD, d), lambda i, e: (seg(i), 0, 0)),
                  pl.BlockSpec((ne, d), lambda i, e: (e, 0)),
                  pl.BlockSpec((d, ne), lambda i, e: (0, e)),
                  pl.BlockSpec((PEER_HEADS, PEER_NKEYS, tm), col),
                  pl.BlockSpec((PEER_HEADS, PEER_NKEYS, tm), col),
                  pl.BlockSpec((PEER_HEADS, PEER_NKEYS, tm), col),
                  pl.BlockSpec((PEER_HEADS, PEER_NKEYS, tm), col),
                  pl.BlockSpec((PEER_HEADS, tm), lambda i, e: (0, i)),
                  pl.BlockSpec((1, d), lambda i, e: (0, 0))],
        out_specs=pl.BlockSpec((tm, d), lambda i, e: (i, 0)),
        out_shape=jax.ShapeDtypeStruct((n_rows, d), F32),
        scratch_shapes=[pltpu.VMEM((d, tm), F32),
                        pltpu.VMEM((ne, tm), F32),
                        pltpu.VMEM((ne, tm), BF16)],
        compiler_params=_cparams(("arbitrary", "arbitrary")),
        name="peer_experts",
    )(h, x, mods, u, vt, s1, s2, e1, e2, tau, fg)


def _rope_tables(l_lat, nb, n_ctx_rows):
    rows = l_lat // GRID_W
    r = jnp.repeat(jnp.arange(rows), GRID_W)
    col = jnp.tile(jnp.arange(GRID_W), rows)
    pos = jnp.stack([r, col], axis=-1).astype(F32)
    nf = ATTN_HD // 4
    inv = 1.0 / (ROPE_BASE ** (jnp.arange(nf, dtype=F32) / nf))
    ang = pos[:, :, None] * inv
    cos, sin = jnp.cos(ang), jnp.sin(ang)
    c64 = jnp.concatenate([cos[:, 0], cos[:, 0], cos[:, 1], cos[:, 1]], axis=-1)
    s64 = jnp.concatenate([-sin[:, 0], sin[:, 0], -sin[:, 1], sin[:, 1]], axis=-1)
    c = jnp.tile(c64, (nb, LANE // ATTN_HD))
    s = jnp.tile(s64, (nb, LANE // ATTN_HD))
    c = jnp.concatenate([c, jnp.ones((n_ctx_rows, LANE), F32)], axis=0)
    s = jnp.concatenate([s, jnp.zeros((n_ctx_rows, LANE), F32)], axis=0)
    return c, s


def kernel(x, c, ctx, c_ctx, w_mod, b_mod, norm1_g, norm2_g, w_in, w_out, lam_q1, lam_k1, lam_q2, lam_k2, subln_g, ssm_lambda_re, ssm_lambda_im, ssm_log_step, ssm_b_re, ssm_b_im, ssm_c_re, ssm_c_im, ssm_d, ssm_w_glu, pool_w, pool_scale, peer_wq, peer_keys, peer_u, peer_v, final_g):
    nb, l_lat, d = x.shape
    l_ctx = ctx.shape[1]
    depth = w_mod.shape[0]
    n_lat = nb * l_lat
    n_all = n_lat + nb * l_ctx
    tiles_per_batch = l_lat // ROW_TILE
    assert l_lat % ROW_TILE == 0 and l_ctx == ROW_TILE and nb + 1 <= 8

    cs = jnp.concatenate([c, c_ctx[None], jnp.zeros((8 - nb - 1, d), F32)], axis=0)
    mod_all = _mod_vectors(cs, w_mod, b_mod)
    cos_t, sin_t = _rope_tables(l_lat, nb, nb * l_ctx)
    xs = jnp.concatenate([x.reshape(n_lat, d), ctx.reshape(nb * l_ctx, d)], axis=0)

    for l in range(depth):
        last = l == depth - 1
        n_rows = n_lat if last else n_all
        mods = mod_all[l, :nb + 1].reshape(nb + 1, N_MOD, d)
        q, k, v, us, up = _inproj(xs, norm1_g[l][None], mods, w_in[l].astype(BF16), cos_t, sin_t,
                                  tiles_per_batch, nb)
        lam_init = 0.8 - 0.6 * math.exp(-0.3 * l)
        lam = (jnp.exp(jnp.sum(lam_q1[l] * lam_k1[l])) - jnp.exp(jnp.sum(lam_q2[l] * lam_k2[l]))
               + lam_init).reshape(1).astype(F32)
        attn = _attention(q, k, v, lam, subln_g[l][None], nb, l_lat, l_ctx, not last,
                          1.0 - lam_init)
        mats = _ssm_matrices(ssm_lambda_re[l], ssm_lambda_im[l], ssm_log_step[l], ssm_b_re[l],
                             ssm_b_im[l], ssm_c_re[l], ssm_c_im[l], ssm_d[l])
        y = _ssm(us, mats, nb, l_lat, l_ctx, not last)
        w_blk = jax.scipy.linalg.block_diag(*[pool_w[l, gi] for gi in range(len(POOL_WINDOWS))])
        pool = _pool(up, w_blk.astype(BF16), pool_scale[l][None], nb, l_lat, l_ctx, n_rows)
        xs = _mixout(xs, attn, y, pool, mods, ssm_w_glu[l].astype(BF16), w_out[l].astype(BF16),
                     tiles_per_batch, nb, n_rows)
        h2, s1, s2, e1, e2, tau = _peer_scores(xs, norm2_g[l][None], mods, peer_wq[l].astype(BF16),
                                               peer_keys[l].astype(BF16), tiles_per_batch, nb, n_rows)
        xs = _peer_dense(h2, xs, mods, peer_u[l].astype(BF16), peer_v[l].T.astype(BF16),
                         s1, s2, e1, e2, tau, final_g[None], tiles_per_batch, nb, n_rows, last)
    return xs.reshape(nb, l_lat, d)
```

```python
import functools
import math

import numpy as np
import jax
import jax.numpy as jnp
from jax import lax
from jax.experimental import pallas as pl
from jax.experimental.pallas import tpu as pltpu

F32 = jnp.float32
BF16 = jnp.bfloat16

EPS = 1e-6
GRID_W = 64
N_MOD = 6
ATTN_HD = 64
ATTN_VD = 128
ATTN_HEADS = 4
ATTN_W = 512
ROPE_BASE = 10000.0
SSM_W = 256
SSM_GROUP = 16
SSM_GROUPS = 16
SSM_STATE = 64
SSM_CHUNK = 16
POOL_W = 256
POOL_WINDOWS = (2, 4, 8, 16)
POOL_GROUP = 64
POOL_HALO = 8
SSM_OFF = 3 * ATTN_W
POOL_OFF = SSM_OFF + SSM_W
IN_W = POOL_OFF + POOL_W
PEER_HEADS = 8
PEER_NKEYS = 128
PEER_KDIM = 128
PEER_TOPK = 16
NEG_BIG = -3.0e38
SQRT_HALF = float(np.sqrt(0.5).astype(np.float32))

LANE = 128
ROW_TILE = 256
VMEM_LIMIT = 56 * 1024 * 1024


def _cparams(sem):
    return pltpu.CompilerParams(dimension_semantics=sem, vmem_limit_bytes=VMEM_LIMIT)


def _gelu(x):
    return 0.5 * x * (1.0 + lax.erf(x * SQRT_HALF))


def _norm_mod(x, g, shift, scale):
    ms = jnp.mean(x * x, axis=-1, keepdims=True)
    y = x * lax.rsqrt(ms + EPS) * g
    return y * (1.0 + scale) + shift


def _mod_kernel(s_ref, w_ref, b_ref, o_ref):
    s = s_ref[...]
    s = s * jax.nn.sigmoid(s)
    o_ref[0] = jnp.dot(s.astype(BF16), w_ref[0].astype(BF16), preferred_element_type=F32) + b_ref[0]


def _mod_vectors(cs, w_mod, b_mod):
    depth, d, n = w_mod.shape
    tn = 1536
    return pl.pallas_call(
        _mod_kernel,
        grid=(depth, n // tn),
        in_specs=[pl.BlockSpec((8, d), lambda l, j: (0, 0)),
                  pl.BlockSpec((1, d, tn), lambda l, j: (l, 0, j)),
                  pl.BlockSpec((1, 1, tn), lambda l, j: (l, 0, j))],
        out_specs=pl.BlockSpec((1, 8, tn), lambda l, j: (l, 0, j)),
        out_shape=jax.ShapeDtypeStruct((depth, 8, n), F32),
        compiler_params=_cparams(("arbitrary", "arbitrary")),
        name="mod_vectors",
    )(cs, w_mod, b_mod.reshape(depth, 1, n))


def _inproj_kernel(x_ref, g_ref, mod_ref, w_ref, cos_ref, sin_ref,
                   q_ref, k_ref, v_ref, us_ref, up_ref):
    tm = x_ref.shape[0]
    h = _norm_mod(x_ref[...], g_ref[...], mod_ref[0, 0:1, :], mod_ref[0, 1:2, :])
    p = jnp.dot(h.astype(BF16), w_ref[...], preferred_element_type=F32)
    c = cos_ref[...]
    s = sin_ref[...]
    lane = lax.broadcasted_iota(jnp.int32, (tm, LANE), 1)
    first = (lane % 32) < 16
    for off, ref, sc in ((0, q_ref, ATTN_HD ** -0.5), (ATTN_W, k_ref, 1.0)):
        for blk in range(ATTN_W // LANE):
            xb = p[:, off + LANE * blk: off + LANE * (blk + 1)]
            partner = jnp.where(first, pltpu.roll(xb, LANE - 16, 1), pltpu.roll(xb, 16, 1))
            ref[:, LANE * blk: LANE * (blk + 1)] = ((xb * c + partner * s) * sc).astype(BF16)
    v_ref[0] = p[:, 2 * ATTN_W:SSM_OFF].T.astype(BF16)
    us_ref[...] = p[:, SSM_OFF:POOL_OFF]
    up_ref[...] = p[:, POOL_OFF:IN_W]


def _seg_map(tiles_per_batch, nb):
    def seg(i):
        return jnp.minimum(i // tiles_per_batch, nb)
    return seg


def _inproj(x, g, mods, w, cos_t, sin_t, tiles_per_batch, nb):
    t, d = x.shape
    tm = ROW_TILE
    seg = _seg_map(tiles_per_batch, nb)
    row = lambda i: (i, 0)
    return pl.pallas_call(
        _inproj_kernel,
        grid=(t // tm,),
        in_specs=[pl.BlockSpec((tm, d), row),
                  pl.BlockSpec((1, d), lambda i: (0, 0)),
                  pl.BlockSpec((1, N_MOD, d), lambda i: (seg(i), 0, 0)),
                  pl.BlockSpec((d, IN_W), lambda i: (0, 0)),
                  pl.BlockSpec((tm, LANE), row),
                  pl.BlockSpec((tm, LANE), row)],
        out_specs=[pl.BlockSpec((tm, ATTN_W), row),
                   pl.BlockSpec((tm, ATTN_W), row),
                   pl.BlockSpec((1, ATTN_W, tm), lambda i: (i, 0, 0)),
                   pl.BlockSpec((tm, SSM_W), row),
                   pl.BlockSpec((tm, POOL_W), row)],
        out_shape=[jax.ShapeDtypeStruct((t, ATTN_W), BF16),
                   jax.ShapeDtypeStruct((t, ATTN_W), BF16),
                   jax.ShapeDtypeStruct((t // tm, ATTN_W, tm), BF16),
                   jax.ShapeDtypeStruct((t, SSM_W), F32),
                   jax.ShapeDtypeStruct((t, POOL_W), F32)],
        compiler_params=_cparams(("arbitrary",)),
        name="inproj",
    )(x, g, mods, w, cos_t, sin_t)


def _attn_stages(q, tq):
    lane = lax.broadcasted_iota(jnp.int32, (tq, LANE), 1)
    zero = jnp.zeros_like(q)
    qs = jnp.concatenate([jnp.where(lane < ATTN_HD, q, zero),
                          jnp.where(lane >= ATTN_HD, q, zero)], axis=0)

    def scores(kb):
        return lax.dot_general(kb, qs, (((1,), (1,)), ((), ())), preferred_element_type=F32)

    def softmax(m, l, s):
        m_new = jnp.maximum(m, jnp.max(s, axis=0, keepdims=True))
        alpha = jnp.exp(m - m_new)
        p = jnp.exp(s - m_new)
        l = alpha * l + jnp.sum(p, axis=0, keepdims=True)
        return m_new, l, alpha, p.astype(BF16)

    def weighted(acc, alpha, pb, vts):
        acc = alpha * acc
        rows = pb.shape[0] // len(vts)
        for c, vt in enumerate(vts):
            acc = acc + jnp.dot(vt, pb[c * rows:(c + 1) * rows], preferred_element_type=F32)
        return acc

    return scores, softmax, weighted


def _attn_finish(lam, m, l, acc, g, tq, out_scale):
    o = acc / l
    o = (o[:, :tq] - lam * o[:, tq:]).T
    ms = jnp.mean(o * o, axis=-1, keepdims=True)
    return (o * lax.rsqrt(ms + EPS) * g * out_scale).astype(BF16)


def _attn_ctx_kernel(lam_ref, q_ref, kc_ref, vc_ref, g_ref, o_ref, *, out_scale):
    tq = q_ref.shape[0]
    scores, softmax, weighted = _attn_stages(q_ref[...], tq)
    m = jnp.full((1, 2 * tq), NEG_BIG, F32)
    l = jnp.zeros((1, 2 * tq), F32)
    acc = jnp.zeros((ATTN_VD, 2 * tq), F32)
    m, l, alpha, pb = softmax(m, l, scores(kc_ref[...]))
    acc = weighted(acc, alpha, pb, [vc_ref[0]])
    o_ref[...] = _attn_finish(lam_ref[0], m, l, acc, g_ref[...], tq, out_scale)


def _attn_kernel(lam_ref, q_ref, kl_ref, kc_ref, vl_ref, vc_ref, g_ref, o_ref,
                 s_a, s_b, p_a, p_b, acc_ref, *, n_lat_k, tk, out_scale):
    tq = q_ref.shape[0]
    scores, softmax, weighted = _attn_stages(q_ref[...], tq)
    n_sub = tk // vc_ref.shape[-1]

    def stage_scores(t, s_buf):
        start = pl.multiple_of(t * tk, tk)
        s_buf[...] = scores(kl_ref[pl.ds(start, tk), :])

    def stage_softmax(m, l, s_buf, p_buf):
        m, l, alpha, pb = softmax(m, l, s_buf[...])
        p_buf[...] = pb
        return m, l, alpha

    def stage_values(alpha, p_buf, t):
        acc_ref[...] = weighted(acc_ref[...], alpha, p_buf[...],
                                [vl_ref[t * n_sub + c] for c in range(n_sub)])

    m = jnp.full((1, 2 * tq), NEG_BIG, F32)
    l = jnp.zeros((1, 2 * tq), F32)
    m, l, alpha, pb = softmax(m, l, scores(kc_ref[...]))
    acc_ref[...] = weighted(jnp.zeros((ATTN_VD, 2 * tq), F32), alpha, pb, [vc_ref[0]])

    if n_lat_k % 2 == 0:
        stage_scores(0, s_a)
        stage_scores(1, s_b)
        m, l, alpha = stage_softmax(m, l, s_a, p_a)

        def pair(i, carry):
            m, l, alpha = carry
            k = 2 * i
            stage_scores(k + 2, s_a)
            m, l, alpha_n = stage_softmax(m, l, s_b, p_b)
            stage_values(alpha, p_a, k)
            stage_scores(k + 3, s_b)
            m, l, alpha_nn = stage_softmax(m, l, s_a, p_a)
            stage_values(alpha_n, p_b, k + 1)
            return m, l, alpha_nn

        m, l, alpha = lax.fori_loop(0, n_lat_k // 2 - 1, pair, (m, l, alpha))
        m, l, alpha_n = stage_softmax(m, l, s_b, p_b)
        stage_values(alpha, p_a, n_lat_k - 2)
        stage_values(alpha_n, p_b, n_lat_k - 1)
    else:
        for t in range(n_lat_k):
            stage_scores(t, s_a)
            m, l, alpha = stage_softmax(m, l, s_a, p_a)
            stage_values(alpha, p_a, t)
    o_ref[...] = _attn_finish(lam_ref[0], m, l, acc_ref[...], g_ref[...], tq, out_scale)


def _attention(q, k, vt, lam, g, nb, l_lat, l_ctx, with_ctx_queries, out_scale):
    tq = ROW_TILE
    tk = 512 if l_lat % 512 == 0 else l_lat
    vt_tile = vt.shape[-1]
    assert l_ctx == tq and l_lat % tq == 0 and vt_tile == l_ctx and tk % vt_tile == 0
    n_lat_q = l_lat // tq
    n_vt_lat = l_lat // vt_tile
    ctx_blk0 = nb * l_lat // l_ctx
    smem = pl.BlockSpec(memory_space=pltpu.SMEM)
    gain = pl.BlockSpec((1, LANE), lambda b, h, i: (0, 0))
    kc_spec = pl.BlockSpec((l_ctx, LANE), lambda b, h, i: (ctx_blk0 + b, h))
    vc_spec = pl.BlockSpec((1, LANE, vt_tile), lambda b, h, i: (nb * n_vt_lat + b, h, 0))
    sem = _cparams(("arbitrary", "arbitrary", "arbitrary"))
    lat = pl.pallas_call(
        functools.partial(_attn_kernel, n_lat_k=l_lat // tk, tk=tk, out_scale=out_scale),
        grid=(nb, ATTN_HEADS, n_lat_q),
        in_specs=[smem,
                  pl.BlockSpec((tq, LANE), lambda b, h, i: (b * n_lat_q + i, h)),
                  pl.BlockSpec((l_lat, LANE), lambda b, h, i: (b, h)),
                  kc_spec,
                  pl.BlockSpec((n_vt_lat, LANE, vt_tile), lambda b, h, i: (b, h, 0)),
                  vc_spec, gain],
        out_specs=pl.BlockSpec((tq, LANE), lambda b, h, i: (b * n_lat_q + i, h)),
        out_shape=jax.ShapeDtypeStruct((nb * l_lat, ATTN_W), BF16),
        scratch_shapes=[pltpu.VMEM((tk, 2 * tq), F32), pltpu.VMEM((tk, 2 * tq), F32),
                        pltpu.VMEM((tk, 2 * tq), BF16), pltpu.VMEM((tk, 2 * tq), BF16),
                        pltpu.VMEM((ATTN_VD, 2 * tq), F32)],
        compiler_params=sem,
        name="diff_attention",
    )(lam, q, k, k, vt, vt, g)
    if not with_ctx_queries:
        return lat
    ctx = pl.pallas_call(
        functools.partial(_attn_ctx_kernel, out_scale=out_scale),
        grid=(nb, ATTN_HEADS, 1),
        in_specs=[smem,
                  pl.BlockSpec((tq, LANE), lambda b, h, i: (ctx_blk0 + b, h)),
                  kc_spec, vc_spec, gain],
        out_specs=pl.BlockSpec((tq, LANE), lambda b, h, i: (b, h)),
        out_shape=jax.ShapeDtypeStruct((nb * l_ctx, ATTN_W), BF16),
        compiler_params=sem,
        name="diff_attention_ctx",
    )(lam, q, k, vt, g)
    return jnp.concatenate([lat, ctx], axis=0)


def _ssm_kernel(ul_ref, uc_ref, m_ref, bm_ref, cm_ref, a_ref, d_ref, yl_ref, yc_ref,
                s_scr, h_scr, *, ctx_out):
    nl = ul_ref.shape[2]
    nc = uc_ref.shape[2]
    half = s_scr.shape[-1] // 2
    ul = ul_ref[0, 0]
    uc = uc_ref[0, 0]
    ulb = ul.astype(BF16)
    ucb = uc.astype(BF16)
    d = d_ref[0]
    yl = jnp.dot(ulb, m_ref[0], preferred_element_type=F32) + ul * d
    if ctx_out:
        yc = jnp.dot(ucb, m_ref[0], preferred_element_type=F32) + uc * d
    for dr in range(2):
        s_scr[dr, 0:nc, :] = jnp.dot(ucb, bm_ref[dr, 0], preferred_element_type=F32)
        s_scr[dr, nc:nc + nl, :] = jnp.dot(ulb, bm_ref[dr, 0], preferred_element_type=F32)
    a = a_ref[0]
    afr, afi, arr, ari = a[0:1], a[1:2], a[2:3], a[3:4]

    def make_body(base, n):
        def body(t, carry):
            fr, fi, rr, ri = carry
            rf = base + t
            rv = base + n - 1 - t
            h_scr[0, pl.ds(rf, 1), :] = jnp.concatenate([fr, fi], axis=-1)
            h_scr[1, pl.ds(rv, 1), :] = jnp.concatenate([rr, ri], axis=-1)
            sf = s_scr[0, pl.ds(rf, 1), :]
            sv = s_scr[1, pl.ds(rv, 1), :]
            nfr = afr * fr - afi * fi + sf[:, :half]
            nfi = afr * fi + afi * fr + sf[:, half:]
            nrr = arr * rr - ari * ri + sv[:, :half]
            nri = arr * ri + ari * rr + sv[:, half:]
            return nfr, nfi, nrr, nri
        return body

    z = jnp.zeros((1, half), F32)
    carry = lax.fori_loop(0, nc, make_body(0, nc), (z, z, z, z))
    lax.fori_loop(0, nl, make_body(nc, nl), carry)
    for dr in range(2):
        yl = yl + jnp.dot(h_scr[dr, nc:nc + nl, :].astype(BF16), cm_ref[dr, 0],
                          preferred_element_type=F32)
        if ctx_out:
            yc = yc + jnp.dot(h_scr[dr, 0:nc, :].astype(BF16), cm_ref[dr, 0],
                              preferred_element_type=F32)
    yl_ref[0, 0] = yl
    if ctx_out:
        yc_ref[0, 0] = yc
    else:
        yc_ref[0, 0] = jnp.zeros_like(uc)


def _ssm_matrices(lam_re, lam_im, log_step, b_re, b_im, c_re, c_im, d):
    tc = SSM_CHUNK
    g, p, hh = SSM_GROUPS, SSM_STATE, SSM_GROUP
    npair = g // 2
    step = jnp.exp(log_step)[..., None]
    den = lam_re * lam_re + lam_im * lam_im

    def power(k):
        er = jnp.exp(lam_re * step * k)
        return er * jnp.cos(lam_im * step * k), er * jnp.sin(lam_im * step * k)

    ar, ai = power(1.0)
    nr = ar - 1.0
    cr_ = (nr * lam_re + ai * lam_im) / den
    ci_ = (ai * lam_re - nr * lam_im) / den
    bbr = cr_[..., None] * b_re - ci_[..., None] * b_im
    bbi = cr_[..., None] * b_im + ci_[..., None] * b_re
    ks = jnp.arange(tc + 1, dtype=F32)
    pw = jax.vmap(power)(ks)
    pwr, pwi = pw
    cpr = c_re[None] * pwr[:, :, :, None, :] - c_im[None] * pwi[:, :, :, None, :]
    cpi = c_re[None] * pwi[:, :, :, None, :] + c_im[None] * pwr[:, :, :, None, :]
    hp = lax.Precision.HIGHEST
    kk = (jnp.einsum('kdgop,dgpi->kdgoi', cpr[:tc], bbr, precision=hp)
          - jnp.einsum('kdgop,dgpi->kdgoi', cpi[:tc], bbi, precision=hp))
    s_idx = jnp.arange(tc)[:, None]
    t_idx = jnp.arange(tc)[None, :]
    lag_f = t_idx - s_idx
    lag_r = s_idx - t_idx
    mf = jnp.where((lag_f >= 0)[:, :, None, None, None], kk[jnp.clip(lag_f, 0, tc - 1), 0], 0.0)
    mr = jnp.where((lag_r >= 0)[:, :, None, None, None], kk[jnp.clip(lag_r, 0, tc - 1), 1], 0.0)
    mm = mf + mr
    mm = jnp.transpose(mm, (2, 0, 4, 1, 3)).reshape(g, tc * hh, tc * hh)
    zero = jnp.zeros_like(mm[0::2])
    m2 = jnp.concatenate([jnp.concatenate([mm[0::2], zero], -1),
                          jnp.concatenate([zero, mm[1::2]], -1)], -2)
    def bmat(dr, exps):
        pr = pwr[exps, dr]
        pi = pwi[exps, dr]
        re = pr[..., None] * bbr[dr][None] - pi[..., None] * bbi[dr][None]
        im = pr[..., None] * bbi[dr][None] + pi[..., None] * bbr[dr][None]
        re = jnp.transpose(re, (1, 0, 3, 2)).reshape(g, tc * hh, p)
        im = jnp.transpose(im, (1, 0, 3, 2)).reshape(g, tc * hh, p)
        z = jnp.zeros_like(re[0::2])
        top = jnp.concatenate([re[0::2], z, im[0::2], z], -1)
        bot = jnp.concatenate([z, re[1::2], z, im[1::2]], -1)
        return jnp.concatenate([top, bot], -2)
    bm = jnp.stack([bmat(0, jnp.arange(tc - 1, -1, -1)), bmat(1, jnp.arange(tc))])
    def cmat(dr, exps):
        re = jnp.transpose(cpr[exps, dr], (1, 3, 0, 2)).reshape(g, p, tc * hh)
        im = jnp.transpose(cpi[exps, dr], (1, 3, 0, 2)).reshape(g, p, tc * hh)
        z = jnp.zeros_like(re[0::2])
        return jnp.concatenate([jnp.concatenate([re[0::2], z], -1),
                                jnp.concatenate([z, re[1::2]], -1),
                                jnp.concatenate([-im[0::2], z], -1),
                                jnp.concatenate([z, -im[1::2]], -1)], -2)
    cm = jnp.stack([cmat(0, jnp.arange(1, tc + 1)), cmat(1, jnp.arange(tc, 0, -1))])
    a16r = pwr[tc].reshape(2, npair, 2 * p)
    a16i = pwi[tc].reshape(2, npair, 2 * p)
    a16 = jnp.stack([a16r[0], a16i[0], a16r[1], a16i[1]], axis=1)
    dd = jnp.broadcast_to(d.reshape(npair, 2, 1, hh), (npair, 2, tc, hh)).reshape(npair, 1, 2 * tc * hh)
    return m2.astype(BF16), bm.astype(BF16), cm.astype(BF16), a16, dd


def _ssm(us, mats, nb, l_lat, l_ctx, ctx_out):
    m2, bm, cm, a16, dd = mats
    tc, hh = SSM_CHUNK, SSM_GROUP
    npair = SSM_GROUPS // 2
    w = 2 * tc * hh

    def fold(u, n):
        u = u.reshape(nb, n // tc, tc, npair, 2, hh)
        return jnp.transpose(u, (0, 3, 1, 4, 2, 5)).reshape(nb, npair, n // tc, w)

    def unfold(y, n):
        y = y.reshape(nb, npair, n // tc, 2, tc, hh)
        return jnp.transpose(y, (0, 2, 4, 1, 3, 5)).reshape(nb * n, SSM_W)

    ul = fold(us[:nb * l_lat], l_lat)
    uc = fold(us[nb * l_lat:], l_ctx)
    nl, nc = l_lat // tc, l_ctx // tc
    yl, yc = pl.pallas_call(
        functools.partial(_ssm_kernel, ctx_out=ctx_out),
        grid=(nb, npair),
        in_specs=[pl.BlockSpec((1, 1, nl, w), lambda b, q: (b, q, 0, 0)),
                  pl.BlockSpec((1, 1, nc, w), lambda b, q: (b, q, 0, 0)),
                  pl.BlockSpec((1, w, w), lambda b, q: (q, 0, 0)),
                  pl.BlockSpec((2, 1, w, 4 * SSM_STATE), lambda b, q: (0, q, 0, 0)),
                  pl.BlockSpec((2, 1, 4 * SSM_STATE, w), lambda b, q: (0, q, 0, 0)),
                  pl.BlockSpec((1, 4, 2 * SSM_STATE), lambda b, q: (q, 0, 0)),
                  pl.BlockSpec((1, 1, w), lambda b, q: (q, 0, 0))],
        out_specs=[pl.BlockSpec((1, 1, nl, w), lambda b, q: (b, q, 0, 0)),
                   pl.BlockSpec((1, 1, nc, w), lambda b, q: (b, q, 0, 0))],
        out_shape=[jax.ShapeDtypeStruct((nb, npair, nl, w), F32),
                   jax.ShapeDtypeStruct((nb, npair, nc, w), F32)],
        scratch_shapes=[pltpu.VMEM((2, nc + nl, 4 * SSM_STATE), F32),
                        pltpu.VMEM((2, nc + nl, 4 * SSM_STATE), F32)],
        compiler_params=_cparams(("arbitrary", "arbitrary")),
        name="s5_scan",
    )(ul, uc, m2, bm, cm, a16, dd)
    y = unfold(yl, l_lat)
    if ctx_out:
        y = jnp.concatenate([y, unfold(yc, l_ctx)], axis=0)
    return y


def _pool_kernel(prev_ref, cur_ref, next_ref, w_ref, scale_ref, o_ref, ext,
                 *, n_lat_tiles, tiles_per_lat, l_lat, l_ctx):
    r = cur_ref.shape[0]
    hl = POOL_HALO
    i = pl.program_id(0)
    is_lat = i < n_lat_tiles
    pos = jnp.where(is_lat, i % tiles_per_lat, 0)
    n_tiles = jnp.where(is_lat, tiles_per_lat, l_ctx // r)
    seq_len = jnp.where(is_lat, l_lat, l_ctx)
    zeros = jnp.zeros((hl, POOL_W), F32)
    ext[0:hl, :] = jnp.where(pos > 0, prev_ref[...], zeros)
    ext[hl:hl + r, :] = cur_ref[...]
    ext[hl + r:hl + r + hl, :] = jnp.where(pos < n_tiles - 1, next_ref[...], zeros)
    u = cur_ref[...]
    t = pos * r + lax.broadcasted_iota(jnp.int32, (r, 1), 0)
    lane = lax.broadcasted_iota(jnp.int32, (r, POOL_W), 1)

    def win(k):
        return ext[hl + k:hl + k + r, :]

    acc = win(-1) + u
    mean = jnp.zeros((r, POOL_W), F32)
    lo_k, hi_k = -1, 0
    for gi, wn in enumerate(POOL_WINDOWS):
        hw = wn // 2
        while lo_k > -hw:
            lo_k -= 1
            acc = acc + win(lo_k)
        while hi_k < hw - 1:
            hi_k += 1
            acc = acc + win(hi_k)
        cnt = (jnp.minimum(t + hw, seq_len) - jnp.maximum(t - hw, 0)).astype(F32)
        sel = (lane >= gi * POOL_GROUP) & (lane < (gi + 1) * POOL_GROUP)
        mean = jnp.where(sel, acc / cnt, mean)
    dlt = (mean - u).astype(BF16)
    o_ref[...] = jnp.dot(dlt, w_ref[...], preferred_element_type=F32) * scale_ref[...]


def _pool(up, w_blk, scale, nb, l_lat, l_ctx, n_rows):
    r = ROW_TILE
    hl = POOL_HALO
    n_tiles = n_rows // r
    last8 = up.shape[0] // hl - 1
    kern = functools.partial(_pool_kernel, n_lat_tiles=nb * l_lat // r, tiles_per_lat=l_lat // r,
                             l_lat=l_lat, l_ctx=l_ctx)
    return pl.pallas_call(
        kern,
        grid=(n_tiles,),
        in_specs=[pl.BlockSpec((hl, POOL_W), lambda i: (jnp.maximum(i * (r // hl) - 1, 0), 0)),
                  pl.BlockSpec((r, POOL_W), lambda i: (i, 0)),
                  pl.BlockSpec((hl, POOL_W), lambda i: (jnp.minimum((i + 1) * (r // hl), last8), 0)),
                  pl.BlockSpec((POOL_W, POOL_W), lambda i: (0, 0)),
                  pl.BlockSpec((1, POOL_W), lambda i: (0, 0))],
        out_specs=pl.BlockSpec((r, POOL_W), lambda i: (i, 0)),
        out_shape=jax.ShapeDtypeStruct((n_rows, POOL_W), F32),
        scratch_shapes=[pltpu.VMEM((r + 2 * hl, POOL_W), F32)],
        compiler_params=_cparams(("arbitrary",)),
        name="pool_mix",
    )(up, up, up, w_blk, scale)


def _mixout_kernel(x_ref, attn_ref, y_ref, pool_ref, mod_ref, wglu_ref, wout_ref, o_ref):
    yg = _gelu(y_ref[...])
    z = yg * jax.nn.sigmoid(jnp.dot(yg.astype(BF16), wglu_ref[...], preferred_element_type=F32))
    r = jnp.dot(attn_ref[...], wout_ref[0:ATTN_W, :], preferred_element_type=F32)
    r = r + jnp.dot(z.astype(BF16), wout_ref[ATTN_W:ATTN_W + SSM_W, :], preferred_element_type=F32)
    r = r + jnp.dot(pool_ref[...].astype(BF16), wout_ref[ATTN_W + SSM_W:, :],
                    preferred_element_type=F32)
    o_ref[...] = x_ref[...] + mod_ref[0, 2:3, :] * r


def _mixout(x, attn, y, pool, mods, wglu, wout, tiles_per_batch, nb, n_rows):
    d = x.shape[1]
    tm = ROW_TILE
    seg = _seg_map(tiles_per_batch, nb)
    row = lambda i: (i, 0)
    return pl.pallas_call(
        _mixout_kernel,
        grid=(n_rows // tm,),
        in_specs=[pl.BlockSpec((tm, d), row),
                  pl.BlockSpec((tm, ATTN_W), row),
                  pl.BlockSpec((tm, SSM_W), row),
                  pl.BlockSpec((tm, POOL_W), row),
                  pl.BlockSpec((1, N_MOD, d), lambda i: (seg(i), 0, 0)),
                  pl.BlockSpec((SSM_W, SSM_W), lambda i: (0, 0)),
                  pl.BlockSpec((d, d), lambda i: (0, 0))],
        out_specs=pl.BlockSpec((tm, d), row),
        out_shape=jax.ShapeDtypeStruct((n_rows, d), F32),
        compiler_params=_cparams(("arbitrary",)),
        name="mix_out",
    )(x, attn, y, pool, mods, wglu, wout)


def _top_sorted(s, k):
    rows = []
    for _ in range(k):
        m = jnp.max(s, axis=0, keepdims=True)
        rows.append(m)
        s = jnp.where(s == m, NEG_BIG, s)
    return rows


def _peer_score_kernel(x_ref, g_ref, mod_ref, wq_ref, keys_ref,
                       h_ref, s1_ref, s2_ref, e1_ref, e2_ref, tau_ref, q_scr):
    tm = x_ref.shape[0]
    h = _norm_mod(x_ref[...], g_ref[...], mod_ref[0, 3:4, :], mod_ref[0, 4:5, :]).astype(BF16)
    h_ref[...] = h
    q = jnp.dot(h, wq_ref[...], preferred_element_type=F32)
    for hc in range(2 * PEER_HEADS):
        q_scr[hc] = q[:, hc * PEER_KDIM:(hc + 1) * PEER_KDIM].astype(BF16)
    row8 = lax.broadcasted_iota(jnp.int32, (8, tm), 0)

    def head(hd, _):
        nt = (((1,), (1,)), ((), ()))
        s1 = lax.dot_general(keys_ref[hd, 0], q_scr[2 * hd], nt,
                             preferred_element_type=F32)
        s2 = lax.dot_general(keys_ref[hd, 1], q_scr[2 * hd + 1], nt,
                             preferred_element_type=F32)
        a = _top_sorted(s1, PEER_TOPK)
        b = _top_sorted(s2, PEER_TOPK)
        acat = jnp.concatenate(a, axis=0)
        bcat = jnp.concatenate(b, axis=0)
        pieces = [a[0] + bcat]
        for i in range(2, 9):
            piece = a[i - 1] + bcat[0:8]
            n_valid = PEER_TOPK // i
            pieces.append(piece if n_valid >= 8 else jnp.where(row8 < n_valid, piece, NEG_BIG))
        pieces.append(acat[8:16] + b[0])
        cand = jnp.concatenate(pieces, axis=0)
        work = cand
        cum = jnp.zeros((1, tm), F32)
        tau = jnp.full((1, tm), NEG_BIG, F32)
        for _ in range(PEER_TOPK):
            m = jnp.max(work, axis=0, keepdims=True)
            eq = work == m
            new = cum + jnp.sum(eq.astype(F32), axis=0, keepdims=True)
            tau = jnp.where((cum < PEER_TOPK) & (new >= PEER_TOPK), m, tau)
            work = jnp.where(eq, NEG_BIG, work)
            cum = new
        top = a[0] + b[0]
        z = jnp.sum(jnp.where(cand >= tau, jnp.exp(cand - top), 0.0), axis=0, keepdims=True)
        s1_ref[hd] = s1
        s2_ref[hd] = s2
        e1_ref[hd] = jnp.exp(s1 - a[0])
        e2_ref[hd] = jnp.exp(s2 - b[0]) * (1.0 / z)
        tau_ref[pl.ds(hd, 1), :] = tau
        return 0

    lax.fori_loop(0, PEER_HEADS, head, 0)


def _peer_scores(x, g, mods, wq, keys, tiles_per_batch, nb, n_rows):
    d = x.shape[1]
    tm = ROW_TILE
    seg = _seg_map(tiles_per_batch, nb)
    nq = wq.shape[1]
    col = lambda i: (0, 0, i)
    big = jax.ShapeDtypeStruct((PEER_HEADS, PEER_NKEYS, n_rows), F32)
    return pl.pallas_call(
        _peer_score_kernel,
        grid=(n_rows // tm,),
        in_specs=[pl.BlockSpec((tm, d), lambda i: (i, 0)),
                  pl.BlockSpec((1, d), lambda i: (0, 0)),
                  pl.BlockSpec((1, N_MOD, d), lambda i: (seg(i), 0, 0)),
                  pl.BlockSpec((d, nq), lambda i: (0, 0)),
                  pl.BlockSpec((PEER_HEADS, 2, PEER_NKEYS, PEER_KDIM), lambda i: (0, 0, 0, 0))],
        out_specs=[pl.BlockSpec((tm, d), lambda i: (i, 0)),
                   pl.BlockSpec((PEER_HEADS, PEER_NKEYS, tm), col),
                   pl.BlockSpec((PEER_HEADS, PEER_NKEYS, tm), col),
                   pl.BlockSpec((PEER_HEADS, PEER_NKEYS, tm), col),
                   pl.BlockSpec((PEER_HEADS, PEER_NKEYS, tm), col),
                   pl.BlockSpec((PEER_HEADS, tm), lambda i: (0, i))],
        out_shape=[jax.ShapeDtypeStruct((n_rows, d), BF16), big, big, big, big,
                   jax.ShapeDtypeStruct((PEER_HEADS, n_rows), F32)],
        scratch_shapes=[pltpu.VMEM((2 * PEER_HEADS, tm, PEER_KDIM), BF16)],
        compiler_params=_cparams(("arbitrary",)),
        name="peer_scores",
    )(x, g, mods, wq, keys)


def _peer_dense_kernel(h_ref, x_ref, mod_ref, u_ref, vt_ref, s1_ref, s2_ref, e1_ref, e2_ref,
                       tau_ref, fg_ref, o_ref, acc_ref, a_scr, w_scr, bs_scr, be_scr,
                       *, n_i, final_norm):
    tm = h_ref.shape[0]
    e = pl.program_id(1)

    @pl.when(e == 0)
    def _():
        acc_ref[...] = jnp.zeros_like(acc_ref)

    a_scr[...] = lax.dot_general(u_ref[...], h_ref[...], (((1,), (1,)), ((), ())),
                                 preferred_element_type=F32)

    i0 = pl.multiple_of(e * n_i, n_i)
    for hd in range(PEER_HEADS):
        s1b = s1_ref[hd, pl.ds(i0, n_i), :]
        e1b = e1_ref[hd, pl.ds(i0, n_i), :]
        for ii in range(n_i):
            bs_scr[hd, ii] = jnp.broadcast_to(s1b[ii:ii + 1], (8, tm))
            be_scr[hd, ii] = jnp.broadcast_to(e1b[ii:ii + 1], (8, tm))

    for lb in range(tm // LANE):
        ls = slice(lb * LANE, (lb + 1) * LANE)
        taus = [jnp.broadcast_to(tau_ref[hd:hd + 1, ls], (8, LANE)) for hd in range(PEER_HEADS)]

        def slab(jb, _):
            j0 = pl.multiple_of(jb * 16, 16)
            gs = [[jnp.zeros((8, LANE), F32), jnp.zeros((8, LANE), F32)] for _ in range(n_i)]
            for hd in range(PEER_HEADS):
                s2v = s2_ref[hd, pl.ds(j0, 16), ls]
                e2v = e2_ref[hd, pl.ds(j0, 16), ls]
                for ii in range(n_i):
                    bs = bs_scr[hd, ii, :, ls]
                    be = be_scr[hd, ii, :, ls]
                    for hf in range(2):
                        cs = bs + s2v[hf * 8:(hf + 1) * 8]
                        gs[ii][hf] = gs[ii][hf] + jnp.where(cs >= taus[hd],
                                                           e2v[hf * 8:(hf + 1) * 8] * be, 0.0)
            for ii in range(n_i):
                r0 = pl.multiple_of(ii * PEER_NKEYS + j0, 16)
                g16 = jnp.concatenate(gs[ii], axis=0)
                w_scr[pl.ds(r0, 16), ls] = (_gelu(a_scr[pl.ds(r0, 16), ls]) * g16).astype(BF16)
            return 0

        lax.fori_loop(0, PEER_NKEYS // 16, slab, 0)
    acc_ref[...] += jnp.dot(vt_ref[...], w_scr[...], preferred_element_type=F32)

    @pl.when(e == pl.num_programs(1) - 1)
    def _():
        out = x_ref[...] + mod_ref[0, 5:6, :] * acc_ref[...].T
        if final_norm:
            ms = jnp.mean(out * out, axis=-1, keepdims=True)
            out = out * lax.rsqrt(ms + EPS) * fg_ref[...]
        o_ref[...] = out


def _peer_dense(h, x, mods, u, vt, s1, s2, e1, e2, tau, fg, tiles_per_batch, nb, n_rows, final_norm):
    d = x.shape[1]
    tm = 2 * ROW_TILE
    n_i = 8
    ne = n_i * PEER_NKEYS
    tiles_per_batch = tiles_per_batch // 2
    n_exp = u.shape[0]
    seg = lambda i: jnp.minimum(i // tiles_per_batch, nb)
    col = lambda i, e: (0, 0, i)
    kern = functools.partial(_peer_dense_kernel, n_i=n_i, final_norm=final_norm)
    return pl.pallas_call(
        kern,
        grid=(n_rows // tm, n_exp // ne),
        in_specs=[pl.BlockSpec((tm, d), lambda i, e: (i, 0)),
                  pl.BlockSpec((tm, d), lambda i, e: (i, 0)),
                  pl.BlockSpec((1, N_MOD, d), lambda i, e: (seg(i), 0, 0)),
                  pl.BlockSpec((ne, d), lambda i, e: (e, 0)),
                  pl.BlockSpec((d, ne), lambda i, e: (0, e)),
                  pl.BlockSpec((PEER_HEADS, PEER_NKEYS, tm), col),
                  pl.BlockSpec((PEER_HEADS, PEER_NKEYS, tm), col),
                  pl.BlockSpec((PEER_HEADS, PEER_NKEYS, tm), col),
                  pl.BlockSpec((PEER_HEADS, PEER_NKEYS, tm), col),
                  pl.BlockSpec((PEER_HEADS, tm), lambda i, e: (0, i)),
                  pl.BlockSpec((1, d), lambda i, e: (0, 0))],
        out_specs=pl.BlockSpec((tm, d), lambda i, e: (i, 0)),
        out_shape=jax.ShapeDtypeStruct((n_rows, d), F32),
        scratch_shapes=[pltpu.VMEM((d, tm), F32),
                        pltpu.VMEM((ne, tm), F32),
                        pltpu.VMEM((ne, tm), BF16),
                        pltpu.VMEM((PEER_HEADS, n_i, 8, tm), F32),
                        pltpu.VMEM((PEER_HEADS, n_i, 8, tm), F32)],
        compiler_params=_cparams(("arbitrary", "arbitrary")),
        name="peer_experts",
    )(h, x, mods, u, vt, s1, s2, e1, e2, tau, fg)


def _rope_tables(l_lat, nb, n_ctx_rows):
    rows = l_lat // GRID_W
    r = jnp.repeat(jnp.arange(rows), GRID_W)
    col = jnp.tile(jnp.arange(GRID_W), rows)
    pos = jnp.stack([r, col], axis=-1).astype(F32)
    nf = ATTN_HD // 4
    inv = 1.0 / (ROPE_BASE ** (jnp.arange(nf, dtype=F32) / nf))
    ang = pos[:, :, None] * inv
    cos, sin = jnp.cos(ang), jnp.sin(ang)
    c64 = jnp.concatenate([cos[:, 0], cos[:, 0], cos[:, 1], cos[:, 1]], axis=-1)
    s64 = jnp.concatenate([-sin[:, 0], sin[:, 0], -sin[:, 1], sin[:, 1]], axis=-1)
    c = jnp.tile(c64, (nb, LANE // ATTN_HD))
    s = jnp.tile(s64, (nb, LANE // ATTN_HD))
    c = jnp.concatenate([c, jnp.ones((n_ctx_rows, LANE), F32)], axis=0)
    s = jnp.concatenate([s, jnp.zeros((n_ctx_rows, LANE), F32)], axis=0)
    return c, s


def kernel(x, c, ctx, c_ctx, w_mod, b_mod, norm1_g, norm2_g, w_in, w_out, lam_q1, lam_k1, lam_q2, lam_k2, subln_g, ssm_lambda_re, ssm_lambda_im, ssm_log_step, ssm_b_re, ssm_b_im, ssm_c_re, ssm_c_im, ssm_d, ssm_w_glu, pool_w, pool_scale, peer_wq, peer_keys, peer_u, peer_v, final_g):
    nb, l_lat, d = x.shape
    l_ctx = ctx.shape[1]
    depth = w_mod.shape[0]
    n_lat = nb * l_lat
    n_all = n_lat + nb * l_ctx
    tiles_per_batch = l_lat // ROW_TILE
    assert l_lat % ROW_TILE == 0 and l_ctx == ROW_TILE and nb + 1 <= 8

    cs = jnp.concatenate([c, c_ctx[None], jnp.zeros((8 - nb - 1, d), F32)], axis=0)
    mod_all = _mod_vectors(cs, w_mod, b_mod)
    cos_t, sin_t = _rope_tables(l_lat, nb, nb * l_ctx)
    xs = jnp.concatenate([x.reshape(n_lat, d), ctx.reshape(nb * l_ctx, d)], axis=0)

    for l in range(depth):
        last = l == depth - 1
        n_rows = n_lat if last else n_all
        mods = mod_all[l, :nb + 1].reshape(nb + 1, N_MOD, d)
        q, k, v, us, up = _inproj(xs, norm1_g[l][None], mods, w_in[l].astype(BF16), cos_t, sin_t,
                                  tiles_per_batch, nb)
        lam_init = 0.8 - 0.6 * math.exp(-0.3 * l)
        lam = (jnp.exp(jnp.sum(lam_q1[l] * lam_k1[l])) - jnp.exp(jnp.sum(lam_q2[l] * lam_k2[l]))
               + lam_init).reshape(1).astype(F32)
        attn = _attention(q, k, v, lam, subln_g[l][None], nb, l_lat, l_ctx, not last,
                          1.0 - lam_init)
        mats = _ssm_matrices(ssm_lambda_re[l], ssm_lambda_im[l], ssm_log_step[l], ssm_b_re[l],
                             ssm_b_im[l], ssm_c_re[l], ssm_c_im[l], ssm_d[l])
        y = _ssm(us, mats, nb, l_lat, l_ctx, not last)
        w_blk = jax.scipy.linalg.block_diag(*[pool_w[l, gi] for gi in range(len(POOL_WINDOWS))])
        pool = _pool(up, w_blk.astype(BF16), pool_scale[l][None], nb, l_lat, l_ctx, n_rows)
        xs = _mixout(xs, attn, y, pool, mods, ssm_w_glu[l].astype(BF16), w_out[l].astype(BF16),
                     tiles_per_batch, nb, n_rows)
        h2, s1, s2, e1, e2, tau = _peer_scores(xs, norm2_g[l][None], mods, peer_wq[l].astype(BF16),
                                               peer_keys[l].astype(BF16), tiles_per_batch, nb, n_rows)
        xs = _peer_dense(h2, xs, mods, peer_u[l].astype(BF16), peer_v[l].T.astype(BF16),
                         s1, s2, e1, e2, tau, final_g[None], tiles_per_batch, nb, n_rows, last)
    return xs.reshape(nb, l_lat, d)
```

```python
import functools
import math

import numpy as np
import jax
import jax.numpy as jnp
from jax import lax
from jax.experimental import pallas as pl
from jax.experimental.pallas import tpu as pltpu

F32 = jnp.float32
BF16 = jnp.bfloat16

EPS = 1e-6
GRID_W = 64
N_MOD = 6
ATTN_HD = 64
ATTN_VD = 128
ATTN_HEADS = 4
ATTN_W = 512
ROPE_BASE = 10000.0
SSM_W = 256
SSM_GROUP = 16
SSM_GROUPS = 16
SSM_STATE = 64
SSM_CHUNK = 16
POOL_W = 256
POOL_WINDOWS = (2, 4, 8, 16)
POOL_GROUP = 64
POOL_HALO = 8
SSM_OFF = 3 * ATTN_W
POOL_OFF = SSM_OFF + SSM_W
IN_W = POOL_OFF + POOL_W
PEER_HEADS = 8
PEER_NKEYS = 128
PEER_KDIM = 128
PEER_TOPK = 16
NEG_BIG = -3.0e38
SQRT_HALF = float(np.sqrt(0.5).astype(np.float32))

LANE = 128
ROW_TILE = 256
VMEM_LIMIT = 56 * 1024 * 1024


def _cparams(sem):
    return pltpu.CompilerParams(dimension_semantics=sem, vmem_limit_bytes=VMEM_LIMIT)


def _gelu(x):
    return 0.5 * x * (1.0 + lax.erf(x * SQRT_HALF))


def _norm_mod(x, g, shift, scale):
    ms = jnp.mean(x * x, axis=-1, keepdims=True)
    y = x * lax.rsqrt(ms + EPS) * g
    return y * (1.0 + scale) + shift


def _mod_kernel(s_ref, w_ref, b_ref, o_ref):
    s = s_ref[...]
    s = s * jax.nn.sigmoid(s)
    o_ref[0] = jnp.dot(s.astype(BF16), w_ref[0].astype(BF16), preferred_element_type=F32) + b_ref[0]


def _mod_vectors(cs, w_mod, b_mod):
    depth, d, n = w_mod.shape
    tn = 1536
    return pl.pallas_call(
        _mod_kernel,
        grid=(depth, n // tn),
        in_specs=[pl.BlockSpec((8, d), lambda l, j: (0, 0)),
                  pl.BlockSpec((1, d, tn), lambda l, j: (l, 0, j)),
                  pl.BlockSpec((1, 1, tn), lambda l, j: (l, 0, j))],
        out_specs=pl.BlockSpec((1, 8, tn), lambda l, j: (l, 0, j)),
        out_shape=jax.ShapeDtypeStruct((depth, 8, n), F32),
        compiler_params=_cparams(("arbitrary", "arbitrary")),
        name="mod_vectors",
    )(cs, w_mod, b_mod.reshape(depth, 1, n))


def _inproj_kernel(x_ref, g_ref, mod_ref, w_ref, cos_ref, sin_ref,
                   q_ref, k_ref, v_ref, us_ref, up_ref):
    tm = x_ref.shape[0]
    h = _norm_mod(x_ref[...], g_ref[...], mod_ref[0, 0:1, :], mod_ref[0, 1:2, :])
    p = jnp.dot(h.astype(BF16), w_ref[...], preferred_element_type=F32)
    c = cos_ref[...]
    s = sin_ref[...]
    lane = lax.broadcasted_iota(jnp.int32, (tm, LANE), 1)
    first = (lane % 32) < 16
    for off, ref, sc in ((0, q_ref, ATTN_HD ** -0.5), (ATTN_W, k_ref, 1.0)):
        for blk in range(ATTN_W // LANE):
            xb = p[:, off + LANE * blk: off + LANE * (blk + 1)]
            partner = jnp.where(first, pltpu.roll(xb, LANE - 16, 1), pltpu.roll(xb, 16, 1))
            ref[:, LANE * blk: LANE * (blk + 1)] = ((xb * c + partner * s) * sc).astype(BF16)
    v_ref[0] = p[:, 2 * ATTN_W:SSM_OFF].T.astype(BF16)
    us_ref[...] = p[:, SSM_OFF:POOL_OFF]
    up_ref[...] = p[:, POOL_OFF:IN_W]


def _seg_map(tiles_per_batch, nb):
    def seg(i):
        return jnp.minimum(i // tiles_per_batch, nb)
    return seg


def _inproj(x, g, mods, w, cos_t, sin_t, tiles_per_batch, nb):
    t, d = x.shape
    tm = ROW_TILE
    seg = _seg_map(tiles_per_batch, nb)
    row = lambda i: (i, 0)
    return pl.pallas_call(
        _inproj_kernel,
        grid=(t // tm,),
        in_specs=[pl.BlockSpec((tm, d), row),
                  pl.BlockSpec((1, d), lambda i: (0, 0)),
                  pl.BlockSpec((1, N_MOD, d), lambda i: (seg(i), 0, 0)),
                  pl.BlockSpec((d, IN_W), lambda i: (0, 0)),
                  pl.BlockSpec((tm, LANE), row),
                  pl.BlockSpec((tm, LANE), row)],
        out_specs=[pl.BlockSpec((tm, ATTN_W), row),
                   pl.BlockSpec((tm, ATTN_W), row),
                   pl.BlockSpec((1, ATTN_W, tm), lambda i: (i, 0, 0)),
                   pl.BlockSpec((tm, SSM_W), row),
                   pl.BlockSpec((tm, POOL_W), row)],
        out_shape=[jax.ShapeDtypeStruct((t, ATTN_W), BF16),
                   jax.ShapeDtypeStruct((t, ATTN_W), BF16),
                   jax.ShapeDtypeStruct((t // tm, ATTN_W, tm), BF16),
                   jax.ShapeDtypeStruct((t, SSM_W), F32),
                   jax.ShapeDtypeStruct((t, POOL_W), F32)],
        compiler_params=_cparams(("arbitrary",)),
        name="inproj",
    )(x, g, mods, w, cos_t, sin_t)


def _attn_stages(q, tq):
    lane = lax.broadcasted_iota(jnp.int32, (tq, LANE), 1)
    zero = jnp.zeros_like(q)
    qs = jnp.concatenate([jnp.where(lane < ATTN_HD, q, zero),
                          jnp.where(lane >= ATTN_HD, q, zero)], axis=0)

    def scores(kb):
        return lax.dot_general(kb, qs, (((1,), (1,)), ((), ())), preferred_element_type=F32)

    def softmax(m, l, s):
        m_new = jnp.maximum(m, jnp.max(s, axis=0, keepdims=True))
        alpha = jnp.exp(m - m_new)
        p = jnp.exp(s - m_new)
        l = alpha * l + jnp.sum(p, axis=0, keepdims=True)
        return m_new, l, alpha, p.astype(BF16)

    def weighted(acc, alpha, pb, vts):
        acc = alpha * acc
        rows = pb.shape[0] // len(vts)
        for c, vt in enumerate(vts):
            acc = acc + jnp.dot(vt, pb[c * rows:(c + 1) * rows], preferred_element_type=F32)
        return acc

    return scores, softmax, weighted


def _attn_finish(lam, m, l, acc, g, tq, out_scale):
    o = acc / l
    o = (o[:, :tq] - lam * o[:, tq:]).T
    ms = jnp.mean(o * o, axis=-1, keepdims=True)
    return (o * lax.rsqrt(ms + EPS) * g * out_scale).astype(BF16)


def _attn_ctx_kernel(lam_ref, q_ref, kc_ref, vc_ref, g_ref, o_ref, *, out_scale):
    tq = q_ref.shape[0]
    scores, softmax, weighted = _attn_stages(q_ref[...], tq)
    m = jnp.full((1, 2 * tq), NEG_BIG, F32)
    l = jnp.zeros((1, 2 * tq), F32)
    acc = jnp.zeros((ATTN_VD, 2 * tq), F32)
    m, l, alpha, pb = softmax(m, l, scores(kc_ref[...]))
    acc = weighted(acc, alpha, pb, [vc_ref[0]])
    o_ref[...] = _attn_finish(lam_ref[0], m, l, acc, g_ref[...], tq, out_scale)


def _attn_kernel(lam_ref, q_ref, kl_ref, kc_ref, vl_ref, vc_ref, g_ref, o_ref,
                 s_a, s_b, p_a, p_b, acc_ref, *, n_lat_k, tk, out_scale):
    tq = q_ref.shape[0]
    scores, softmax, weighted = _attn_stages(q_ref[...], tq)
    n_sub = tk // vc_ref.shape[-1]

    def stage_scores(t, s_buf):
        start = pl.multiple_of(t * tk, tk)
        s_buf[...] = scores(kl_ref[pl.ds(start, tk), :])

    def stage_softmax(m, l, s_buf, p_buf):
        m, l, alpha, pb = softmax(m, l, s_buf[...])
        p_buf[...] = pb
        return m, l, alpha

    def stage_values(alpha, p_buf, t):
        acc_ref[...] = weighted(acc_ref[...], alpha, p_buf[...],
                                [vl_ref[t * n_sub + c] for c in range(n_sub)])

    m = jnp.full((1, 2 * tq), NEG_BIG, F32)
    l = jnp.zeros((1, 2 * tq), F32)
    m, l, alpha, pb = softmax(m, l, scores(kc_ref[...]))
    acc_ref[...] = weighted(jnp.zeros((ATTN_VD, 2 * tq), F32), alpha, pb, [vc_ref[0]])

    if n_lat_k % 2 == 0:
        stage_scores(0, s_a)
        stage_scores(1, s_b)
        m, l, alpha = stage_softmax(m, l, s_a, p_a)

        def pair(i, carry):
            m, l, alpha = carry
            k = 2 * i
            stage_scores(k + 2, s_a)
            m, l, alpha_n = stage_softmax(m, l, s_b, p_b)
            stage_values(alpha, p_a, k)
            stage_scores(k + 3, s_b)
            m, l, alpha_nn = stage_softmax(m, l, s_a, p_a)
            stage_values(alpha_n, p_b, k + 1)
            return m, l, alpha_nn

        m, l, alpha = lax.fori_loop(0, n_lat_k // 2 - 1, pair, (m, l, alpha))
        m, l, alpha_n = stage_softmax(m, l, s_b, p_b)
        stage_values(alpha, p_a, n_lat_k - 2)
        stage_values(alpha_n, p_b, n_lat_k - 1)
    else:
        for t in range(n_lat_k):
            stage_scores(t, s_a)
            m, l, alpha = stage_softmax(m, l, s_a, p_a)
            stage_values(alpha, p_a, t)
    o_ref[...] = _attn_finish(lam_ref[0], m, l, acc_ref[...], g_ref[...], tq, out_scale)


def _attention(q, k, vt, lam, g, nb, l_lat, l_ctx, with_ctx_queries, out_scale):
    tq = ROW_TILE
    tk = 512 if l_lat % 512 == 0 else l_lat
    vt_tile = vt.shape[-1]
    assert l_ctx == tq and l_lat % tq == 0 and vt_tile == l_ctx and tk % vt_tile == 0
    n_lat_q = l_lat // tq
    n_vt_lat = l_lat // vt_tile
    ctx_blk0 = nb * l_lat // l_ctx
    smem = pl.BlockSpec(memory_space=pltpu.SMEM)
    gain = pl.BlockSpec((1, LANE), lambda b, h, i: (0, 0))
    kc_spec = pl.BlockSpec((l_ctx, LANE), lambda b, h, i: (ctx_blk0 + b, h))
    vc_spec = pl.BlockSpec((1, LANE, vt_tile), lambda b, h, i: (nb * n_vt_lat + b, h, 0))
    sem = _cparams(("arbitrary", "arbitrary", "arbitrary"))
    lat = pl.pallas_call(
        functools.partial(_attn_kernel, n_lat_k=l_lat // tk, tk=tk, out_scale=out_scale),
        grid=(nb, ATTN_HEADS, n_lat_q),
        in_specs=[smem,
                  pl.BlockSpec((tq, LANE), lambda b, h, i: (b * n_lat_q + i, h)),
                  pl.BlockSpec((l_lat, LANE), lambda b, h, i: (b, h)),
                  kc_spec,
                  pl.BlockSpec((n_vt_lat, LANE, vt_tile), lambda b, h, i: (b, h, 0)),
                  vc_spec, gain],
        out_specs=pl.BlockSpec((tq, LANE), lambda b, h, i: (b * n_lat_q + i, h)),
        out_shape=jax.ShapeDtypeStruct((nb * l_lat, ATTN_W), BF16),
        scratch_shapes=[pltpu.VMEM((tk, 2 * tq), F32), pltpu.VMEM((tk, 2 * tq), F32),
                        pltpu.VMEM((tk, 2 * tq), BF16), pltpu.VMEM((tk, 2 * tq), BF16),
                        pltpu.VMEM((ATTN_VD, 2 * tq), F32)],
        compiler_params=sem,
        name="diff_attention",
    )(lam, q, k, k, vt, vt, g)
    if not with_ctx_queries:
        return lat
    ctx = pl.pallas_call(
        functools.partial(_attn_ctx_kernel, out_scale=out_scale),
        grid=(nb, ATTN_HEADS, 1),
        in_specs=[smem,
                  pl.BlockSpec((tq, LANE), lambda b, h, i: (ctx_blk0 + b, h)),
                  kc_spec, vc_spec, gain],
        out_specs=pl.BlockSpec((tq, LANE), lambda b, h, i: (b, h)),
        out_shape=jax.ShapeDtypeStruct((nb * l_ctx, ATTN_W), BF16),
        compiler_params=sem,
        name="diff_attention_ctx",
    )(lam, q, k, vt, g)
    return jnp.concatenate([lat, ctx], axis=0)


def _ssm_kernel(ul_ref, uc_ref, m_ref, bm_ref, cm_ref, a_ref, d_ref, yl_ref, yc_ref,
                s_scr, h_scr, *, ctx_out):
    nl = ul_ref.shape[2]
    nc = uc_ref.shape[2]
    half = s_scr.shape[-1] // 2
    ul = ul_ref[0, 0]
    uc = uc_ref[0, 0]
    ulb = ul.astype(BF16)
    ucb = uc.astype(BF16)
    d = d_ref[0]
    yl = jnp.dot(ulb, m_ref[0], preferred_element_type=F32) + ul * d
    if ctx_out:
        yc = jnp.dot(ucb, m_ref[0], preferred_element_type=F32) + uc * d
    for dr in range(2):
        s_scr[dr, 0:nc, :] = jnp.dot(ucb, bm_ref[dr, 0], preferred_element_type=F32)
        s_scr[dr, nc:nc + nl, :] = jnp.dot(ulb, bm_ref[dr, 0], preferred_element_type=F32)
    a = a_ref[0]
    afr, afi, arr, ari = a[0:1], a[1:2], a[2:3], a[3:4]

    def make_body(base, n):
        def body(t, carry):
            fr, fi, rr, ri = carry
            rf = base + t
            rv = base + n - 1 - t
            h_scr[0, pl.ds(rf, 1), :] = jnp.concatenate([fr, fi], axis=-1)
            h_scr[1, pl.ds(rv, 1), :] = jnp.concatenate([rr, ri], axis=-1)
            sf = s_scr[0, pl.ds(rf, 1), :]
            sv = s_scr[1, pl.ds(rv, 1), :]
            nfr = afr * fr - afi * fi + sf[:, :half]
            nfi = afr * fi + afi * fr + sf[:, half:]
            nrr = arr * rr - ari * ri + sv[:, :half]
            nri = arr * ri + ari * rr + sv[:, half:]
            return nfr, nfi, nrr, nri
        return body

    z = jnp.zeros((1, half), F32)
    carry = lax.fori_loop(0, nc, make_body(0, nc), (z, z, z, z))
    lax.fori_loop(0, nl, make_body(nc, nl), carry)
    for dr in range(2):
        yl = yl + jnp.dot(h_scr[dr, nc:nc + nl, :].astype(BF16), cm_ref[dr, 0],
                          preferred_element_type=F32)
        if ctx_out:
            yc = yc + jnp.dot(h_scr[dr, 0:nc, :].astype(BF16), cm_ref[dr, 0],
                              preferred_element_type=F32)
    yl_ref[0, 0] = yl
    if ctx_out:
        yc_ref[0, 0] = yc
    else:
        yc_ref[0, 0] = jnp.zeros_like(uc)


def _ssm_matrices(lam_re, lam_im, log_step, b_re, b_im, c_re, c_im, d):
    tc = SSM_CHUNK
    g, p, hh = SSM_GROUPS, SSM_STATE, SSM_GROUP
    npair = g // 2
    step = jnp.exp(log_step)[..., None]
    den = lam_re * lam_re + lam_im * lam_im

    def power(k):
        er = jnp.exp(lam_re * step * k)
        return er * jnp.cos(lam_im * step * k), er * jnp.sin(lam_im * step * k)

    ar, ai = power(1.0)
    nr = ar - 1.0
    cr_ = (nr * lam_re + ai * lam_im) / den
    ci_ = (ai * lam_re - nr * lam_im) / den
    bbr = cr_[..., None] * b_re - ci_[..., None] * b_im
    bbi = cr_[..., None] * b_im + ci_[..., None] * b_re
    ks = jnp.arange(tc + 1, dtype=F32)
    pw = jax.vmap(power)(ks)
    pwr, pwi = pw
    cpr = c_re[None] * pwr[:, :, :, None, :] - c_im[None] * pwi[:, :, :, None, :]
    cpi = c_re[None] * pwi[:, :, :, None, :] + c_im[None] * pwr[:, :, :, None, :]
    hp = lax.Precision.HIGHEST
    kk = (jnp.einsum('kdgop,dgpi->kdgoi', cpr[:tc], bbr, precision=hp)
          - jnp.einsum('kdgop,dgpi->kdgoi', cpi[:tc], bbi, precision=hp))
    s_idx = jnp.arange(tc)[:, None]
    t_idx = jnp.arange(tc)[None, :]
    lag_f = t_idx - s_idx
    lag_r = s_idx - t_idx
    mf = jnp.where((lag_f >= 0)[:, :, None, None, None], kk[jnp.clip(lag_f, 0, tc - 1), 0], 0.0)
    mr = jnp.where((lag_r >= 0)[:, :, None, None, None], kk[jnp.clip(lag_r, 0, tc - 1), 1], 0.0)
    mm = mf + mr
    mm = jnp.transpose(mm, (2, 0, 4, 1, 3)).reshape(g, tc * hh, tc * hh)
    zero = jnp.zeros_like(mm[0::2])
    m2 = jnp.concatenate([jnp.concatenate([mm[0::2], zero], -1),
                          jnp.concatenate([zero, mm[1::2]], -1)], -2)
    def bmat(dr, exps):
        pr = pwr[exps, dr]
        pi = pwi[exps, dr]
        re = pr[..., None] * bbr[dr][None] - pi[..., None] * bbi[dr][None]
        im = pr[..., None] * bbi[dr][None] + pi[..., None] * bbr[dr][None]
        re = jnp.transpose(re, (1, 0, 3, 2)).reshape(g, tc * hh, p)
        im = jnp.transpose(im, (1, 0, 3, 2)).reshape(g, tc * hh, p)
        z = jnp.zeros_like(re[0::2])
        top = jnp.concatenate([re[0::2], z, im[0::2], z], -1)
        bot = jnp.concatenate([z, re[1::2], z, im[1::2]], -1)
        return jnp.concatenate([top, bot], -2)
    bm = jnp.stack([bmat(0, jnp.arange(tc - 1, -1, -1)), bmat(1, jnp.arange(tc))])
    def cmat(dr, exps):
        re = jnp.transpose(cpr[exps, dr], (1, 3, 0, 2)).reshape(g, p, tc * hh)
        im = jnp.transpose(cpi[exps, dr], (1, 3, 0, 2)).reshape(g, p, tc * hh)
        z = jnp.zeros_like(re[0::2])
        return jnp.concatenate([jnp.concatenate([re[0::2], z], -1),
                                jnp.concatenate([z, re[1::2]], -1),
                                jnp.concatenate([-im[0::2], z], -1),
                                jnp.concatenate([z, -im[1::2]], -1)], -2)
    cm = jnp.stack([cmat(0, jnp.arange(1, tc + 1)), cmat(1, jnp.arange(tc, 0, -1))])
    a16r = pwr[tc].reshape(2, npair, 2 * p)
    a16i = pwi[tc].reshape(2, npair, 2 * p)
    a16 = jnp.stack([a16r[0], a16i[0], a16r[1], a16i[1]], axis=1)
    dd = jnp.broadcast_to(d.reshape(npair, 2, 1, hh), (npair, 2, tc, hh)).reshape(npair, 1, 2 * tc * hh)
    return m2.astype(BF16), bm.astype(BF16), cm.astype(BF16), a16, dd


def _ssm(us, mats, nb, l_lat, l_ctx, ctx_out):
    m2, bm, cm, a16, dd = mats
    tc, hh = SSM_CHUNK, SSM_GROUP
    npair = SSM_GROUPS // 2
    w = 2 * tc * hh

    def fold(u, n):
        u = u.reshape(nb, n // tc, tc, npair, 2, hh)
        return jnp.transpose(u, (0, 3, 1, 4, 2, 5)).reshape(nb, npair, n // tc, w)

    def unfold(y, n):
        y = y.reshape(nb, npair, n // tc, 2, tc, hh)
        return jnp.transpose(y, (0, 2, 4, 1, 3, 5)).reshape(nb * n, SSM_W)

    ul = fold(us[:nb * l_lat], l_lat)
    uc = fold(us[nb * l_lat:], l_ctx)
    nl, nc = l_lat // tc, l_ctx // tc
    yl, yc = pl.pallas_call(
        functools.partial(_ssm_kernel, ctx_out=ctx_out),
        grid=(nb, npair),
        in_specs=[pl.BlockSpec((1, 1, nl, w), lambda b, q: (b, q, 0, 0)),
                  pl.BlockSpec((1, 1, nc, w), lambda b, q: (b, q, 0, 0)),
                  pl.BlockSpec((1, w, w), lambda b, q: (q, 0, 0)),
                  pl.BlockSpec((2, 1, w, 4 * SSM_STATE), lambda b, q: (0, q, 0, 0)),
                  pl.BlockSpec((2, 1, 4 * SSM_STATE, w), lambda b, q: (0, q, 0, 0)),
                  pl.BlockSpec((1, 4, 2 * SSM_STATE), lambda b, q: (q, 0, 0)),
                  pl.BlockSpec((1, 1, w), lambda b, q: (q, 0, 0))],
        out_specs=[pl.BlockSpec((1, 1, nl, w), lambda b, q: (b, q, 0, 0)),
                   pl.BlockSpec((1, 1, nc, w), lambda b, q: (b, q, 0, 0))],
        out_shape=[jax.ShapeDtypeStruct((nb, npair, nl, w), F32),
                   jax.ShapeDtypeStruct((nb, npair, nc, w), F32)],
        scratch_shapes=[pltpu.VMEM((2, nc + nl, 4 * SSM_STATE), F32),
                        pltpu.VMEM((2, nc + nl, 4 * SSM_STATE), F32)],
        compiler_params=_cparams(("arbitrary", "arbitrary")),
        name="s5_scan",
    )(ul, uc, m2, bm, cm, a16, dd)
    y = unfold(yl, l_lat)
    if ctx_out:
        y = jnp.concatenate([y, unfold(yc, l_ctx)], axis=0)
    return y


def _pool_kernel(prev_ref, cur_ref, next_ref, w_ref, scale_ref, o_ref, ext,
                 *, n_lat_tiles, tiles_per_lat, l_lat, l_ctx):
    r = cur_ref.shape[0]
    hl = POOL_HALO
    i = pl.program_id(0)
    is_lat = i < n_lat_tiles
    pos = jnp.where(is_lat, i % tiles_per_lat, 0)
    n_tiles = jnp.where(is_lat, tiles_per_lat, l_ctx // r)
    seq_len = jnp.where(is_lat, l_lat, l_ctx)
    zeros = jnp.zeros((hl, POOL_W), F32)
    ext[0:hl, :] = jnp.where(pos > 0, prev_ref[...], zeros)
    ext[hl:hl + r, :] = cur_ref[...]
    ext[hl + r:hl + r + hl, :] = jnp.where(pos < n_tiles - 1, next_ref[...], zeros)
    u = cur_ref[...]
    t = pos * r + lax.broadcasted_iota(jnp.int32, (r, 1), 0)
    lane = lax.broadcasted_iota(jnp.int32, (r, POOL_W), 1)

    def win(k):
        return ext[hl + k:hl + k + r, :]

    acc = win(-1) + u
    mean = jnp.zeros((r, POOL_W), F32)
    lo_k, hi_k = -1, 0
    for gi, wn in enumerate(POOL_WINDOWS):
        hw = wn // 2
        while lo_k > -hw:
            lo_k -= 1
            acc = acc + win(lo_k)
        while hi_k < hw - 1:
            hi_k += 1
            acc = acc + win(hi_k)
        cnt = (jnp.minimum(t + hw, seq_len) - jnp.maximum(t - hw, 0)).astype(F32)
        sel = (lane >= gi * POOL_GROUP) & (lane < (gi + 1) * POOL_GROUP)
        mean = jnp.where(sel, acc / cnt, mean)
    dlt = (mean - u).astype(BF16)
    o_ref[...] = jnp.dot(dlt, w_ref[...], preferred_element_type=F32) * scale_ref[...]


def _pool(up, w_blk, scale, nb, l_lat, l_ctx, n_rows):
    r = ROW_TILE
    hl = POOL_HALO
    n_tiles = n_rows // r
    last8 = up.shape[0] // hl - 1
    kern = functools.partial(_pool_kernel, n_lat_tiles=nb * l_lat // r, tiles_per_lat=l_lat // r,
                             l_lat=l_lat, l_ctx=l_ctx)
    return pl.pallas_call(
        kern,
        grid=(n_tiles,),
        in_specs=[pl.BlockSpec((hl, POOL_W), lambda i: (jnp.maximum(i * (r // hl) - 1, 0), 0)),
                  pl.BlockSpec((r, POOL_W), lambda i: (i, 0)),
                  pl.BlockSpec((hl, POOL_W), lambda i: (jnp.minimum((i + 1) * (r // hl), last8), 0)),
                  pl.BlockSpec((POOL_W, POOL_W), lambda i: (0, 0)),
                  pl.BlockSpec((1, POOL_W), lambda i: (0, 0))],
        out_specs=pl.BlockSpec((r, POOL_W), lambda i: (i, 0)),
        out_shape=jax.ShapeDtypeStruct((n_rows, POOL_W), F32),
        scratch_shapes=[pltpu.VMEM((r + 2 * hl, POOL_W), F32)],
        compiler_params=_cparams(("arbitrary",)),
        name="pool_mix",
    )(up, up, up, w_blk, scale)


def _mixout_kernel(x_ref, attn_ref, y_ref, pool_ref, mod_ref, wglu_ref, wout_ref, o_ref):
    yg = _gelu(y_ref[...])
    z = yg * jax.nn.sigmoid(jnp.dot(yg.astype(BF16), wglu_ref[...], preferred_element_type=F32))
    r = jnp.dot(attn_ref[...], wout_ref[0:ATTN_W, :], preferred_element_type=F32)
    r = r + jnp.dot(z.astype(BF16), wout_ref[ATTN_W:ATTN_W + SSM_W, :], preferred_element_type=F32)
    r = r + jnp.dot(pool_ref[...].astype(BF16), wout_ref[ATTN_W + SSM_W:, :],
                    preferred_element_type=F32)
    o_ref[...] = x_ref[...] + mod_ref[0, 2:3, :] * r


def _mixout(x, attn, y, pool, mods, wglu, wout, tiles_per_batch, nb, n_rows):
    d = x.shape[1]
    tm = ROW_TILE
    seg = _seg_map(tiles_per_batch, nb)
    row = lambda i: (i, 0)
    return pl.pallas_call(
        _mixout_kernel,
        grid=(n_rows // tm,),
        in_specs=[pl.BlockSpec((tm, d), row),
                  pl.BlockSpec((tm, ATTN_W), row),
                  pl.BlockSpec((tm, SSM_W), row),
                  pl.BlockSpec((tm, POOL_W), row),
                  pl.BlockSpec((1, N_MOD, d), lambda i: (seg(i), 0, 0)),
                  pl.BlockSpec((SSM_W, SSM_W), lambda i: (0, 0)),
                  pl.BlockSpec((d, d), lambda i: (0, 0))],
        out_specs=pl.BlockSpec((tm, d), row),
        out_shape=jax.ShapeDtypeStruct((n_rows, d), F32),
        compiler_params=_cparams(("arbitrary",)),
        name="mix_out",
    )(x, attn, y, pool, mods, wglu, wout)


def _top_sorted(s, k):
    rows = []
    for _ in range(k):
        m = jnp.max(s, axis=0, keepdims=True)
        rows.append(m)
        s = jnp.where(s == m, NEG_BIG, s)
    return rows


def _peer_score_kernel(x_ref, g_ref, mod_ref, wq_ref, keys_ref,
                       h_ref, s1_ref, s2_ref, e1_ref, e2_ref, tau_ref, q_scr):
    tm = x_ref.shape[0]
    h = _norm_mod(x_ref[...], g_ref[...], mod_ref[0, 3:4, :], mod_ref[0, 4:5, :]).astype(BF16)
    h_ref[...] = h
    q = jnp.dot(h, wq_ref[...], preferred_element_type=F32)
    for hc in range(2 * PEER_HEADS):
        q_scr[hc] = q[:, hc * PEER_KDIM:(hc + 1) * PEER_KDIM].astype(BF16)
    row8 = lax.broadcasted_iota(jnp.int32, (8, tm), 0)

    def head(hd, _):
        nt = (((1,), (1,)), ((), ()))
        s1 = lax.dot_general(keys_ref[hd, 0], q_scr[2 * hd], nt,
                             preferred_element_type=F32)
        s2 = lax.dot_general(keys_ref[hd, 1], q_scr[2 * hd + 1], nt,
                             preferred_element_type=F32)
        a = _top_sorted(s1, PEER_TOPK)
        b = _top_sorted(s2, PEER_TOPK)
        acat = jnp.concatenate(a, axis=0)
        bcat = jnp.concatenate(b, axis=0)
        pieces = [a[0] + bcat]
        for i in range(2, 9):
            piece = a[i - 1] + bcat[0:8]
            n_valid = PEER_TOPK // i
            pieces.append(piece if n_valid >= 8 else jnp.where(row8 < n_valid, piece, NEG_BIG))
        pieces.append(acat[8:16] + b[0])
        cand = jnp.concatenate(pieces, axis=0)
        work = cand
        cum = jnp.zeros((1, tm), F32)
        tau = jnp.full((1, tm), NEG_BIG, F32)
        for _ in range(PEER_TOPK):
            m = jnp.max(work, axis=0, keepdims=True)
            eq = work == m
            new = cum + jnp.sum(eq.astype(F32), axis=0, keepdims=True)
            tau = jnp.where((cum < PEER_TOPK) & (new >= PEER_TOPK), m, tau)
            work = jnp.where(eq, NEG_BIG, work)
            cum = new
        top = a[0] + b[0]
        z = jnp.sum(jnp.where(cand >= tau, jnp.exp(cand - top), 0.0), axis=0, keepdims=True)
        s1_ref[hd] = s1
        s2_ref[hd] = s2
        e1_ref[hd] = jnp.exp(s1 - a[0])
        e2_ref[hd] = jnp.exp(s2 - b[0]) * (1.0 / z)
        tau_ref[pl.ds(hd, 1), :] = tau
        return 0

    lax.fori_loop(0, PEER_HEADS, head, 0)


def _peer_scores(x, g, mods, wq, keys, tiles_per_batch, nb, n_rows):
    d = x.shape[1]
    tm = ROW_TILE
    seg = _seg_map(tiles_per_batch, nb)
    nq = wq.shape[1]
    col = lambda i: (0, 0, i)
    big = jax.ShapeDtypeStruct((PEER_HEADS, PEER_NKEYS, n_rows), F32)
    return pl.pallas_call(
        _peer_score_kernel,
        grid=(n_rows // tm,),
        in_specs=[pl.BlockSpec((tm, d), lambda i: (i, 0)),
                  pl.BlockSpec((1, d), lambda i: (0, 0)),
                  pl.BlockSpec((1, N_MOD, d), lambda i: (seg(i), 0, 0)),
                  pl.BlockSpec((d, nq), lambda i: (0, 0)),
                  pl.BlockSpec((PEER_HEADS, 2, PEER_NKEYS, PEER_KDIM), lambda i: (0, 0, 0, 0))],
        out_specs=[pl.BlockSpec((tm, d), lambda i: (i, 0)),
                   pl.BlockSpec((PEER_HEADS, PEER_NKEYS, tm), col),
                   pl.BlockSpec((PEER_HEADS, PEER_NKEYS, tm), col),
                   pl.BlockSpec((PEER_HEADS, PEER_NKEYS, tm), col),
                   pl.BlockSpec((PEER_HEADS, PEER_NKEYS, tm), col),
                   pl.BlockSpec((PEER_HEADS, tm), lambda i: (0, i))],
        out_shape=[jax.ShapeDtypeStruct((n_rows, d), BF16), big, big, big, big,
                   jax.ShapeDtypeStruct((PEER_HEADS, n_rows), F32)],
        scratch_shapes=[pltpu.VMEM((2 * PEER_HEADS, tm, PEER_KDIM), BF16)],
        compiler_params=_cparams(("arbitrary",)),
        name="peer_scores",
    )(x, g, mods, wq, keys)


PEER_CHUNK_KEYS = 8
PEER_TOK_TILE = 512
PEER_SUB = 256


def _peer_dense_kernel(h_ref, x_ref, mod_ref, u_ref, vt_ref, s1_ref, s2_ref, e1_ref, e2_ref,
                       tau_ref, fg_ref, o_ref, acc_ref, a0, a1, w0, w1, bs_scr, be_scr,
                       *, n_chunks, final_norm):
    n_i = PEER_CHUNK_KEYS
    tm = h_ref.shape[0]
    g = pl.program_id(0)
    nt_dims = (((1,), (1,)), ((), ()))
    n_k = u_ref.shape[1] // PEER_SUB
    n_lb = tm // LANE
    assert n_k == n_lb == vt_ref.shape[1] // PEER_SUB and n_i * PEER_NKEYS == u_ref.shape[0]
    n_mt = u_ref.shape[0] // PEER_SUB

    @pl.when(g == 0)
    def _():
        acc_ref[...] = jnp.zeros_like(acc_ref)
        a1[...] = jnp.zeros_like(a1)
        w0[...] = jnp.zeros_like(w0)

    for hd in range(PEER_HEADS):
        s1b = s1_ref[hd]
        e1b = e1_ref[hd]
        for ii in range(n_i):
            bs_scr[hd, ii] = jnp.broadcast_to(s1b[ii:ii + 1], (8, tm))
            be_scr[hd, ii] = jnp.broadcast_to(e1b[ii:ii + 1], (8, tm))

    def run(a_wr, a_rd, w_wr, w_rd):
        for lb in range(n_lb):
            th, lo = divmod(lb * LANE, PEER_SUB)
            ls = slice(lb * LANE, (lb + 1) * LANE)
            lh = slice(lo, lo + LANE)
            ks = slice(lb * PEER_SUB, (lb + 1) * PEER_SUB)
            taus = [jnp.broadcast_to(tau_ref[hd:hd + 1, ls], (8, LANE)) for hd in range(PEER_HEADS)]

            def slab(jb, _):
                nt = jb // n_mt
                m0 = pl.multiple_of((jb % n_mt) * PEER_SUB, PEER_SUB)
                t0 = pl.multiple_of(nt * PEER_SUB, PEER_SUB)
                part_a = lax.dot_general(u_ref[pl.ds(m0, PEER_SUB), ks], h_ref[pl.ds(t0, PEER_SUB), ks],
                                         nt_dims, preferred_element_type=F32)
                part_v = jnp.dot(vt_ref[pl.ds(m0, PEER_SUB), ks], w_rd[nt, ks, :],
                                 preferred_element_type=F32)
                j0 = pl.multiple_of(jb * 16, 16)
                gs = [[jnp.zeros((8, LANE), F32), jnp.zeros((8, LANE), F32)] for _ in range(n_i)]
                for hd in range(PEER_HEADS):
                    s2v = s2_ref[hd, pl.ds(j0, 16), ls]
                    e2v = e2_ref[hd, pl.ds(j0, 16), ls]
                    for ii in range(n_i):
                        bs = bs_scr[hd, ii, :, ls]
                        be = be_scr[hd, ii, :, ls]
                        for hf in range(2):
                            csum = bs + s2v[hf * 8:(hf + 1) * 8]
                            gs[ii][hf] = gs[ii][hf] + jnp.where(csum >= taus[hd],
                                                               e2v[hf * 8:(hf + 1) * 8] * be, 0.0)
                for ii in range(n_i):
                    r0 = pl.multiple_of(ii * PEER_NKEYS + j0, 16)
                    g16 = jnp.concatenate(gs[ii], axis=0)
                    w_wr[th, pl.ds(r0, 16), lh] = (_gelu(a_rd[th, pl.ds(r0, 16), lh]) * g16).astype(BF16)
                if lb == 0:
                    a_wr[nt, pl.ds(m0, PEER_SUB), :] = part_a
                else:
                    a_wr[nt, pl.ds(m0, PEER_SUB), :] += part_a
                acc_ref[nt, pl.ds(m0, PEER_SUB), :] += part_v
                return 0

            lax.fori_loop(0, PEER_NKEYS // 16, slab, 0, unroll=True)

    @pl.when(g % 2 == 0)
    def _():
        run(a0, a1, w1, w0)

    @pl.when(g % 2 == 1)
    def _():
        run(a1, a0, w0, w1)

    @pl.when((g >= 2) & ((g - 2) % n_chunks == n_chunks - 1))
    def _():
        out = x_ref[...] + mod_ref[0, 5:6, :] * jnp.concatenate(
            [acc_ref[t].T for t in range(acc_ref.shape[0])], axis=0)
        if final_norm:
            ms = jnp.mean(out * out, axis=-1, keepdims=True)
            out = out * lax.rsqrt(ms + EPS) * fg_ref[...]
        o_ref[...] = out
        acc_ref[...] = jnp.zeros_like(acc_ref)


def _peer_dense(h, x, mods, u, vt, s1, s2, e1, e2, tau, fg, tiles_per_batch, nb, n_rows, final_norm):
    d = x.shape[1]
    tm = PEER_TOK_TILE
    n_i = PEER_CHUNK_KEYS
    ne = n_i * PEER_NKEYS
    tiles_per_batch = tiles_per_batch * ROW_TILE // tm
    n_chunks = u.shape[0] // ne
    n_steps = (n_rows // tm) * n_chunks
    assert n_rows % tm == 0 and d == ne
    assert PEER_NKEYS // 16 == (tm // PEER_SUB) * (ne // PEER_SUB)

    def stage(lag):
        def split(g):
            n = jnp.clip(g - lag, 0, n_steps - 1)
            return n // n_chunks, n % n_chunks
        return split

    act, gate, val = stage(0), stage(1), stage(2)
    seg = lambda i: jnp.minimum(i // tiles_per_batch, nb)
    kern = functools.partial(_peer_dense_kernel, n_chunks=n_chunks, final_norm=final_norm)
    full = (PEER_HEADS, PEER_NKEYS, tm)
    rows = (PEER_HEADS, n_i, tm)
    return pl.pallas_call(
        kern,
        grid=(n_steps + 2,),
        in_specs=[pl.BlockSpec((tm, d), lambda g: (act(g)[0], 0)),
                  pl.BlockSpec((tm, d), lambda g: (val(g)[0], 0)),
                  pl.BlockSpec((1, N_MOD, d), lambda g: (seg(val(g)[0]), 0, 0)),
                  pl.BlockSpec((ne, d), lambda g: (act(g)[1], 0)),
                  pl.BlockSpec((d, ne), lambda g: (0, val(g)[1])),
                  pl.BlockSpec(rows, lambda g: (0, gate(g)[1], gate(g)[0])),
                  pl.BlockSpec(full, lambda g: (0, 0, gate(g)[0])),
                  pl.BlockSpec(rows, lambda g: (0, gate(g)[1], gate(g)[0])),
                  pl.BlockSpec(full, lambda g: (0, 0, gate(g)[0])),
                  pl.BlockSpec((PEER_HEADS, tm), lambda g: (0, gate(g)[0])),
                  pl.BlockSpec((1, d), lambda g: (0, 0))],
        out_specs=pl.BlockSpec((tm, d), lambda g: (val(g)[0], 0)),
        out_shape=jax.ShapeDtypeStruct((n_rows, d), F32),
        scratch_shapes=[pltpu.VMEM((tm // PEER_SUB, d, PEER_SUB), F32),
                        pltpu.VMEM((tm // PEER_SUB, ne, PEER_SUB), F32),
                        pltpu.VMEM((tm // PEER_SUB, ne, PEER_SUB), F32),
                        pltpu.VMEM((tm // PEER_SUB, ne, PEER_SUB), BF16),
                        pltpu.VMEM((tm // PEER_SUB, ne, PEER_SUB), BF16),
                        pltpu.VMEM((PEER_HEADS, n_i, 8, tm), F32),
                        pltpu.VMEM((PEER_HEADS, n_i, 8, tm), F32)],
        compiler_params=_cparams(("arbitrary",)),
        name="peer_experts",
    )(h, x, mods, u, vt, s1, s2, e1, e2, tau, fg)


def _rope_tables(l_lat, nb, n_ctx_rows):
    rows = l_lat // GRID_W
    r = jnp.repeat(jnp.arange(rows), GRID_W)
    col = jnp.tile(jnp.arange(GRID_W), rows)
    pos = jnp.stack([r, col], axis=-1).astype(F32)
    nf = ATTN_HD // 4
    inv = 1.0 / (ROPE_BASE ** (jnp.arange(nf, dtype=F32) / nf))
    ang = pos[:, :, None] * inv
    cos, sin = jnp.cos(ang), jnp.sin(ang)
    c64 = jnp.concatenate([cos[:, 0], cos[:, 0], cos[:, 1], cos[:, 1]], axis=-1)
    s64 = jnp.concatenate([-sin[:, 0], sin[:, 0], -sin[:, 1], sin[:, 1]], axis=-1)
    c = jnp.tile(c64, (nb, LANE // ATTN_HD))
    s = jnp.tile(s64, (nb, LANE // ATTN_HD))
    c = jnp.concatenate([c, jnp.ones((n_ctx_rows, LANE), F32)], axis=0)
    s = jnp.concatenate([s, jnp.zeros((n_ctx_rows, LANE), F32)], axis=0)
    return c, s


def kernel(x, c, ctx, c_ctx, w_mod, b_mod, norm1_g, norm2_g, w_in, w_out, lam_q1, lam_k1, lam_q2, lam_k2, subln_g, ssm_lambda_re, ssm_lambda_im, ssm_log_step, ssm_b_re, ssm_b_im, ssm_c_re, ssm_c_im, ssm_d, ssm_w_glu, pool_w, pool_scale, peer_wq, peer_keys, peer_u, peer_v, final_g):
    nb, l_lat, d = x.shape
    l_ctx = ctx.shape[1]
    depth = w_mod.shape[0]
    n_lat = nb * l_lat
    n_all = n_lat + nb * l_ctx
    tiles_per_batch = l_lat // ROW_TILE
    assert l_lat % ROW_TILE == 0 and l_ctx == ROW_TILE and nb + 1 <= 8

    cs = jnp.concatenate([c, c_ctx[None], jnp.zeros((8 - nb - 1, d), F32)], axis=0)
    mod_all = _mod_vectors(cs, w_mod, b_mod)
    cos_t, sin_t = _rope_tables(l_lat, nb, nb * l_ctx)
    xs = jnp.concatenate([x.reshape(n_lat, d), ctx.reshape(nb * l_ctx, d)], axis=0)

    for l in range(depth):
        last = l == depth - 1
        n_rows = n_lat if last else n_all
        mods = mod_all[l, :nb + 1].reshape(nb + 1, N_MOD, d)
        q, k, v, us, up = _inproj(xs, norm1_g[l][None], mods, w_in[l].astype(BF16), cos_t, sin_t,
                                  tiles_per_batch, nb)
        lam_init = 0.8 - 0.6 * math.exp(-0.3 * l)
        lam = (jnp.exp(jnp.sum(lam_q1[l] * lam_k1[l])) - jnp.exp(jnp.sum(lam_q2[l] * lam_k2[l]))
               + lam_init).reshape(1).astype(F32)
        attn = _attention(q, k, v, lam, subln_g[l][None], nb, l_lat, l_ctx, not last,
                          1.0 - lam_init)
        mats = _ssm_matrices(ssm_lambda_re[l], ssm_lambda_im[l], ssm_log_step[l], ssm_b_re[l],
                             ssm_b_im[l], ssm_c_re[l], ssm_c_im[l], ssm_d[l])
        y = _ssm(us, mats, nb, l_lat, l_ctx, not last)
        w_blk = jax.scipy.linalg.block_diag(*[pool_w[l, gi] for gi in range(len(POOL_WINDOWS))])
        pool = _pool(up, w_blk.astype(BF16), pool_scale[l][None], nb, l_lat, l_ctx, n_rows)
        xs = _mixout(xs, attn, y, pool, mods, ssm_w_glu[l].astype(BF16), w_out[l].astype(BF16),
                     tiles_per_batch, nb, n_rows)
        h2, s1, s2, e1, e2, tau = _peer_scores(xs, norm2_g[l][None], mods, peer_wq[l].astype(BF16),
                                               peer_keys[l].astype(BF16), tiles_per_batch, nb, n_rows)
        xs = _peer_dense(h2, xs, mods, peer_u[l].astype(BF16), peer_v[l].T.astype(BF16),
                         s1, s2, e1, e2, tau, final_g[None], tiles_per_batch, nb, n_rows, last)
    return xs.reshape(nb, l_lat, d)
```

```python
import functools
import math

import numpy as np
import jax
import jax.numpy as jnp
from jax import lax
from jax.experimental import pallas as pl
from jax.experimental.pallas import tpu as pltpu

F32 = jnp.float32
BF16 = jnp.bfloat16

EPS = 1e-6
GRID_W = 64
N_MOD = 6
ATTN_HD = 64
ATTN_VD = 128
ATTN_VROWS = ATTN_VD + 16
LOG2E = float(np.log2(np.e))
ATTN_HEADS = 4
ATTN_W = 512
ROPE_BASE = 10000.0
SSM_W = 256
SSM_GROUP = 16
SSM_GROUPS = 16
SSM_STATE = 64
SSM_CHUNK = 16
POOL_W = 256
POOL_WINDOWS = (2, 4, 8, 16)
POOL_GROUP = 64
POOL_HALO = 8
SSM_OFF = 3 * ATTN_W
POOL_OFF = SSM_OFF + SSM_W
IN_W = POOL_OFF + POOL_W
PEER_HEADS = 8
PEER_NKEYS = 128
PEER_KDIM = 128
PEER_TOPK = 16
NEG_BIG = -3.0e38
SQRT_HALF = float(np.sqrt(0.5).astype(np.float32))

LANE = 128
ROW_TILE = 256
VMEM_LIMIT = 56 * 1024 * 1024


def _cparams(sem):
    return pltpu.CompilerParams(dimension_semantics=sem, vmem_limit_bytes=VMEM_LIMIT)


def _gelu(x):
    return 0.5 * x * (1.0 + lax.erf(x * SQRT_HALF))


def _norm_mod(x, g, shift, scale):
    ms = jnp.mean(x * x, axis=-1, keepdims=True)
    y = x * lax.rsqrt(ms + EPS) * g
    return y * (1.0 + scale) + shift


def _mod_kernel(s_ref, w_ref, b_ref, o_ref):
    s = s_ref[...]
    s = s * jax.nn.sigmoid(s)
    o_ref[0] = jnp.dot(s.astype(BF16), w_ref[0].astype(BF16), preferred_element_type=F32) + b_ref[0]


def _mod_vectors(cs, w_mod, b_mod):
    depth, d, n = w_mod.shape
    tn = 1536
    return pl.pallas_call(
        _mod_kernel,
        grid=(depth, n // tn),
        in_specs=[pl.BlockSpec((8, d), lambda l, j: (0, 0)),
                  pl.BlockSpec((1, d, tn), lambda l, j: (l, 0, j)),
                  pl.BlockSpec((1, 1, tn), lambda l, j: (l, 0, j))],
        out_specs=pl.BlockSpec((1, 8, tn), lambda l, j: (l, 0, j)),
        out_shape=jax.ShapeDtypeStruct((depth, 8, n), F32),
        compiler_params=_cparams(("arbitrary", "arbitrary")),
        name="mod_vectors",
    )(cs, w_mod, b_mod.reshape(depth, 1, n))


def _inproj_kernel(x_ref, g_ref, mod_ref, w_ref, cos_ref, sin_ref,
                   q_ref, k_ref, v_ref, us_ref, up_ref):
    tm = x_ref.shape[0]
    h = _norm_mod(x_ref[...], g_ref[...], mod_ref[0, 0:1, :], mod_ref[0, 1:2, :])
    p = jnp.dot(h.astype(BF16), w_ref[...], preferred_element_type=F32)
    c = cos_ref[...]
    s = sin_ref[...]
    lane = lax.broadcasted_iota(jnp.int32, (tm, LANE), 1)
    first = (lane % 32) < 16
    for off, ref, sc in ((0, q_ref, ATTN_HD ** -0.5 * LOG2E), (ATTN_W, k_ref, 1.0)):
        for blk in range(ATTN_W // LANE):
            xb = p[:, off + LANE * blk: off + LANE * (blk + 1)]
            partner = jnp.where(first, pltpu.roll(xb, LANE - 16, 1), pltpu.roll(xb, 16, 1))
            ref[:, LANE * blk: LANE * (blk + 1)] = ((xb * c + partner * s) * sc).astype(BF16)
    vt = p[:, 2 * ATTN_W:SSM_OFF].T.astype(BF16)
    ones_rows = (lax.broadcasted_iota(jnp.int32, (ATTN_VROWS - ATTN_VD, tm), 0) == 0).astype(BF16)
    for hd in range(ATTN_HEADS):
        v_ref[0, hd * ATTN_VROWS:hd * ATTN_VROWS + ATTN_VD, :] = vt[hd * ATTN_VD:(hd + 1) * ATTN_VD]
        v_ref[0, hd * ATTN_VROWS + ATTN_VD:(hd + 1) * ATTN_VROWS, :] = ones_rows
    us_ref[...] = p[:, SSM_OFF:POOL_OFF]
    up_ref[...] = p[:, POOL_OFF:IN_W]


def _seg_map(tiles_per_batch, nb):
    def seg(i):
        return jnp.minimum(i // tiles_per_batch, nb)
    return seg


def _inproj(x, g, mods, w, cos_t, sin_t, tiles_per_batch, nb):
    t, d = x.shape
    tm = ROW_TILE
    seg = _seg_map(tiles_per_batch, nb)
    row = lambda i: (i, 0)
    return pl.pallas_call(
        _inproj_kernel,
        grid=(t // tm,),
        in_specs=[pl.BlockSpec((tm, d), row),
                  pl.BlockSpec((1, d), lambda i: (0, 0)),
                  pl.BlockSpec((1, N_MOD, d), lambda i: (seg(i), 0, 0)),
                  pl.BlockSpec((d, IN_W), lambda i: (0, 0)),
                  pl.BlockSpec((tm, LANE), row),
                  pl.BlockSpec((tm, LANE), row)],
        out_specs=[pl.BlockSpec((tm, ATTN_W), row),
                   pl.BlockSpec((tm, ATTN_W), row),
                   pl.BlockSpec((1, ATTN_HEADS * ATTN_VROWS, tm), lambda i: (i, 0, 0)),
                   pl.BlockSpec((tm, SSM_W), row),
                   pl.BlockSpec((tm, POOL_W), row)],
        out_shape=[jax.ShapeDtypeStruct((t, ATTN_W), BF16),
                   jax.ShapeDtypeStruct((t, ATTN_W), BF16),
                   jax.ShapeDtypeStruct((t // tm, ATTN_HEADS * ATTN_VROWS, tm), BF16),
                   jax.ShapeDtypeStruct((t, SSM_W), F32),
                   jax.ShapeDtypeStruct((t, POOL_W), F32)],
        compiler_params=_cparams(("arbitrary",)),
        name="inproj",
    )(x, g, mods, w, cos_t, sin_t)


def _attn_stages(q, tq):
    lane = lax.broadcasted_iota(jnp.int32, (tq, LANE), 1)
    zero = jnp.zeros_like(q)
    qs = jnp.concatenate([jnp.where(lane < ATTN_HD, q, zero),
                          jnp.where(lane >= ATTN_HD, q, zero)], axis=0)

    def scores(kb):
        return lax.dot_general(kb, qs, (((1,), (1,)), ((), ())), preferred_element_type=F32)

    def softmax(m, s):
        m_new = jnp.maximum(m, jnp.max(s, axis=0, keepdims=True))
        return m_new, jnp.exp2(m - m_new), jnp.exp2(s - m_new).astype(BF16)

    def weighted(acc, alpha, pb, vts):
        acc = alpha * acc
        rows = pb.shape[0] // len(vts)
        for c, vt in enumerate(vts):
            acc = acc + jnp.dot(vt, pb[c * rows:(c + 1) * rows], preferred_element_type=F32)
        return acc

    return scores, softmax, weighted


def _attn_finish(lam, acc, g, tq, out_scale):
    o = acc[:ATTN_VD] / acc[ATTN_VD:ATTN_VD + 1]
    o = (o[:, :tq] - lam * o[:, tq:]).T
    ms = jnp.mean(o * o, axis=-1, keepdims=True)
    return (o * lax.rsqrt(ms + EPS) * g * out_scale).astype(BF16)


def _attn_ctx_kernel(lam_ref, q_ref, kc_ref, vc_ref, g_ref, o_ref, *, out_scale):
    tq = q_ref.shape[0]
    scores, softmax, weighted = _attn_stages(q_ref[...], tq)
    m = jnp.full((1, 2 * tq), NEG_BIG, F32)
    acc = jnp.zeros((ATTN_VROWS, 2 * tq), F32)
    m, alpha, pb = softmax(m, scores(kc_ref[...]))
    acc = weighted(acc, alpha, pb, [vc_ref[0]])
    o_ref[...] = _attn_finish(lam_ref[0], acc, g_ref[...], tq, out_scale)


def _attn_kernel(lam_ref, q_ref, kl_ref, kc_ref, vl_ref, vc_ref, g_ref, o_ref,
                 s_a, s_b, p_a, p_b, acc_ref, *, n_lat_k, tk, out_scale):
    tq = q_ref.shape[0]
    scores, softmax, weighted = _attn_stages(q_ref[...], tq)
    n_sub = tk // vc_ref.shape[-1]

    def stage_scores(t, s_buf):
        start = pl.multiple_of(t * tk, tk)
        s_buf[...] = scores(kl_ref[pl.ds(start, tk), :])

    def stage_softmax(m, s_buf, p_buf):
        m, alpha, pb = softmax(m, s_buf[...])
        p_buf[...] = pb
        return m, alpha

    def stage_values(alpha, p_buf, t):
        acc_ref[...] = weighted(acc_ref[...], alpha, p_buf[...],
                                [vl_ref[t * n_sub + c] for c in range(n_sub)])

    m = jnp.full((1, 2 * tq), NEG_BIG, F32)
    m, alpha, pb = softmax(m, scores(kc_ref[...]))
    acc_ref[...] = weighted(jnp.zeros((ATTN_VROWS, 2 * tq), F32), alpha, pb, [vc_ref[0]])

    if n_lat_k % 2 == 0:
        stage_scores(0, s_a)
        stage_scores(1, s_b)
        m, alpha = stage_softmax(m, s_a, p_a)

        def pair(i, carry):
            m, alpha = carry
            k = 2 * i
            stage_scores(k + 2, s_a)
            m, alpha_n = stage_softmax(m, s_b, p_b)
            stage_values(alpha, p_a, k)
            stage_scores(k + 3, s_b)
            m, alpha_nn = stage_softmax(m, s_a, p_a)
            stage_values(alpha_n, p_b, k + 1)
            return m, alpha_nn

        m, alpha = lax.fori_loop(0, n_lat_k // 2 - 1, pair, (m, alpha))
        m, alpha_n = stage_softmax(m, s_b, p_b)
        stage_values(alpha, p_a, n_lat_k - 2)
        stage_values(alpha_n, p_b, n_lat_k - 1)
    else:
        for t in range(n_lat_k):
            stage_scores(t, s_a)
            m, alpha = stage_softmax(m, s_a, p_a)
            stage_values(alpha, p_a, t)
    o_ref[...] = _attn_finish(lam_ref[0], acc_ref[...], g_ref[...], tq, out_scale)


def _attention(q, k, vt, lam, g, nb, l_lat, l_ctx, with_ctx_queries, out_scale):
    tq_ctx = ROW_TILE
    tq = 512 if l_lat % 512 == 0 else ROW_TILE
    tk = 512 if l_lat % 512 == 0 else l_lat
    vt_tile = vt.shape[-1]
    assert l_ctx == tq_ctx and l_lat % tq == 0 and vt_tile == l_ctx and tk % vt_tile == 0
    n_lat_q = l_lat // tq
    n_vt_lat = l_lat // vt_tile
    ctx_blk0 = nb * l_lat // l_ctx
    smem = pl.BlockSpec(memory_space=pltpu.SMEM)
    gain = pl.BlockSpec((1, LANE), lambda b, h, i: (0, 0))
    kc_spec = pl.BlockSpec((l_ctx, LANE), lambda b, h, i: (ctx_blk0 + b, h))
    vc_spec = pl.BlockSpec((1, ATTN_VROWS, vt_tile), lambda b, h, i: (nb * n_vt_lat + b, h, 0))
    sem = _cparams(("arbitrary", "arbitrary", "arbitrary"))
    lat = pl.pallas_call(
        functools.partial(_attn_kernel, n_lat_k=l_lat // tk, tk=tk, out_scale=out_scale),
        grid=(nb, ATTN_HEADS, n_lat_q),
        in_specs=[smem,
                  pl.BlockSpec((tq, LANE), lambda b, h, i: (b * n_lat_q + i, h)),
                  pl.BlockSpec((l_lat, LANE), lambda b, h, i: (b, h)),
                  kc_spec,
                  pl.BlockSpec((n_vt_lat, ATTN_VROWS, vt_tile), lambda b, h, i: (b, h, 0)),
                  vc_spec, gain],
        out_specs=pl.BlockSpec((tq, LANE), lambda b, h, i: (b * n_lat_q + i, h)),
        out_shape=jax.ShapeDtypeStruct((nb * l_lat, ATTN_W), BF16),
        scratch_shapes=[pltpu.VMEM((tk, 2 * tq), F32), pltpu.VMEM((tk, 2 * tq), F32),
                        pltpu.VMEM((tk, 2 * tq), BF16), pltpu.VMEM((tk, 2 * tq), BF16),
                        pltpu.VMEM((ATTN_VROWS, 2 * tq), F32)],
        compiler_params=sem,
        name="diff_attention",
    )(lam, q, k, k, vt, vt, g)
    if not with_ctx_queries:
        return lat
    ctx = pl.pallas_call(
        functools.partial(_attn_ctx_kernel, out_scale=out_scale),
        grid=(nb, ATTN_HEADS, 1),
        in_specs=[smem,
                  pl.BlockSpec((tq_ctx, LANE), lambda b, h, i: (ctx_blk0 + b, h)),
                  kc_spec, vc_spec, gain],
        out_specs=pl.BlockSpec((tq_ctx, LANE), lambda b, h, i: (b, h)),
        out_shape=jax.ShapeDtypeStruct((nb * l_ctx, ATTN_W), BF16),
        compiler_params=sem,
        name="diff_attention_ctx",
    )(lam, q, k, vt, g)
    return jnp.concatenate([lat, ctx], axis=0)


def _ssm_kernel(ul_ref, uc_ref, m_ref, bm_ref, cm_ref, a_ref, d_ref, yl_ref, yc_ref,
                s_scr, h_scr, *, ctx_out):
    nl = ul_ref.shape[2]
    nc = uc_ref.shape[2]
    half = s_scr.shape[-1] // 2
    ul = ul_ref[0, 0]
    uc = uc_ref[0, 0]
    ulb = ul.astype(BF16)
    ucb = uc.astype(BF16)
    d = d_ref[0]
    yl = jnp.dot(ulb, m_ref[0], preferred_element_type=F32) + ul * d
    if ctx_out:
        yc = jnp.dot(ucb, m_ref[0], preferred_element_type=F32) + uc * d
    for dr in range(2):
        s_scr[dr, 0:nc, :] = jnp.dot(ucb, bm_ref[dr, 0], preferred_element_type=F32)
        s_scr[dr, nc:nc + nl, :] = jnp.dot(ulb, bm_ref[dr, 0], preferred_element_type=F32)
    a = a_ref[0]
    afr, afi, arr, ari = a[0:1], a[1:2], a[2:3], a[3:4]

    def make_body(base, n):
        def body(t, carry):
            fr, fi, rr, ri = carry
            rf = base + t
            rv = base + n - 1 - t
            h_scr[0, pl.ds(rf, 1), :] = jnp.concatenate([fr, fi], axis=-1)
            h_scr[1, pl.ds(rv, 1), :] = jnp.concatenate([rr, ri], axis=-1)
            sf = s_scr[0, pl.ds(rf, 1), :]
            sv = s_scr[1, pl.ds(rv, 1), :]
            nfr = afr * fr - afi * fi + sf[:, :half]
            nfi = afr * fi + afi * fr + sf[:, half:]
            nrr = arr * rr - ari * ri + sv[:, :half]
            nri = arr * ri + ari * rr + sv[:, half:]
            return nfr, nfi, nrr, nri
        return body

    z = jnp.zeros((1, half), F32)
    carry = lax.fori_loop(0, nc, make_body(0, nc), (z, z, z, z))
    lax.fori_loop(0, nl, make_body(nc, nl), carry)
    for dr in range(2):
        yl = yl + jnp.dot(h_scr[dr, nc:nc + nl, :].astype(BF16), cm_ref[dr, 0],
                          preferred_element_type=F32)
        if ctx_out:
            yc = yc + jnp.dot(h_scr[dr, 0:nc, :].astype(BF16), cm_ref[dr, 0],
                              preferred_element_type=F32)
    yl_ref[0, 0] = yl
    if ctx_out:
        yc_ref[0, 0] = yc
    else:
        yc_ref[0, 0] = jnp.zeros_like(uc)


def _ssm_matrices(lam_re, lam_im, log_step, b_re, b_im, c_re, c_im, d):
    tc = SSM_CHUNK
    g, p, hh = SSM_GROUPS, SSM_STATE, SSM_GROUP
    npair = g // 2
    step = jnp.exp(log_step)[..., None]
    den = lam_re * lam_re + lam_im * lam_im

    def power(k):
        er = jnp.exp(lam_re * step * k)
        return er * jnp.cos(lam_im * step * k), er * jnp.sin(lam_im * step * k)

    ar, ai = power(1.0)
    nr = ar - 1.0
    cr_ = (nr * lam_re + ai * lam_im) / den
    ci_ = (ai * lam_re - nr * lam_im) / den
    bbr = cr_[..., None] * b_re - ci_[..., None] * b_im
    bbi = cr_[..., None] * b_im + ci_[..., None] * b_re
    ks = jnp.arange(tc + 1, dtype=F32)
    pw = jax.vmap(power)(ks)
    pwr, pwi = pw
    cpr = c_re[None] * pwr[:, :, :, None, :] - c_im[None] * pwi[:, :, :, None, :]
    cpi = c_re[None] * pwi[:, :, :, None, :] + c_im[None] * pwr[:, :, :, None, :]
    hp = lax.Precision.HIGHEST
    kk = (jnp.einsum('kdgop,dgpi->kdgoi', cpr[:tc], bbr, precision=hp)
          - jnp.einsum('kdgop,dgpi->kdgoi', cpi[:tc], bbi, precision=hp))
    s_idx = jnp.arange(tc)[:, None]
    t_idx = jnp.arange(tc)[None, :]
    lag_f = t_idx - s_idx
    lag_r = s_idx - t_idx
    mf = jnp.where((lag_f >= 0)[:, :, None, None, None], kk[jnp.clip(lag_f, 0, tc - 1), 0], 0.0)
    mr = jnp.where((lag_r >= 0)[:, :, None, None, None], kk[jnp.clip(lag_r, 0, tc - 1), 1], 0.0)
    mm = mf + mr
    mm = jnp.transpose(mm, (2, 0, 4, 1, 3)).reshape(g, tc * hh, tc * hh)
    zero = jnp.zeros_like(mm[0::2])
    m2 = jnp.concatenate([jnp.concatenate([mm[0::2], zero], -1),
                          jnp.concatenate([zero, mm[1::2]], -1)], -2)
    def bmat(dr, exps):
        pr = pwr[exps, dr]
        pi = pwi[exps, dr]
        re = pr[..., None] * bbr[dr][None] - pi[..., None] * bbi[dr][None]
        im = pr[..., None] * bbi[dr][None] + pi[..., None] * bbr[dr][None]
        re = jnp.transpose(re, (1, 0, 3, 2)).reshape(g, tc * hh, p)
        im = jnp.transpose(im, (1, 0, 3, 2)).reshape(g, tc * hh, p)
        z = jnp.zeros_like(re[0::2])
        top = jnp.concatenate([re[0::2], z, im[0::2], z], -1)
        bot = jnp.concatenate([z, re[1::2], z, im[1::2]], -1)
        return jnp.concatenate([top, bot], -2)
    bm = jnp.stack([bmat(0, jnp.arange(tc - 1, -1, -1)), bmat(1, jnp.arange(tc))])
    def cmat(dr, exps):
        re = jnp.transpose(cpr[exps, dr], (1, 3, 0, 2)).reshape(g, p, tc * hh)
        im = jnp.transpose(cpi[exps, dr], (1, 3, 0, 2)).reshape(g, p, tc * hh)
        z = jnp.zeros_like(re[0::2])
        return jnp.concatenate([jnp.concatenate([re[0::2], z], -1),
                                jnp.concatenate([z, re[1::2]], -1),
                                jnp.concatenate([-im[0::2], z], -1),
                                jnp.concatenate([z, -im[1::2]], -1)], -2)
    cm = jnp.stack([cmat(0, jnp.arange(1, tc + 1)), cmat(1, jnp.arange(tc, 0, -1))])
    a16r = pwr[tc].reshape(2, npair, 2 * p)
    a16i = pwi[tc].reshape(2, npair, 2 * p)
    a16 = jnp.stack([a16r[0], a16i[0], a16r[1], a16i[1]], axis=1)
    dd = jnp.broadcast_to(d.reshape(npair, 2, 1, hh), (npair, 2, tc, hh)).reshape(npair, 1, 2 * tc * hh)
    return m2.astype(BF16), bm.astype(BF16), cm.astype(BF16), a16, dd


def _ssm(us, mats, nb, l_lat, l_ctx, ctx_out):
    m2, bm, cm, a16, dd = mats
    tc, hh = SSM_CHUNK, SSM_GROUP
    npair = SSM_GROUPS // 2
    w = 2 * tc * hh

    def fold(u, n):
        u = u.reshape(nb, n // tc, tc, npair, 2, hh)
        return jnp.transpose(u, (0, 3, 1, 4, 2, 5)).reshape(nb, npair, n // tc, w)

    def unfold(y, n):
        y = y.reshape(nb, npair, n // tc, 2, tc, hh)
        return jnp.transpose(y, (0, 2, 4, 1, 3, 5)).reshape(nb * n, SSM_W)

    ul = fold(us[:nb * l_lat], l_lat)
    uc = fold(us[nb * l_lat:], l_ctx)
    nl, nc = l_lat // tc, l_ctx // tc
    yl, yc = pl.pallas_call(
        functools.partial(_ssm_kernel, ctx_out=ctx_out),
        grid=(nb, npair),
        in_specs=[pl.BlockSpec((1, 1, nl, w), lambda b, q: (b, q, 0, 0)),
                  pl.BlockSpec((1, 1, nc, w), lambda b, q: (b, q, 0, 0)),
                  pl.BlockSpec((1, w, w), lambda b, q: (q, 0, 0)),
                  pl.BlockSpec((2, 1, w, 4 * SSM_STATE), lambda b, q: (0, q, 0, 0)),
                  pl.BlockSpec((2, 1, 4 * SSM_STATE, w), lambda b, q: (0, q, 0, 0)),
                  pl.BlockSpec((1, 4, 2 * SSM_STATE), lambda b, q: (q, 0, 0)),
                  pl.BlockSpec((1, 1, w), lambda b, q: (q, 0, 0))],
        out_specs=[pl.BlockSpec((1, 1, nl, w), lambda b, q: (b, q, 0, 0)),
                   pl.BlockSpec((1, 1, nc, w), lambda b, q: (b, q, 0, 0))],
        out_shape=[jax.ShapeDtypeStruct((nb, npair, nl, w), F32),
                   jax.ShapeDtypeStruct((nb, npair, nc, w), F32)],
        scratch_shapes=[pltpu.VMEM((2, nc + nl, 4 * SSM_STATE), F32),
                        pltpu.VMEM((2, nc + nl, 4 * SSM_STATE), F32)],
        compiler_params=_cparams(("arbitrary", "arbitrary")),
        name="s5_scan",
    )(ul, uc, m2, bm, cm, a16, dd)
    y = unfold(yl, l_lat)
    if ctx_out:
        y = jnp.concatenate([y, unfold(yc, l_ctx)], axis=0)
    return y


def _pool_kernel(prev_ref, cur_ref, next_ref, w_ref, scale_ref, o_ref, ext,
                 *, n_lat_tiles, tiles_per_lat, l_lat, l_ctx):
    r = cur_ref.shape[0]
    hl = POOL_HALO
    i = pl.program_id(0)
    is_lat = i < n_lat_tiles
    pos = jnp.where(is_lat, i % tiles_per_lat, 0)
    n_tiles = jnp.where(is_lat, tiles_per_lat, l_ctx // r)
    seq_len = jnp.where(is_lat, l_lat, l_ctx)
    zeros = jnp.zeros((hl, POOL_W), F32)
    ext[0:hl, :] = jnp.where(pos > 0, prev_ref[...], zeros)
    ext[hl:hl + r, :] = cur_ref[...]
    ext[hl + r:hl + r + hl, :] = jnp.where(pos < n_tiles - 1, next_ref[...], zeros)
    u = cur_ref[...]
    t = pos * r + lax.broadcasted_iota(jnp.int32, (r, 1), 0)
    lane = lax.broadcasted_iota(jnp.int32, (r, POOL_W), 1)

    def win(k):
        return ext[hl + k:hl + k + r, :]

    acc = win(-1) + u
    mean = jnp.zeros((r, POOL_W), F32)
    lo_k, hi_k = -1, 0
    for gi, wn in enumerate(POOL_WINDOWS):
        hw = wn // 2
        while lo_k > -hw:
            lo_k -= 1
            acc = acc + win(lo_k)
        while hi_k < hw - 1:
            hi_k += 1
            acc = acc + win(hi_k)
        cnt = (jnp.minimum(t + hw, seq_len) - jnp.maximum(t - hw, 0)).astype(F32)
        sel = (lane >= gi * POOL_GROUP) & (lane < (gi + 1) * POOL_GROUP)
        mean = jnp.where(sel, acc / cnt, mean)
    dlt = (mean - u).astype(BF16)
    o_ref[...] = jnp.dot(dlt, w_ref[...], preferred_element_type=F32) * scale_ref[...]


def _pool(up, w_blk, scale, nb, l_lat, l_ctx, n_rows):
    r = ROW_TILE
    hl = POOL_HALO
    n_tiles = n_rows // r
    last8 = up.shape[0] // hl - 1
    kern = functools.partial(_pool_kernel, n_lat_tiles=nb * l_lat // r, tiles_per_lat=l_lat // r,
                             l_lat=l_lat, l_ctx=l_ctx)
    return pl.pallas_call(
        kern,
        grid=(n_tiles,),
        in_specs=[pl.BlockSpec((hl, POOL_W), lambda i: (jnp.maximum(i * (r // hl) - 1, 0), 0)),
                  pl.BlockSpec((r, POOL_W), lambda i: (i, 0)),
                  pl.BlockSpec((hl, POOL_W), lambda i: (jnp.minimum((i + 1) * (r // hl), last8), 0)),
                  pl.BlockSpec((POOL_W, POOL_W), lambda i: (0, 0)),
                  pl.BlockSpec((1, POOL_W), lambda i: (0, 0))],
        out_specs=pl.BlockSpec((r, POOL_W), lambda i: (i, 0)),
        out_shape=jax.ShapeDtypeStruct((n_rows, POOL_W), F32),
        scratch_shapes=[pltpu.VMEM((r + 2 * hl, POOL_W), F32)],
        compiler_params=_cparams(("arbitrary",)),
        name="pool_mix",
    )(up, up, up, w_blk, scale)


def _mixout_kernel(x_ref, attn_ref, y_ref, pool_ref, mod_ref, wglu_ref, wout_ref, o_ref):
    yg = _gelu(y_ref[...])
    z = yg * jax.nn.sigmoid(jnp.dot(yg.astype(BF16), wglu_ref[...], preferred_element_type=F32))
    r = jnp.dot(attn_ref[...], wout_ref[0:ATTN_W, :], preferred_element_type=F32)
    r = r + jnp.dot(z.astype(BF16), wout_ref[ATTN_W:ATTN_W + SSM_W, :], preferred_element_type=F32)
    r = r + jnp.dot(pool_ref[...].astype(BF16), wout_ref[ATTN_W + SSM_W:, :],
                    preferred_element_type=F32)
    o_ref[...] = x_ref[...] + mod_ref[0, 2:3, :] * r


def _mixout(x, attn, y, pool, mods, wglu, wout, tiles_per_batch, nb, n_rows):
    d = x.shape[1]
    tm = ROW_TILE
    seg = _seg_map(tiles_per_batch, nb)
    row = lambda i: (i, 0)
    return pl.pallas_call(
        _mixout_kernel,
        grid=(n_rows // tm,),
        in_specs=[pl.BlockSpec((tm, d), row),
                  pl.BlockSpec((tm, ATTN_W), row),
                  pl.BlockSpec((tm, SSM_W), row),
                  pl.BlockSpec((tm, POOL_W), row),
                  pl.BlockSpec((1, N_MOD, d), lambda i: (seg(i), 0, 0)),
                  pl.BlockSpec((SSM_W, SSM_W), lambda i: (0, 0)),
                  pl.BlockSpec((d, d), lambda i: (0, 0))],
        out_specs=pl.BlockSpec((tm, d), row),
        out_shape=jax.ShapeDtypeStruct((n_rows, d), F32),
        compiler_params=_cparams(("arbitrary",)),
        name="mix_out",
    )(x, attn, y, pool, mods, wglu, wout)


def _top_sorted(s, k):
    rows = []
    for _ in range(k):
        m = jnp.max(s, axis=0, keepdims=True)
        rows.append(m)
        s = jnp.where(s == m, NEG_BIG, s)
    return rows


def _peer_score_kernel(x_ref, g_ref, mod_ref, wq_ref, keys_ref,
                       h_ref, s1_ref, s2_ref, e1_ref, e2_ref, tau_ref, q_scr):
    tm = x_ref.shape[0]
    h = _norm_mod(x_ref[...], g_ref[...], mod_ref[0, 3:4, :], mod_ref[0, 4:5, :]).astype(BF16)
    h_ref[...] = h
    q = jnp.dot(h, wq_ref[...], preferred_element_type=F32)
    for hc in range(2 * PEER_HEADS):
        q_scr[hc] = q[:, hc * PEER_KDIM:(hc + 1) * PEER_KDIM].astype(BF16)
    row8 = lax.broadcasted_iota(jnp.int32, (8, tm), 0)

    def head(hd, _):
        nt = (((1,), (1,)), ((), ()))
        s1 = lax.dot_general(keys_ref[hd, 0], q_scr[2 * hd], nt,
                             preferred_element_type=F32)
        s2 = lax.dot_general(keys_ref[hd, 1], q_scr[2 * hd + 1], nt,
                             preferred_element_type=F32)
        a = _top_sorted(s1, PEER_TOPK)
        b = _top_sorted(s2, PEER_TOPK)
        acat = jnp.concatenate(a, axis=0)
        bcat = jnp.concatenate(b, axis=0)
        pieces = [a[0] + bcat]
        for i in range(2, 9):
            piece = a[i - 1] + bcat[0:8]
            n_valid = PEER_TOPK // i
            pieces.append(piece if n_valid >= 8 else jnp.where(row8 < n_valid, piece, NEG_BIG))
        pieces.append(acat[8:16] + b[0])
        cand = jnp.concatenate(pieces, axis=0)
        work = cand
        cum = jnp.zeros((1, tm), F32)
        tau = jnp.full((1, tm), NEG_BIG, F32)
        for _ in range(PEER_TOPK):
            m = jnp.max(work, axis=0, keepdims=True)
            eq = work == m
            new = cum + jnp.sum(eq.astype(F32), axis=0, keepdims=True)
            tau = jnp.where((cum < PEER_TOPK) & (new >= PEER_TOPK), m, tau)
            work = jnp.where(eq, NEG_BIG, work)
            cum = new
        top = a[0] + b[0]
        z = jnp.sum(jnp.where(cand >= tau, jnp.exp(cand - top), 0.0), axis=0, keepdims=True)
        s1_ref[hd] = s1
        s2_ref[hd] = s2
        e1_ref[hd] = jnp.exp(s1 - a[0])
        e2_ref[hd] = jnp.exp(s2 - b[0]) * (1.0 / z)
        tau_ref[pl.ds(hd, 1), :] = tau
        return 0

    lax.fori_loop(0, PEER_HEADS, head, 0, unroll=4)


def _peer_scores(x, g, mods, wq, keys, tiles_per_batch, nb, n_rows):
    d = x.shape[1]
    tm = ROW_TILE
    seg = _seg_map(tiles_per_batch, nb)
    nq = wq.shape[1]
    col = lambda i: (0, 0, i)
    big = jax.ShapeDtypeStruct((PEER_HEADS, PEER_NKEYS, n_rows), F32)
    return pl.pallas_call(
        _peer_score_kernel,
        grid=(n_rows // tm,),
        in_specs=[pl.BlockSpec((tm, d), lambda i: (i, 0)),
                  pl.BlockSpec((1, d), lambda i: (0, 0)),
                  pl.BlockSpec((1, N_MOD, d), lambda i: (seg(i), 0, 0)),
                  pl.BlockSpec((d, nq), lambda i: (0, 0)),
                  pl.BlockSpec((PEER_HEADS, 2, PEER_NKEYS, PEER_KDIM), lambda i: (0, 0, 0, 0))],
        out_specs=[pl.BlockSpec((tm, d), lambda i: (i, 0)),
                   pl.BlockSpec((PEER_HEADS, PEER_NKEYS, tm), col),
                   pl.BlockSpec((PEER_HEADS, PEER_NKEYS, tm), col),
                   pl.BlockSpec((PEER_HEADS, PEER_NKEYS, tm), col),
                   pl.BlockSpec((PEER_HEADS, PEER_NKEYS, tm), col),
                   pl.BlockSpec((PEER_HEADS, tm), lambda i: (0, i))],
        out_shape=[jax.ShapeDtypeStruct((n_rows, d), BF16), big, big, big, big,
                   jax.ShapeDtypeStruct((PEER_HEADS, n_rows), F32)],
        scratch_shapes=[pltpu.VMEM((2 * PEER_HEADS, tm, PEER_KDIM), BF16)],
        compiler_params=_cparams(("arbitrary",)),
        name="peer_scores",
    )(x, g, mods, wq, keys)


def _transpose_cast_kernel(x_ref, o_ref):
    o_ref[...] = x_ref[...].T.astype(o_ref.dtype)


def _transpose_cast(x, dtype, rows_per_step=512):
    n, d = x.shape
    return pl.pallas_call(
        _transpose_cast_kernel,
        grid=(n // rows_per_step,),
        in_specs=[pl.BlockSpec((rows_per_step, d), lambda i: (i, 0))],
        out_specs=pl.BlockSpec((d, rows_per_step), lambda i: (0, i)),
        out_shape=jax.ShapeDtypeStruct((d, n), dtype),
        compiler_params=_cparams(("arbitrary",)),
        name="transpose_cast",
    )(x)


PEER_CHUNK_KEYS = 8
PEER_TOK_TILE = 512
PEER_SUB = 256


def _peer_dense_kernel(h_ref, x_ref, mod_ref, u_ref, vt_ref, s1_ref, s2_ref, e1_ref, e2_ref,
                       tau_ref, fg_ref, o_ref, acc_ref, a0, a1, w0, w1, bs_scr, be_scr,
                       *, n_chunks, final_norm):
    n_i = PEER_CHUNK_KEYS
    tm = h_ref.shape[0]
    g = pl.program_id(0)
    nt_dims = (((1,), (1,)), ((), ()))
    n_k = u_ref.shape[1] // PEER_SUB
    n_lb = tm // LANE
    assert n_k == n_lb == vt_ref.shape[1] // PEER_SUB and n_i * PEER_NKEYS == u_ref.shape[0]
    n_mt = u_ref.shape[0] // PEER_SUB

    @pl.when(g == 0)
    def _():
        acc_ref[...] = jnp.zeros_like(acc_ref)
        a1[...] = jnp.zeros_like(a1)
        w0[...] = jnp.zeros_like(w0)

    for hd in range(PEER_HEADS):
        s1b = s1_ref[hd]
        e1b = e1_ref[hd]
        for ii in range(n_i):
            bs_scr[hd, ii] = jnp.broadcast_to(s1b[ii:ii + 1], (8, tm))
            be_scr[hd, ii] = jnp.broadcast_to(e1b[ii:ii + 1], (8, tm))

    def run(a_wr, a_rd, w_wr, w_rd):
        for lb in range(n_lb):
            th, lo = divmod(lb * LANE, PEER_SUB)
            ls = slice(lb * LANE, (lb + 1) * LANE)
            lh = slice(lo, lo + LANE)
            ks = slice(lb * PEER_SUB, (lb + 1) * PEER_SUB)
            taus = [jnp.broadcast_to(tau_ref[hd:hd + 1, ls], (8, LANE)) for hd in range(PEER_HEADS)]

            def slab(jb, _):
                nt = jb // n_mt
                m0 = pl.multiple_of((jb % n_mt) * PEER_SUB, PEER_SUB)
                t0 = pl.multiple_of(nt * PEER_SUB, PEER_SUB)
                part_a = lax.dot_general(u_ref[pl.ds(m0, PEER_SUB), ks], h_ref[pl.ds(t0, PEER_SUB), ks],
                                         nt_dims, preferred_element_type=F32)
                part_v = jnp.dot(vt_ref[pl.ds(m0, PEER_SUB), ks], w_rd[nt, ks, :],
                                 preferred_element_type=F32)
                j0 = pl.multiple_of(jb * 16, 16)
                gs = [[jnp.zeros((8, LANE), F32), jnp.zeros((8, LANE), F32)] for _ in range(n_i)]
                for hd in range(PEER_HEADS):
                    s2v = s2_ref[hd, pl.ds(j0, 16), ls]
                    e2v = e2_ref[hd, pl.ds(j0, 16), ls]
                    for ii in range(n_i):
                        bs = bs_scr[hd, ii, :, ls]
                        be = be_scr[hd, ii, :, ls]
                        for hf in range(2):
                            csum = bs + s2v[hf * 8:(hf + 1) * 8]
                            gs[ii][hf] = gs[ii][hf] + jnp.where(csum >= taus[hd],
                                                               e2v[hf * 8:(hf + 1) * 8] * be, 0.0)
                for ii in range(n_i):
                    r0 = pl.multiple_of(ii * PEER_NKEYS + j0, 16)
                    g16 = jnp.concatenate(gs[ii], axis=0)
                    w_wr[th, pl.ds(r0, 16), lh] = (_gelu(a_rd[th, pl.ds(r0, 16), lh]) * g16).astype(BF16)
                if lb == 0:
                    a_wr[nt, pl.ds(m0, PEER_SUB), :] = part_a
                else:
                    a_wr[nt, pl.ds(m0, PEER_SUB), :] += part_a
                acc_ref[nt, pl.ds(m0, PEER_SUB), :] += part_v
                return 0

            lax.fori_loop(0, PEER_NKEYS // 16, slab, 0, unroll=True)

    @pl.when(g % 2 == 0)
    def _():
        run(a0, a1, w1, w0)

    @pl.when(g % 2 == 1)
    def _():
        run(a1, a0, w0, w1)

    @pl.when((g >= 2) & ((g - 2) % n_chunks == n_chunks - 1))
    def _():
        out = x_ref[...] + mod_ref[0, 5:6, :] * jnp.concatenate(
            [acc_ref[t].T for t in range(acc_ref.shape[0])], axis=0)
        if final_norm:
            ms = jnp.mean(out * out, axis=-1, keepdims=True)
            out = out * lax.rsqrt(ms + EPS) * fg_ref[...]
        o_ref[...] = out
        acc_ref[...] = jnp.zeros_like(acc_ref)


def _peer_dense(h, x, mods, u, vt, s1, s2, e1, e2, tau, fg, tiles_per_batch, nb, n_rows, final_norm):
    d = x.shape[1]
    tm = PEER_TOK_TILE
    n_i = PEER_CHUNK_KEYS
    ne = n_i * PEER_NKEYS
    tiles_per_batch = tiles_per_batch * ROW_TILE // tm
    n_chunks = u.shape[0] // ne
    n_steps = (n_rows // tm) * n_chunks
    assert n_rows % tm == 0 and d == ne
    assert PEER_NKEYS // 16 == (tm // PEER_SUB) * (ne // PEER_SUB)

    def stage(lag):
        def split(g):
            n = jnp.clip(g - lag, 0, n_steps - 1)
            return n // n_chunks, n % n_chunks
        return split

    act, gate, val = stage(0), stage(1), stage(2)
    seg = lambda i: jnp.minimum(i // tiles_per_batch, nb)
    kern = functools.partial(_peer_dense_kernel, n_chunks=n_chunks, final_norm=final_norm)
    full = (PEER_HEADS, PEER_NKEYS, tm)
    rows = (PEER_HEADS, n_i, tm)
    return pl.pallas_call(
        kern,
        grid=(n_steps + 2,),
        in_specs=[pl.BlockSpec((tm, d), lambda g: (act(g)[0], 0)),
                  pl.BlockSpec((tm, d), lambda g: (val(g)[0], 0)),
                  pl.BlockSpec((1, N_MOD, d), lambda g: (seg(val(g)[0]), 0, 0)),
                  pl.BlockSpec((ne, d), lambda g: (act(g)[1], 0)),
                  pl.BlockSpec((d, ne), lambda g: (0, val(g)[1])),
                  pl.BlockSpec(rows, lambda g: (0, gate(g)[1], gate(g)[0])),
                  pl.BlockSpec(full, lambda g: (0, 0, gate(g)[0])),
                  pl.BlockSpec(rows, lambda g: (0, gate(g)[1], gate(g)[0])),
                  pl.BlockSpec(full, lambda g: (0, 0, gate(g)[0])),
                  pl.BlockSpec((PEER_HEADS, tm), lambda g: (0, gate(g)[0])),
                  pl.BlockSpec((1, d), lambda g: (0, 0))],
        out_specs=pl.BlockSpec((tm, d), lambda g: (val(g)[0], 0)),
        out_shape=jax.ShapeDtypeStruct((n_rows, d), F32),
        scratch_shapes=[pltpu.VMEM((tm // PEER_SUB, d, PEER_SUB), F32),
                        pltpu.VMEM((tm // PEER_SUB, ne, PEER_SUB), F32),
                        pltpu.VMEM((tm // PEER_SUB, ne, PEER_SUB), F32),
                        pltpu.VMEM((tm // PEER_SUB, ne, PEER_SUB), BF16),
                        pltpu.VMEM((tm // PEER_SUB, ne, PEER_SUB), BF16),
                        pltpu.VMEM((PEER_HEADS, n_i, 8, tm), F32),
                        pltpu.VMEM((PEER_HEADS, n_i, 8, tm), F32)],
        compiler_params=_cparams(("arbitrary",)),
        name="peer_experts",
    )(h, x, mods, u, vt, s1, s2, e1, e2, tau, fg)


def _rope_tables(l_lat, nb, n_ctx_rows):
    rows = l_lat // GRID_W
    r = jnp.repeat(jnp.arange(rows), GRID_W)
    col = jnp.tile(jnp.arange(GRID_W), rows)
    pos = jnp.stack([r, col], axis=-1).astype(F32)
    nf = ATTN_HD // 4
    inv = 1.0 / (ROPE_BASE ** (jnp.arange(nf, dtype=F32) / nf))
    ang = pos[:, :, None] * inv
    cos, sin = jnp.cos(ang), jnp.sin(ang)
    c64 = jnp.concatenate([cos[:, 0], cos[:, 0], cos[:, 1], cos[:, 1]], axis=-1)
    s64 = jnp.concatenate([-sin[:, 0], sin[:, 0], -sin[:, 1], sin[:, 1]], axis=-1)
    c = jnp.tile(c64, (nb, LANE // ATTN_HD))
    s = jnp.tile(s64, (nb, LANE // ATTN_HD))
    c = jnp.concatenate([c, jnp.ones((n_ctx_rows, LANE), F32)], axis=0)
    s = jnp.concatenate([s, jnp.zeros((n_ctx_rows, LANE), F32)], axis=0)
    return c, s


def kernel(x, c, ctx, c_ctx, w_mod, b_mod, norm1_g, norm2_g, w_in, w_out, lam_q1, lam_k1, lam_q2, lam_k2, subln_g, ssm_lambda_re, ssm_lambda_im, ssm_log_step, ssm_b_re, ssm_b_im, ssm_c_re, ssm_c_im, ssm_d, ssm_w_glu, pool_w, pool_scale, peer_wq, peer_keys, peer_u, peer_v, final_g):
    nb, l_lat, d = x.shape
    l_ctx = ctx.shape[1]
    depth = w_mod.shape[0]
    n_lat = nb * l_lat
    n_all = n_lat + nb * l_ctx
    tiles_per_batch = l_lat // ROW_TILE
    assert l_lat % ROW_TILE == 0 and l_ctx == ROW_TILE and nb + 1 <= 8

    cs = jnp.concatenate([c, c_ctx[None], jnp.zeros((8 - nb - 1, d), F32)], axis=0)
    mod_all = _mod_vectors(cs, w_mod, b_mod)
    cos_t, sin_t = _rope_tables(l_lat, nb, nb * l_ctx)
    xs = jnp.concatenate([x.reshape(n_lat, d), ctx.reshape(nb * l_ctx, d)], axis=0)

    for l in range(depth):
        last = l == depth - 1
        n_rows = n_lat if last else n_all
        mods = mod_all[l, :nb + 1].reshape(nb + 1, N_MOD, d)
        q, k, v, us, up = _inproj(xs, norm1_g[l][None], mods, w_in[l].astype(BF16), cos_t, sin_t,
                                  tiles_per_batch, nb)
        lam_init = 0.8 - 0.6 * math.exp(-0.3 * l)
        lam = (jnp.exp(jnp.sum(lam_q1[l] * lam_k1[l])) - jnp.exp(jnp.sum(lam_q2[l] * lam_k2[l]))
               + lam_init).reshape(1).astype(F32)
        attn = _attention(q, k, v, lam, subln_g[l][None], nb, l_lat, l_ctx, not last,
                          1.0 - lam_init)
        mats = _ssm_matrices(ssm_lambda_re[l], ssm_lambda_im[l], ssm_log_step[l], ssm_b_re[l],
                             ssm_b_im[l], ssm_c_re[l], ssm_c_im[l], ssm_d[l])
        y = _ssm(us, mats, nb, l_lat, l_ctx, not last)
        w_blk = jax.scipy.linalg.block_diag(*[pool_w[l, gi] for gi in range(len(POOL_WINDOWS))])
        pool = _pool(up, w_blk.astype(BF16), pool_scale[l][None], nb, l_lat, l_ctx, n_rows)
        xs = _mixout(xs, attn, y, pool, mods, ssm_w_glu[l].astype(BF16), w_out[l].astype(BF16),
                     tiles_per_batch, nb, n_rows)
        h2, s1, s2, e1, e2, tau = _peer_scores(xs, norm2_g[l][None], mods, peer_wq[l].astype(BF16),
                                               peer_keys[l].astype(BF16), tiles_per_batch, nb, n_rows)
        xs = _peer_dense(h2, xs, mods, peer_u[l].astype(BF16), _transpose_cast(peer_v[l], BF16),
                         s1, s2, e1, e2, tau, final_g[None], tiles_per_batch, nb, n_rows, last)
    return xs.reshape(nb, l_lat, d)
```

```python
import functools
import math

import numpy as np
import jax
import jax.numpy as jnp
from jax import lax
from jax.experimental import pallas as pl
from jax.experimental.pallas import tpu as pltpu

F32 = jnp.float32
BF16 = jnp.bfloat16

EPS = 1e-6
GRID_W = 64
N_MOD = 6
ATTN_HD = 64
ATTN_VD = 128
ATTN_VROWS = ATTN_VD + 16
LOG2E = float(np.log2(np.e))
ATTN_HEADS = 4
ATTN_W = 512
ROPE_BASE = 10000.0
SSM_W = 256
SSM_GROUP = 16
SSM_GROUPS = 16
SSM_STATE = 64
SSM_CHUNK = 16
POOL_W = 256
POOL_WINDOWS = (2, 4, 8, 16)
POOL_GROUP = 64
POOL_HALO = 8
SSM_OFF = 3 * ATTN_W
POOL_OFF = SSM_OFF + SSM_W
IN_W = POOL_OFF + POOL_W
PEER_HEADS = 8
PEER_NKEYS = 128
PEER_KDIM = 128
PEER_TOPK = 16
NEG_BIG = -3.0e38
SQRT_HALF = float(np.sqrt(0.5).astype(np.float32))

LANE = 128
ROW_TILE = 256
VMEM_LIMIT = 56 * 1024 * 1024


def _cparams(sem):
    return pltpu.CompilerParams(dimension_semantics=sem, vmem_limit_bytes=VMEM_LIMIT)


def _gelu(x):
    return 0.5 * x * (1.0 + lax.erf(x * SQRT_HALF))


def _norm_mod(x, g, shift, scale):
    ms = jnp.mean(x * x, axis=-1, keepdims=True)
    y = x * lax.rsqrt(ms + EPS) * g
    return y * (1.0 + scale) + shift


def _mod_kernel(s_ref, w_ref, b_ref, o_ref):
    s = s_ref[...]
    s = s * jax.nn.sigmoid(s)
    o_ref[0] = jnp.dot(s.astype(BF16), w_ref[0].astype(BF16), preferred_element_type=F32) + b_ref[0]


def _mod_vectors(cs, w_mod, b_mod):
    depth, d, n = w_mod.shape
    tn = 1536
    return pl.pallas_call(
        _mod_kernel,
        grid=(depth, n // tn),
        in_specs=[pl.BlockSpec((8, d), lambda l, j: (0, 0)),
                  pl.BlockSpec((1, d, tn), lambda l, j: (l, 0, j)),
                  pl.BlockSpec((1, 1, tn), lambda l, j: (l, 0, j))],
        out_specs=pl.BlockSpec((1, 8, tn), lambda l, j: (l, 0, j)),
        out_shape=jax.ShapeDtypeStruct((depth, 8, n), F32),
        compiler_params=_cparams(("arbitrary", "arbitrary")),
        name="mod_vectors",
    )(cs, w_mod, b_mod.reshape(depth, 1, n))


def _inproj_kernel(x_ref, g_ref, mod_ref, w_ref, cos_ref, sin_ref,
                   q_ref, k_ref, v_ref, us_ref, up_ref):
    tm = x_ref.shape[0]
    h = _norm_mod(x_ref[...], g_ref[...], mod_ref[0, 0:1, :], mod_ref[0, 1:2, :])
    p = jnp.dot(h.astype(BF16), w_ref[...], preferred_element_type=F32)
    c = cos_ref[...]
    s = sin_ref[...]
    lane = lax.broadcasted_iota(jnp.int32, (tm, LANE), 1)
    first = (lane % 32) < 16
    for off, ref, sc in ((0, q_ref, ATTN_HD ** -0.5 * LOG2E), (ATTN_W, k_ref, 1.0)):
        for blk in range(ATTN_W // LANE):
            xb = p[:, off + LANE * blk: off + LANE * (blk + 1)]
            partner = jnp.where(first, pltpu.roll(xb, LANE - 16, 1), pltpu.roll(xb, 16, 1))
            ref[:, LANE * blk: LANE * (blk + 1)] = ((xb * c + partner * s) * sc).astype(BF16)
    vt = p[:, 2 * ATTN_W:SSM_OFF].T.astype(BF16)
    ones_rows = (lax.broadcasted_iota(jnp.int32, (ATTN_VROWS - ATTN_VD, tm), 0) == 0).astype(BF16)
    for hd in range(ATTN_HEADS):
        v_ref[0, hd * ATTN_VROWS:hd * ATTN_VROWS + ATTN_VD, :] = vt[hd * ATTN_VD:(hd + 1) * ATTN_VD]
        v_ref[0, hd * ATTN_VROWS + ATTN_VD:(hd + 1) * ATTN_VROWS, :] = ones_rows
    us_ref[...] = p[:, SSM_OFF:POOL_OFF]
    up_ref[...] = p[:, POOL_OFF:IN_W]


def _seg_map(tiles_per_batch, nb):
    def seg(i):
        return jnp.minimum(i // tiles_per_batch, nb)
    return seg


def _inproj(x, g, mods, w, cos_t, sin_t, tiles_per_batch, nb):
    t, d = x.shape
    tm = ROW_TILE
    seg = _seg_map(tiles_per_batch, nb)
    row = lambda i: (i, 0)
    return pl.pallas_call(
        _inproj_kernel,
        grid=(t // tm,),
        in_specs=[pl.BlockSpec((tm, d), row),
                  pl.BlockSpec((1, d), lambda i: (0, 0)),
                  pl.BlockSpec((1, N_MOD, d), lambda i: (seg(i), 0, 0)),
                  pl.BlockSpec((d, IN_W), lambda i: (0, 0)),
                  pl.BlockSpec((tm, LANE), row),
                  pl.BlockSpec((tm, LANE), row)],
        out_specs=[pl.BlockSpec((tm, ATTN_W), row),
                   pl.BlockSpec((tm, ATTN_W), row),
                   pl.BlockSpec((1, ATTN_HEADS * ATTN_VROWS, tm), lambda i: (i, 0, 0)),
                   pl.BlockSpec((tm, SSM_W), row),
                   pl.BlockSpec((tm, POOL_W), row)],
        out_shape=[jax.ShapeDtypeStruct((t, ATTN_W), BF16),
                   jax.ShapeDtypeStruct((t, ATTN_W), BF16),
                   jax.ShapeDtypeStruct((t // tm, ATTN_HEADS * ATTN_VROWS, tm), BF16),
                   jax.ShapeDtypeStruct((t, SSM_W), F32),
                   jax.ShapeDtypeStruct((t, POOL_W), F32)],
        compiler_params=_cparams(("arbitrary",)),
        name="inproj",
    )(x, g, mods, w, cos_t, sin_t)


def _attn_stages(q, tq):
    lane = lax.broadcasted_iota(jnp.int32, (tq, LANE), 1)
    zero = jnp.zeros_like(q)
    qs = jnp.concatenate([jnp.where(lane < ATTN_HD, q, zero),
                          jnp.where(lane >= ATTN_HD, q, zero)], axis=0)

    def scores(kb):
        return lax.dot_general(kb, qs, (((1,), (1,)), ((), ())), preferred_element_type=F32)

    def softmax(m, s):
        m_new = jnp.maximum(m, jnp.max(s, axis=0, keepdims=True))
        return m_new, jnp.exp2(m - m_new), jnp.exp2(s - m_new).astype(BF16)

    def weighted(acc, alpha, pb, vts):
        acc = alpha * acc
        rows = pb.shape[0] // len(vts)
        for c, vt in enumerate(vts):
            acc = acc + jnp.dot(vt, pb[c * rows:(c + 1) * rows], preferred_element_type=F32)
        return acc

    return scores, softmax, weighted


def _attn_finish(lam, acc, g, tq, out_scale):
    o = acc[:ATTN_VD] / acc[ATTN_VD:ATTN_VD + 1]
    o = (o[:, :tq] - lam * o[:, tq:]).T
    ms = jnp.mean(o * o, axis=-1, keepdims=True)
    return (o * lax.rsqrt(ms + EPS) * g * out_scale).astype(BF16)


def _attn_ctx_kernel(lam_ref, q_ref, kc_ref, vc_ref, g_ref, o_ref, *, out_scale):
    tq = q_ref.shape[0]
    scores, softmax, weighted = _attn_stages(q_ref[...], tq)
    m = jnp.full((1, 2 * tq), NEG_BIG, F32)
    acc = jnp.zeros((ATTN_VROWS, 2 * tq), F32)
    m, alpha, pb = softmax(m, scores(kc_ref[...]))
    acc = weighted(acc, alpha, pb, [vc_ref[0]])
    o_ref[...] = _attn_finish(lam_ref[0], acc, g_ref[...], tq, out_scale)


def _attn_kernel(lam_ref, q_ref, kl_ref, kc_ref, vl_ref, vc_ref, g_ref, o_ref,
                 s_a, s_b, p_a, p_b, acc_ref, *, n_lat_k, tk, out_scale):
    tq = q_ref.shape[0]
    scores, softmax, weighted = _attn_stages(q_ref[...], tq)
    n_sub = tk // vc_ref.shape[-1]

    def stage_scores(t, s_buf):
        start = pl.multiple_of(t * tk, tk)
        s_buf[...] = scores(kl_ref[pl.ds(start, tk), :])

    def stage_softmax(m, s_buf, p_buf):
        m, alpha, pb = softmax(m, s_buf[...])
        p_buf[...] = pb
        return m, alpha

    def stage_values(alpha, p_buf, t):
        acc_ref[...] = weighted(acc_ref[...], alpha, p_buf[...],
                                [vl_ref[t * n_sub + c] for c in range(n_sub)])

    m = jnp.full((1, 2 * tq), NEG_BIG, F32)
    m, alpha, pb = softmax(m, scores(kc_ref[...]))
    acc_ref[...] = weighted(jnp.zeros((ATTN_VROWS, 2 * tq), F32), alpha, pb, [vc_ref[0]])

    if n_lat_k % 2 == 0:
        stage_scores(0, s_a)
        stage_scores(1, s_b)
        m, alpha = stage_softmax(m, s_a, p_a)

        def pair(i, carry):
            m, alpha = carry
            k = 2 * i
            stage_scores(k + 2, s_a)
            m, alpha_n = stage_softmax(m, s_b, p_b)
            stage_values(alpha, p_a, k)
            stage_scores(k + 3, s_b)
            m, alpha_nn = stage_softmax(m, s_a, p_a)
            stage_values(alpha_n, p_b, k + 1)
            return m, alpha_nn

        m, alpha = lax.fori_loop(0, n_lat_k // 2 - 1, pair, (m, alpha))
        m, alpha_n = stage_softmax(m, s_b, p_b)
        stage_values(alpha, p_a, n_lat_k - 2)
        stage_values(alpha_n, p_b, n_lat_k - 1)
    else:
        for t in range(n_lat_k):
            stage_scores(t, s_a)
            m, alpha = stage_softmax(m, s_a, p_a)
            stage_values(alpha, p_a, t)
    o_ref[...] = _attn_finish(lam_ref[0], acc_ref[...], g_ref[...], tq, out_scale)


def _attention(q, k, vt, lam, g, nb, l_lat, l_ctx, with_ctx_queries, out_scale):
    tq_ctx = ROW_TILE
    tq = 512 if l_lat % 512 == 0 else ROW_TILE
    tk = 512 if l_lat % 512 == 0 else l_lat
    vt_tile = vt.shape[-1]
    assert l_ctx == tq_ctx and l_lat % tq == 0 and vt_tile == l_ctx and tk % vt_tile == 0
    n_lat_q = l_lat // tq
    n_vt_lat = l_lat // vt_tile
    ctx_blk0 = nb * l_lat // l_ctx
    smem = pl.BlockSpec(memory_space=pltpu.SMEM)
    gain = pl.BlockSpec((1, LANE), lambda b, h, i: (0, 0))
    kc_spec = pl.BlockSpec((l_ctx, LANE), lambda b, h, i: (ctx_blk0 + b, h))
    vc_spec = pl.BlockSpec((1, ATTN_VROWS, vt_tile), lambda b, h, i: (nb * n_vt_lat + b, h, 0))
    sem = _cparams(("arbitrary", "arbitrary", "arbitrary"))
    lat = pl.pallas_call(
        functools.partial(_attn_kernel, n_lat_k=l_lat // tk, tk=tk, out_scale=out_scale),
        grid=(nb, ATTN_HEADS, n_lat_q),
        in_specs=[smem,
                  pl.BlockSpec((tq, LANE), lambda b, h, i: (b * n_lat_q + i, h)),
                  pl.BlockSpec((l_lat, LANE), lambda b, h, i: (b, h)),
                  kc_spec,
                  pl.BlockSpec((n_vt_lat, ATTN_VROWS, vt_tile), lambda b, h, i: (b, h, 0)),
                  vc_spec, gain],
        out_specs=pl.BlockSpec((tq, LANE), lambda b, h, i: (b * n_lat_q + i, h)),
        out_shape=jax.ShapeDtypeStruct((nb * l_lat, ATTN_W), BF16),
        scratch_shapes=[pltpu.VMEM((tk, 2 * tq), F32), pltpu.VMEM((tk, 2 * tq), F32),
                        pltpu.VMEM((tk, 2 * tq), BF16), pltpu.VMEM((tk, 2 * tq), BF16),
                        pltpu.VMEM((ATTN_VROWS, 2 * tq), F32)],
        compiler_params=sem,
        name="diff_attention",
    )(lam, q, k, k, vt, vt, g)
    if not with_ctx_queries:
        return lat
    ctx = pl.pallas_call(
        functools.partial(_attn_ctx_kernel, out_scale=out_scale),
        grid=(nb, ATTN_HEADS, 1),
        in_specs=[smem,
                  pl.BlockSpec((tq_ctx, LANE), lambda b, h, i: (ctx_blk0 + b, h)),
                  kc_spec, vc_spec, gain],
        out_specs=pl.BlockSpec((tq_ctx, LANE), lambda b, h, i: (b, h)),
        out_shape=jax.ShapeDtypeStruct((nb * l_ctx, ATTN_W), BF16),
        compiler_params=sem,
        name="diff_attention_ctx",
    )(lam, q, k, vt, g)
    return jnp.concatenate([lat, ctx], axis=0)


def _ssm_kernel(ul_ref, uc_ref, m_ref, bm_ref, cm_ref, a_ref, d_ref, yl_ref, yc_ref,
                s_scr, h_scr, *, ctx_out):
    nl = ul_ref.shape[2]
    nc = uc_ref.shape[2]
    half = s_scr.shape[-1] // 2
    ul = ul_ref[0, 0]
    uc = uc_ref[0, 0]
    ulb = ul.astype(BF16)
    ucb = uc.astype(BF16)
    d = d_ref[0]
    yl = jnp.dot(ulb, m_ref[0], preferred_element_type=F32) + ul * d
    if ctx_out:
        yc = jnp.dot(ucb, m_ref[0], preferred_element_type=F32) + uc * d
    for dr in range(2):
        s_scr[dr, 0:nc, :] = jnp.dot(ucb, bm_ref[dr, 0], preferred_element_type=F32)
        s_scr[dr, nc:nc + nl, :] = jnp.dot(ulb, bm_ref[dr, 0], preferred_element_type=F32)
    a = a_ref[0]
    afr, afi, arr, ari = a[0:1], a[1:2], a[2:3], a[3:4]

    def make_body(base, n):
        def body(t, carry):
            fr, fi, rr, ri = carry
            rf = base + t
            rv = base + n - 1 - t
            h_scr[0, pl.ds(rf, 1), :] = jnp.concatenate([fr, fi], axis=-1)
            h_scr[1, pl.ds(rv, 1), :] = jnp.concatenate([rr, ri], axis=-1)
            sf = s_scr[0, pl.ds(rf, 1), :]
            sv = s_scr[1, pl.ds(rv, 1), :]
            nfr = afr * fr - afi * fi + sf[:, :half]
            nfi = afr * fi + afi * fr + sf[:, half:]
            nrr = arr * rr - ari * ri + sv[:, :half]
            nri = arr * ri + ari * rr + sv[:, half:]
            return nfr, nfi, nrr, nri
        return body

    z = jnp.zeros((1, half), F32)
    carry = lax.fori_loop(0, nc, make_body(0, nc), (z, z, z, z))
    lax.fori_loop(0, nl, make_body(nc, nl), carry)
    for dr in range(2):
        yl = yl + jnp.dot(h_scr[dr, nc:nc + nl, :].astype(BF16), cm_ref[dr, 0],
                          preferred_element_type=F32)
        if ctx_out:
            yc = yc + jnp.dot(h_scr[dr, 0:nc, :].astype(BF16), cm_ref[dr, 0],
                              preferred_element_type=F32)
    yl_ref[0, 0] = yl
    if ctx_out:
        yc_ref[0, 0] = yc
    else:
        yc_ref[0, 0] = jnp.zeros_like(uc)


def _ssm_matrices(lam_re, lam_im, log_step, b_re, b_im, c_re, c_im, d):
    tc = SSM_CHUNK
    g, p, hh = SSM_GROUPS, SSM_STATE, SSM_GROUP
    npair = g // 2
    step = jnp.exp(log_step)[..., None]
    den = lam_re * lam_re + lam_im * lam_im

    def power(k):
        er = jnp.exp(lam_re * step * k)
        return er * jnp.cos(lam_im * step * k), er * jnp.sin(lam_im * step * k)

    ar, ai = power(1.0)
    nr = ar - 1.0
    cr_ = (nr * lam_re + ai * lam_im) / den
    ci_ = (ai * lam_re - nr * lam_im) / den
    bbr = cr_[..., None] * b_re - ci_[..., None] * b_im
    bbi = cr_[..., None] * b_im + ci_[..., None] * b_re
    ks = jnp.arange(tc + 1, dtype=F32)
    pw = jax.vmap(power)(ks)
    pwr, pwi = pw
    cpr = c_re[None] * pwr[:, :, :, None, :] - c_im[None] * pwi[:, :, :, None, :]
    cpi = c_re[None] * pwi[:, :, :, None, :] + c_im[None] * pwr[:, :, :, None, :]
    hp = lax.Precision.HIGHEST
    kk = (jnp.einsum('kdgop,dgpi->kdgoi', cpr[:tc], bbr, precision=hp)
          - jnp.einsum('kdgop,dgpi->kdgoi', cpi[:tc], bbi, precision=hp))
    s_idx = jnp.arange(tc)[:, None]
    t_idx = jnp.arange(tc)[None, :]
    lag_f = t_idx - s_idx
    lag_r = s_idx - t_idx
    mf = jnp.where((lag_f >= 0)[:, :, None, None, None], kk[jnp.clip(lag_f, 0, tc - 1), 0], 0.0)
    mr = jnp.where((lag_r >= 0)[:, :, None, None, None], kk[jnp.clip(lag_r, 0, tc - 1), 1], 0.0)
    mm = mf + mr
    mm = jnp.transpose(mm, (2, 0, 4, 1, 3)).reshape(g, tc * hh, tc * hh)
    zero = jnp.zeros_like(mm[0::2])
    m2 = jnp.concatenate([jnp.concatenate([mm[0::2], zero], -1),
                          jnp.concatenate([zero, mm[1::2]], -1)], -2)
    def bmat(dr, exps):
        pr = pwr[exps, dr]
        pi = pwi[exps, dr]
        re = pr[..., None] * bbr[dr][None] - pi[..., None] * bbi[dr][None]
        im = pr[..., None] * bbi[dr][None] + pi[..., None] * bbr[dr][None]
        re = jnp.transpose(re, (1, 0, 3, 2)).reshape(g, tc * hh, p)
        im = jnp.transpose(im, (1, 0, 3, 2)).reshape(g, tc * hh, p)
        z = jnp.zeros_like(re[0::2])
        top = jnp.concatenate([re[0::2], z, im[0::2], z], -1)
        bot = jnp.concatenate([z, re[1::2], z, im[1::2]], -1)
        return jnp.concatenate([top, bot], -2)
    bm = jnp.stack([bmat(0, jnp.arange(tc - 1, -1, -1)), bmat(1, jnp.arange(tc))])
    def cmat(dr, exps):
        re = jnp.transpose(cpr[exps, dr], (1, 3, 0, 2)).reshape(g, p, tc * hh)
        im = jnp.transpose(cpi[exps, dr], (1, 3, 0, 2)).reshape(g, p, tc * hh)
        z = jnp.zeros_like(re[0::2])
        return jnp.concatenate([jnp.concatenate([re[0::2], z], -1),
                                jnp.concatenate([z, re[1::2]], -1),
                                jnp.concatenate([-im[0::2], z], -1),
                                jnp.concatenate([z, -im[1::2]], -1)], -2)
    cm = jnp.stack([cmat(0, jnp.arange(1, tc + 1)), cmat(1, jnp.arange(tc, 0, -1))])
    a16r = pwr[tc].reshape(2, npair, 2 * p)
    a16i = pwi[tc].reshape(2, npair, 2 * p)
    a16 = jnp.stack([a16r[0], a16i[0], a16r[1], a16i[1]], axis=1)
    dd = jnp.broadcast_to(d.reshape(npair, 2, 1, hh), (npair, 2, tc, hh)).reshape(npair, 1, 2 * tc * hh)
    return m2.astype(BF16), bm.astype(BF16), cm.astype(BF16), a16, dd


def _ssm(us, mats, nb, l_lat, l_ctx, ctx_out):
    m2, bm, cm, a16, dd = mats
    tc, hh = SSM_CHUNK, SSM_GROUP
    npair = SSM_GROUPS // 2
    w = 2 * tc * hh

    def fold(u, n):
        u = u.reshape(nb, n // tc, tc, npair, 2, hh)
        return jnp.transpose(u, (0, 3, 1, 4, 2, 5)).reshape(nb, npair, n // tc, w)

    def unfold(y, n):
        y = y.reshape(nb, npair, n // tc, 2, tc, hh)
        return jnp.transpose(y, (0, 2, 4, 1, 3, 5)).reshape(nb * n, SSM_W)

    ul = fold(us[:nb * l_lat], l_lat)
    uc = fold(us[nb * l_lat:], l_ctx)
    nl, nc = l_lat // tc, l_ctx // tc
    yl, yc = pl.pallas_call(
        functools.partial(_ssm_kernel, ctx_out=ctx_out),
        grid=(nb, npair),
        in_specs=[pl.BlockSpec((1, 1, nl, w), lambda b, q: (b, q, 0, 0)),
                  pl.BlockSpec((1, 1, nc, w), lambda b, q: (b, q, 0, 0)),
                  pl.BlockSpec((1, w, w), lambda b, q: (q, 0, 0)),
                  pl.BlockSpec((2, 1, w, 4 * SSM_STATE), lambda b, q: (0, q, 0, 0)),
                  pl.BlockSpec((2, 1, 4 * SSM_STATE, w), lambda b, q: (0, q, 0, 0)),
                  pl.BlockSpec((1, 4, 2 * SSM_STATE), lambda b, q: (q, 0, 0)),
                  pl.BlockSpec((1, 1, w), lambda b, q: (q, 0, 0))],
        out_specs=[pl.BlockSpec((1, 1, nl, w), lambda b, q: (b, q, 0, 0)),
                   pl.BlockSpec((1, 1, nc, w), lambda b, q: (b, q, 0, 0))],
        out_shape=[jax.ShapeDtypeStruct((nb, npair, nl, w), F32),
                   jax.ShapeDtypeStruct((nb, npair, nc, w), F32)],
        scratch_shapes=[pltpu.VMEM((2, nc + nl, 4 * SSM_STATE), F32),
                        pltpu.VMEM((2, nc + nl, 4 * SSM_STATE), F32)],
        compiler_params=_cparams(("arbitrary", "arbitrary")),
        name="s5_scan",
    )(ul, uc, m2, bm, cm, a16, dd)
    y = unfold(yl, l_lat)
    if ctx_out:
        y = jnp.concatenate([y, unfold(yc, l_ctx)], axis=0)
    return y


def _pool_kernel(prev_ref, cur_ref, next_ref, w_ref, scale_ref, o_ref, ext,
                 *, n_lat_tiles, tiles_per_lat, l_lat, l_ctx):
    r = cur_ref.shape[0]
    hl = POOL_HALO
    i = pl.program_id(0)
    is_lat = i < n_lat_tiles
    pos = jnp.where(is_lat, i % tiles_per_lat, 0)
    n_tiles = jnp.where(is_lat, tiles_per_lat, l_ctx // r)
    seq_len = jnp.where(is_lat, l_lat, l_ctx)
    zeros = jnp.zeros((hl, POOL_W), F32)
    ext[0:hl, :] = jnp.where(pos > 0, prev_ref[...], zeros)
    ext[hl:hl + r, :] = cur_ref[...]
    ext[hl + r:hl + r + hl, :] = jnp.where(pos < n_tiles - 1, next_ref[...], zeros)
    u = cur_ref[...]
    t = pos * r + lax.broadcasted_iota(jnp.int32, (r, 1), 0)
    lane = lax.broadcasted_iota(jnp.int32, (r, POOL_W), 1)

    def win(k):
        return ext[hl + k:hl + k + r, :]

    acc = win(-1) + u
    mean = jnp.zeros((r, POOL_W), F32)
    lo_k, hi_k = -1, 0
    for gi, wn in enumerate(POOL_WINDOWS):
        hw = wn // 2
        while lo_k > -hw:
            lo_k -= 1
            acc = acc + win(lo_k)
        while hi_k < hw - 1:
            hi_k += 1
            acc = acc + win(hi_k)
        cnt = (jnp.minimum(t + hw, seq_len) - jnp.maximum(t - hw, 0)).astype(F32)
        sel = (lane >= gi * POOL_GROUP) & (lane < (gi + 1) * POOL_GROUP)
        mean = jnp.where(sel, acc / cnt, mean)
    dlt = (mean - u).astype(BF16)
    o_ref[...] = jnp.dot(dlt, w_ref[...], preferred_element_type=F32) * scale_ref[...]


def _pool(up, w_blk, scale, nb, l_lat, l_ctx, n_rows):
    r = ROW_TILE
    hl = POOL_HALO
    n_tiles = n_rows // r
    last8 = up.shape[0] // hl - 1
    kern = functools.partial(_pool_kernel, n_lat_tiles=nb * l_lat // r, tiles_per_lat=l_lat // r,
                             l_lat=l_lat, l_ctx=l_ctx)
    return pl.pallas_call(
        kern,
        grid=(n_tiles,),
        in_specs=[pl.BlockSpec((hl, POOL_W), lambda i: (jnp.maximum(i * (r // hl) - 1, 0), 0)),
                  pl.BlockSpec((r, POOL_W), lambda i: (i, 0)),
                  pl.BlockSpec((hl, POOL_W), lambda i: (jnp.minimum((i + 1) * (r // hl), last8), 0)),
                  pl.BlockSpec((POOL_W, POOL_W), lambda i: (0, 0)),
                  pl.BlockSpec((1, POOL_W), lambda i: (0, 0))],
        out_specs=pl.BlockSpec((r, POOL_W), lambda i: (i, 0)),
        out_shape=jax.ShapeDtypeStruct((n_rows, POOL_W), F32),
        scratch_shapes=[pltpu.VMEM((r + 2 * hl, POOL_W), F32)],
        compiler_params=_cparams(("arbitrary",)),
        name="pool_mix",
    )(up, up, up, w_blk, scale)


def _mixout_kernel(x_ref, attn_ref, y_ref, pool_ref, mod_ref, wglu_ref, wout_ref, o_ref):
    yg = _gelu(y_ref[...])
    z = yg * jax.nn.sigmoid(jnp.dot(yg.astype(BF16), wglu_ref[...], preferred_element_type=F32))
    r = jnp.dot(attn_ref[...], wout_ref[0:ATTN_W, :], preferred_element_type=F32)
    r = r + jnp.dot(z.astype(BF16), wout_ref[ATTN_W:ATTN_W + SSM_W, :], preferred_element_type=F32)
    r = r + jnp.dot(pool_ref[...].astype(BF16), wout_ref[ATTN_W + SSM_W:, :],
                    preferred_element_type=F32)
    o_ref[...] = x_ref[...] + mod_ref[0, 2:3, :] * r


def _mixout(x, attn, y, pool, mods, wglu, wout, tiles_per_batch, nb, n_rows):
    d = x.shape[1]
    tm = ROW_TILE
    seg = _seg_map(tiles_per_batch, nb)
    row = lambda i: (i, 0)
    return pl.pallas_call(
        _mixout_kernel,
        grid=(n_rows // tm,),
        in_specs=[pl.BlockSpec((tm, d), row),
                  pl.BlockSpec((tm, ATTN_W), row),
                  pl.BlockSpec((tm, SSM_W), row),
                  pl.BlockSpec((tm, POOL_W), row),
                  pl.BlockSpec((1, N_MOD, d), lambda i: (seg(i), 0, 0)),
                  pl.BlockSpec((SSM_W, SSM_W), lambda i: (0, 0)),
                  pl.BlockSpec((d, d), lambda i: (0, 0))],
        out_specs=pl.BlockSpec((tm, d), row),
        out_shape=jax.ShapeDtypeStruct((n_rows, d), F32),
        compiler_params=_cparams(("arbitrary",)),
        name="mix_out",
    )(x, attn, y, pool, mods, wglu, wout)


def _top_sorted(s, k):
    rows = []
    for _ in range(k):
        m = jnp.max(s, axis=0, keepdims=True)
        rows.append(m)
        s = jnp.where(s == m, NEG_BIG, s)
    return rows


def _peer_score_kernel(x_ref, g_ref, mod_ref, wq_ref, keys_ref,
                       h_ref, beta_ref, s2_ref, e1_ref, e2_ref, q_scr):
    tm = x_ref.shape[0]
    h = _norm_mod(x_ref[...], g_ref[...], mod_ref[0, 3:4, :], mod_ref[0, 4:5, :]).astype(BF16)
    h_ref[...] = h
    q = jnp.dot(h, wq_ref[...], preferred_element_type=F32)
    for hc in range(2 * PEER_HEADS):
        q_scr[hc] = q[:, hc * PEER_KDIM:(hc + 1) * PEER_KDIM].astype(BF16)
    row8 = lax.broadcasted_iota(jnp.int32, (8, tm), 0)

    def head(hd, _):
        nt = (((1,), (1,)), ((), ()))
        s1 = lax.dot_general(keys_ref[hd, 0], q_scr[2 * hd], nt,
                             preferred_element_type=F32)
        s2 = lax.dot_general(keys_ref[hd, 1], q_scr[2 * hd + 1], nt,
                             preferred_element_type=F32)
        a = _top_sorted(s1, PEER_TOPK)
        b = _top_sorted(s2, PEER_TOPK)
        acat = jnp.concatenate(a, axis=0)
        bcat = jnp.concatenate(b, axis=0)
        pieces = [a[0] + bcat]
        for i in range(2, 9):
            piece = a[i - 1] + bcat[0:8]
            n_valid = PEER_TOPK // i
            pieces.append(piece if n_valid >= 8 else jnp.where(row8 < n_valid, piece, NEG_BIG))
        pieces.append(acat[8:16] + b[0])
        cand = jnp.concatenate(pieces, axis=0)
        work = cand
        cum = jnp.zeros((1, tm), F32)
        tau = jnp.full((1, tm), NEG_BIG, F32)
        for _ in range(PEER_TOPK):
            m = jnp.max(work, axis=0, keepdims=True)
            eq = work == m
            new = cum + jnp.sum(eq.astype(F32), axis=0, keepdims=True)
            tau = jnp.where((cum < PEER_TOPK) & (new >= PEER_TOPK), m, tau)
            work = jnp.where(eq, NEG_BIG, work)
            cum = new
        top = a[0] + b[0]
        z = jnp.sum(jnp.where(cand >= tau, jnp.exp(cand - top), 0.0), axis=0, keepdims=True)
        big = -NEG_BIG
        b_rows = [bcat] + [bcat[0:8]] * 7
        beta_r = [jnp.min(jnp.where(pc >= tau, br, big), axis=0, keepdims=True)
                  for pc, br in zip(pieces[:8], b_rows)]
        tail = jnp.where(pieces[8] >= tau, b[0], big)
        beta_r += [tail[r:r + 1] for r in range(8)]
        beta = jnp.full_like(s1, big)
        for r in range(PEER_TOPK):
            beta = jnp.where(s1 == a[r], beta_r[r], beta)
        beta_ref[hd] = beta
        s2_ref[hd] = s2
        e1_ref[hd] = jnp.exp(s1 - a[0])
        e2_ref[hd] = jnp.exp(s2 - b[0]) * (1.0 / z)
        return 0

    lax.fori_loop(0, PEER_HEADS, head, 0, unroll=4)


def _peer_scores(x, g, mods, wq, keys, tiles_per_batch, nb, n_rows):
    d = x.shape[1]
    tm = ROW_TILE
    seg = _seg_map(tiles_per_batch, nb)
    nq = wq.shape[1]
    col = lambda i: (0, 0, i)
    big = jax.ShapeDtypeStruct((PEER_HEADS, PEER_NKEYS, n_rows), F32)
    return pl.pallas_call(
        _peer_score_kernel,
        grid=(n_rows // tm,),
        in_specs=[pl.BlockSpec((tm, d), lambda i: (i, 0)),
                  pl.BlockSpec((1, d), lambda i: (0, 0)),
                  pl.BlockSpec((1, N_MOD, d), lambda i: (seg(i), 0, 0)),
                  pl.BlockSpec((d, nq), lambda i: (0, 0)),
                  pl.BlockSpec((PEER_HEADS, 2, PEER_NKEYS, PEER_KDIM), lambda i: (0, 0, 0, 0))],
        out_specs=[pl.BlockSpec((tm, d), lambda i: (i, 0)),
                   pl.BlockSpec((PEER_HEADS, PEER_NKEYS, tm), col),
                   pl.BlockSpec((PEER_HEADS, PEER_NKEYS, tm), col),
                   pl.BlockSpec((PEER_HEADS, PEER_NKEYS, tm), col),
                   pl.BlockSpec((PEER_HEADS, PEER_NKEYS, tm), col)],
        out_shape=[jax.ShapeDtypeStruct((n_rows, d), BF16), big, big, big, big],
        scratch_shapes=[pltpu.VMEM((2 * PEER_HEADS, tm, PEER_KDIM), BF16)],
        compiler_params=_cparams(("arbitrary",)),
        name="peer_scores",
    )(x, g, mods, wq, keys)


def _transpose_cast_kernel(x_ref, o_ref):
    o_ref[...] = x_ref[...].T.astype(o_ref.dtype)


def _transpose_cast(x, dtype, rows_per_step=512):
    n, d = x.shape
    return pl.pallas_call(
        _transpose_cast_kernel,
        grid=(n // rows_per_step,),
        in_specs=[pl.BlockSpec((rows_per_step, d), lambda i: (i, 0))],
        out_specs=pl.BlockSpec((d, rows_per_step), lambda i: (0, i)),
        out_shape=jax.ShapeDtypeStruct((d, n), dtype),
        compiler_params=_cparams(("arbitrary",)),
        name="transpose_cast",
    )(x)


PEER_CHUNK_KEYS = 8
PEER_TOK_TILE = 512
PEER_SUB = 256


def _peer_dense_kernel(h_ref, x_ref, mod_ref, u_ref, vt_ref, beta_ref, s2_ref, e1_ref, e2_ref,
                       fg_ref, o_ref, acc_ref, a0, a1, w0, w1, row_scr,
                       *, n_chunks, final_norm):
    n_i = PEER_CHUNK_KEYS
    tm = h_ref.shape[0]
    g = pl.program_id(0)
    nt_dims = (((1,), (1,)), ((), ()))
    n_k = u_ref.shape[1] // PEER_SUB
    n_lb = tm // LANE
    assert n_k == n_lb == vt_ref.shape[1] // PEER_SUB and n_i * PEER_NKEYS == u_ref.shape[0]
    n_mt = u_ref.shape[0] // PEER_SUB

    @pl.when(g == 0)
    def _():
        acc_ref[...] = jnp.zeros_like(acc_ref)
        a1[...] = jnp.zeros_like(a1)
        w0[...] = jnp.zeros_like(w0)

    for hd in range(PEER_HEADS):
        betab = beta_ref[hd]
        e1b = e1_ref[hd]
        for ii in range(n_i):
            row_scr[hd, ii, 0, :, 0:tm] = jnp.broadcast_to(betab[ii:ii + 1], (8, tm))
            row_scr[hd, ii, 1, :, 0:tm] = jnp.broadcast_to(e1b[ii:ii + 1], (8, tm))

    def run(a_wr, a_rd, w_wr, w_rd):
        n_trips = tm // PEER_SUB
        slabs_per_trip = PEER_NKEYS // 16 // n_trips
        for lb in range(n_lb):
            th, lo = divmod(lb * LANE, PEER_SUB)
            ls = slice(lb * LANE, (lb + 1) * LANE)
            lh = slice(lo, lo + LANE)
            ms = slice(lb * PEER_SUB, (lb + 1) * PEER_SUB)

            def trip(tr, _):
                t0 = pl.multiple_of(tr * PEER_SUB, PEER_SUB)
                part_a = lax.dot_general(u_ref[ms, :], h_ref[pl.ds(t0, PEER_SUB), :], nt_dims,
                                         preferred_element_type=F32)
                part_v = jnp.dot(vt_ref[ms, :], w_rd[tr], preferred_element_type=F32)
                for sl in range(slabs_per_trip):
                    j0 = pl.multiple_of((tr * slabs_per_trip + sl) * 16, 16)
                    gs = [[jnp.zeros((8, LANE), F32), jnp.zeros((8, LANE), F32)] for _ in range(n_i)]
                    for hd in range(PEER_HEADS):
                        s2v = s2_ref[hd, pl.ds(j0, 16), ls]
                        e2v = e2_ref[hd, pl.ds(j0, 16), ls]
                        for ii in range(n_i):
                            bs = row_scr[hd, ii, 0, :, ls]
                            be = row_scr[hd, ii, 1, :, ls]
                            for hf in range(2):
                                hs = slice(hf * 8, (hf + 1) * 8)
                                gs[ii][hf] = gs[ii][hf] + jnp.where(s2v[hs] >= bs, e2v[hs], 0.0) * be
                    for ii in range(n_i):
                        r0 = pl.multiple_of(ii * PEER_NKEYS + j0, 16)
                        g16 = jnp.concatenate(gs[ii], axis=0)
                        w_wr[th, pl.ds(r0, 16), lh] = (_gelu(a_rd[th, pl.ds(r0, 16), lh]) * g16).astype(BF16)
                a_wr[tr, ms, :] = part_a
                acc_ref[tr, ms, :] += part_v
                return 0

            lax.fori_loop(0, n_trips, trip, 0, unroll=True)

    @pl.when(g % 2 == 0)
    def _():
        run(a0, a1, w1, w0)

    @pl.when(g % 2 == 1)
    def _():
        run(a1, a0, w0, w1)

    @pl.when((g >= 2) & ((g - 2) % n_chunks == n_chunks - 1))
    def _():
        out = x_ref[...] + mod_ref[0, 5:6, :] * jnp.concatenate(
            [acc_ref[t].T for t in range(acc_ref.shape[0])], axis=0)
        if final_norm:
            ms = jnp.mean(out * out, axis=-1, keepdims=True)
            out = out * lax.rsqrt(ms + EPS) * fg_ref[...]
        o_ref[...] = out
        acc_ref[...] = jnp.zeros_like(acc_ref)


def _peer_dense(h, x, mods, u, vt, beta, s2, e1, e2, fg, tiles_per_batch, nb, n_rows, final_norm):
    d = x.shape[1]
    tm = PEER_TOK_TILE
    n_i = PEER_CHUNK_KEYS
    ne = n_i * PEER_NKEYS
    tiles_per_batch = tiles_per_batch * ROW_TILE // tm
    n_chunks = u.shape[0] // ne
    n_steps = (n_rows // tm) * n_chunks
    assert n_rows % tm == 0 and d == ne
    assert PEER_NKEYS // 16 == (tm // PEER_SUB) * (ne // PEER_SUB)

    def stage(lag):
        def split(g):
            n = jnp.clip(g - lag, 0, n_steps - 1)
            return n // n_chunks, n % n_chunks
        return split

    act, gate, val = stage(0), stage(1), stage(2)
    seg = lambda i: jnp.minimum(i // tiles_per_batch, nb)
    kern = functools.partial(_peer_dense_kernel, n_chunks=n_chunks, final_norm=final_norm)
    full = (PEER_HEADS, PEER_NKEYS, tm)
    rows = (PEER_HEADS, n_i, tm)
    return pl.pallas_call(
        kern,
        grid=(n_steps + 2,),
        in_specs=[pl.BlockSpec((tm, d), lambda g: (act(g)[0], 0)),
                  pl.BlockSpec((tm, d), lambda g: (val(g)[0], 0)),
                  pl.BlockSpec((1, N_MOD, d), lambda g: (seg(val(g)[0]), 0, 0)),
                  pl.BlockSpec((ne, d), lambda g: (act(g)[1], 0)),
                  pl.BlockSpec((d, ne), lambda g: (0, val(g)[1])),
                  pl.BlockSpec(rows, lambda g: (0, gate(g)[1], gate(g)[0])),
                  pl.BlockSpec(full, lambda g: (0, 0, gate(g)[0])),
                  pl.BlockSpec(rows, lambda g: (0, gate(g)[1], gate(g)[0])),
                  pl.BlockSpec(full, lambda g: (0, 0, gate(g)[0])),
                  pl.BlockSpec((1, d), lambda g: (0, 0))],
        out_specs=pl.BlockSpec((tm, d), lambda g: (val(g)[0], 0)),
        out_shape=jax.ShapeDtypeStruct((n_rows, d), F32),
        scratch_shapes=[pltpu.VMEM((tm // PEER_SUB, d, PEER_SUB), F32),
                        pltpu.VMEM((tm // PEER_SUB, ne, PEER_SUB), F32),
                        pltpu.VMEM((tm // PEER_SUB, ne, PEER_SUB), F32),
                        pltpu.VMEM((tm // PEER_SUB, ne, PEER_SUB), BF16),
                        pltpu.VMEM((tm // PEER_SUB, ne, PEER_SUB), BF16),
                        pltpu.VMEM((PEER_HEADS, n_i, 2, 8, tm + LANE), F32)],
        compiler_params=_cparams(("arbitrary",)),
        name="peer_experts",
    )(h, x, mods, u, vt, beta, s2, e1, e2, fg)


def _rope_tables(l_lat, nb, n_ctx_rows):
    rows = l_lat // GRID_W
    r = jnp.repeat(jnp.arange(rows), GRID_W)
    col = jnp.tile(jnp.arange(GRID_W), rows)
    pos = jnp.stack([r, col], axis=-1).astype(F32)
    nf = ATTN_HD // 4
    inv = 1.0 / (ROPE_BASE ** (jnp.arange(nf, dtype=F32) / nf))
    ang = pos[:, :, None] * inv
    cos, sin = jnp.cos(ang), jnp.sin(ang)
    c64 = jnp.concatenate([cos[:, 0], cos[:, 0], cos[:, 1], cos[:, 1]], axis=-1)
    s64 = jnp.concatenate([-sin[:, 0], sin[:, 0], -sin[:, 1], sin[:, 1]], axis=-1)
    c = jnp.tile(c64, (nb, LANE // ATTN_HD))
    s = jnp.tile(s64, (nb, LANE // ATTN_HD))
    c = jnp.concatenate([c, jnp.ones((n_ctx_rows, LANE), F32)], axis=0)
    s = jnp.concatenate([s, jnp.zeros((n_ctx_rows, LANE), F32)], axis=0)
    return c, s


def kernel(x, c, ctx, c_ctx, w_mod, b_mod, norm1_g, norm2_g, w_in, w_out, lam_q1, lam_k1, lam_q2, lam_k2, subln_g, ssm_lambda_re, ssm_lambda_im, ssm_log_step, ssm_b_re, ssm_b_im, ssm_c_re, ssm_c_im, ssm_d, ssm_w_glu, pool_w, pool_scale, peer_wq, peer_keys, peer_u, peer_v, final_g):
    nb, l_lat, d = x.shape
    l_ctx = ctx.shape[1]
    depth = w_mod.shape[0]
    n_lat = nb * l_lat
    n_all = n_lat + nb * l_ctx
    tiles_per_batch = l_lat // ROW_TILE
    assert l_lat % ROW_TILE == 0 and l_ctx == ROW_TILE and nb + 1 <= 8

    cs = jnp.concatenate([c, c_ctx[None], jnp.zeros((8 - nb - 1, d), F32)], axis=0)
    mod_all = _mod_vectors(cs, w_mod, b_mod)
    cos_t, sin_t = _rope_tables(l_lat, nb, nb * l_ctx)
    xs = jnp.concatenate([x.reshape(n_lat, d), ctx.reshape(nb * l_ctx, d)], axis=0)

    for l in range(depth):
        last = l == depth - 1
        n_rows = n_lat if last else n_all
        mods = mod_all[l, :nb + 1].reshape(nb + 1, N_MOD, d)
        q, k, v, us, up = _inproj(xs, norm1_g[l][None], mods, w_in[l].astype(BF16), cos_t, sin_t,
                                  tiles_per_batch, nb)
        lam_init = 0.8 - 0.6 * math.exp(-0.3 * l)
        lam = (jnp.exp(jnp.sum(lam_q1[l] * lam_k1[l])) - jnp.exp(jnp.sum(lam_q2[l] * lam_k2[l]))
               + lam_init).reshape(1).astype(F32)
        attn = _attention(q, k, v, lam, subln_g[l][None], nb, l_lat, l_ctx, not last,
                          1.0 - lam_init)
        mats = _ssm_matrices(ssm_lambda_re[l], ssm_lambda_im[l], ssm_log_step[l], ssm_b_re[l],
                             ssm_b_im[l], ssm_c_re[l], ssm_c_im[l], ssm_d[l])
        y = _ssm(us, mats, nb, l_lat, l_ctx, not last)
        w_blk = jax.scipy.linalg.block_diag(*[pool_w[l, gi] for gi in range(len(POOL_WINDOWS))])
        pool = _pool(up, w_blk.astype(BF16), pool_scale[l][None], nb, l_lat, l_ctx, n_rows)
        xs = _mixout(xs, attn, y, pool, mods, ssm_w_glu[l].astype(BF16), w_out[l].astype(BF16),
                     tiles_per_batch, nb, n_rows)
        h2, beta, s2, e1, e2 = _peer_scores(xs, norm2_g[l][None], mods, peer_wq[l].astype(BF16),
                                               peer_keys[l].astype(BF16), tiles_per_batch, nb, n_rows)
        xs = _peer_dense(h2, xs, mods, peer_u[l].astype(BF16), _transpose_cast(peer_v[l], BF16),
                         beta, s2, e1, e2, final_g[None], tiles_per_batch, nb, n_rows, last)
    return xs.reshape(nb, l_lat, d)
```

```python
import functools
import math

import numpy as np
import jax
import jax.numpy as jnp
from jax import lax
from jax.experimental import pallas as pl
from jax.experimental.pallas import tpu as pltpu

F32 = jnp.float32
BF16 = jnp.bfloat16

EPS = 1e-6
GRID_W = 64
N_MOD = 6
ATTN_HD = 64
ATTN_VD = 128
ATTN_VROWS = ATTN_VD + 16
LOG2E = float(np.log2(np.e))
ATTN_HEADS = 4
ATTN_W = 512
ROPE_BASE = 10000.0
SSM_W = 256
SSM_GROUP = 16
SSM_GROUPS = 16
SSM_STATE = 64
SSM_CHUNK = 16
POOL_W = 256
POOL_WINDOWS = (2, 4, 8, 16)
POOL_GROUP = 64
POOL_HALO = 8
SSM_OFF = 3 * ATTN_W
POOL_OFF = SSM_OFF + SSM_W
IN_W = POOL_OFF + POOL_W
PEER_HEADS = 8
PEER_NKEYS = 128
PEER_KDIM = 128
PEER_TOPK = 16
NEG_BIG = -3.0e38
SQRT_HALF = float(np.sqrt(0.5).astype(np.float32))

LANE = 128
ROW_TILE = 256
VMEM_LIMIT = 56 * 1024 * 1024


def _cparams(sem):
    return pltpu.CompilerParams(dimension_semantics=sem, vmem_limit_bytes=VMEM_LIMIT)


def _gelu(x):
    return 0.5 * x * (1.0 + lax.erf(x * SQRT_HALF))


def _norm_mod(x, g, shift, scale):
    ms = jnp.mean(x * x, axis=-1, keepdims=True)
    y = x * lax.rsqrt(ms + EPS) * g
    return y * (1.0 + scale) + shift


def _mod_kernel(s_ref, w_ref, b_ref, o_ref):
    s = s_ref[...]
    s = s * jax.nn.sigmoid(s)
    o_ref[0] = jnp.dot(s.astype(BF16), w_ref[0].astype(BF16), preferred_element_type=F32) + b_ref[0]


def _mod_vectors(cs, w_mod, b_mod):
    depth, d, n = w_mod.shape
    tn = 1536
    return pl.pallas_call(
        _mod_kernel,
        grid=(depth, n // tn),
        in_specs=[pl.BlockSpec((8, d), lambda l, j: (0, 0)),
                  pl.BlockSpec((1, d, tn), lambda l, j: (l, 0, j)),
                  pl.BlockSpec((1, 1, tn), lambda l, j: (l, 0, j))],
        out_specs=pl.BlockSpec((1, 8, tn), lambda l, j: (l, 0, j)),
        out_shape=jax.ShapeDtypeStruct((depth, 8, n), F32),
        compiler_params=_cparams(("arbitrary", "arbitrary")),
        name="mod_vectors",
    )(cs, w_mod, b_mod.reshape(depth, 1, n))


def _inproj_kernel(x_ref, g_ref, mod_ref, w_ref, cos_ref, sin_ref,
                   q_ref, k_ref, v_ref, us_ref, up_ref):
    tm = x_ref.shape[0]
    h = _norm_mod(x_ref[...], g_ref[...], mod_ref[0, 0:1, :], mod_ref[0, 1:2, :])
    p = jnp.dot(h.astype(BF16), w_ref[...], preferred_element_type=F32)
    c = cos_ref[...]
    s = sin_ref[...]
    lane = lax.broadcasted_iota(jnp.int32, (tm, LANE), 1)
    first = (lane % 32) < 16
    for off, ref, sc in ((0, q_ref, ATTN_HD ** -0.5 * LOG2E), (ATTN_W, k_ref, 1.0)):
        for blk in range(ATTN_W // LANE):
            xb = p[:, off + LANE * blk: off + LANE * (blk + 1)]
            partner = jnp.where(first, pltpu.roll(xb, LANE - 16, 1), pltpu.roll(xb, 16, 1))
            ref[:, LANE * blk: LANE * (blk + 1)] = ((xb * c + partner * s) * sc).astype(BF16)
    vt = p[:, 2 * ATTN_W:SSM_OFF].T.astype(BF16)
    ones_rows = (lax.broadcasted_iota(jnp.int32, (ATTN_VROWS - ATTN_VD, tm), 0) == 0).astype(BF16)
    for hd in range(ATTN_HEADS):
        v_ref[0, hd * ATTN_VROWS:hd * ATTN_VROWS + ATTN_VD, :] = vt[hd * ATTN_VD:(hd + 1) * ATTN_VD]
        v_ref[0, hd * ATTN_VROWS + ATTN_VD:(hd + 1) * ATTN_VROWS, :] = ones_rows
    us_ref[...] = p[:, SSM_OFF:POOL_OFF]
    up_ref[...] = p[:, POOL_OFF:IN_W]


def _seg_map(tiles_per_batch, nb):
    def seg(i):
        return jnp.minimum(i // tiles_per_batch, nb)
    return seg


def _inproj(x, g, mods, w, cos_t, sin_t, tiles_per_batch, nb):
    t, d = x.shape
    tm = ROW_TILE
    seg = _seg_map(tiles_per_batch, nb)
    row = lambda i: (i, 0)
    return pl.pallas_call(
        _inproj_kernel,
        grid=(t // tm,),
        in_specs=[pl.BlockSpec((tm, d), row),
                  pl.BlockSpec((1, d), lambda i: (0, 0)),
                  pl.BlockSpec((1, N_MOD, d), lambda i: (seg(i), 0, 0)),
                  pl.BlockSpec((d, IN_W), lambda i: (0, 0)),
                  pl.BlockSpec((tm, LANE), row),
                  pl.BlockSpec((tm, LANE), row)],
        out_specs=[pl.BlockSpec((tm, ATTN_W), row),
                   pl.BlockSpec((tm, ATTN_W), row),
                   pl.BlockSpec((1, ATTN_HEADS * ATTN_VROWS, tm), lambda i: (i, 0, 0)),
                   pl.BlockSpec((tm, SSM_W), row),
                   pl.BlockSpec((tm, POOL_W), row)],
        out_shape=[jax.ShapeDtypeStruct((t, ATTN_W), BF16),
                   jax.ShapeDtypeStruct((t, ATTN_W), BF16),
                   jax.ShapeDtypeStruct((t // tm, ATTN_HEADS * ATTN_VROWS, tm), BF16),
                   jax.ShapeDtypeStruct((t, SSM_W), F32),
                   jax.ShapeDtypeStruct((t, POOL_W), F32)],
        compiler_params=_cparams(("arbitrary",)),
        name="inproj",
    )(x, g, mods, w, cos_t, sin_t)


def _attn_stages(q, tq):
    lane = lax.broadcasted_iota(jnp.int32, (tq, LANE), 1)
    zero = jnp.zeros_like(q)
    qs = jnp.concatenate([jnp.where(lane < ATTN_HD, q, zero),
                          jnp.where(lane >= ATTN_HD, q, zero)], axis=0)

    def scores(kb):
        return lax.dot_general(kb, qs, (((1,), (1,)), ((), ())), preferred_element_type=F32)

    def softmax(m, s):
        m_new = jnp.maximum(m, jnp.max(s, axis=0, keepdims=True))
        return m_new, jnp.exp2(m - m_new), jnp.exp2(s - m_new).astype(BF16)

    def weighted(acc, alpha, pb, vts):
        acc = alpha * acc
        rows = pb.shape[0] // len(vts)
        for c, vt in enumerate(vts):
            acc = acc + jnp.dot(vt, pb[c * rows:(c + 1) * rows], preferred_element_type=F32)
        return acc

    return scores, softmax, weighted


def _attn_finish(lam, acc, g, tq, out_scale):
    o = acc[:ATTN_VD] / acc[ATTN_VD:ATTN_VD + 1]
    o = (o[:, :tq] - lam * o[:, tq:]).T
    ms = jnp.mean(o * o, axis=-1, keepdims=True)
    return (o * lax.rsqrt(ms + EPS) * g * out_scale).astype(BF16)


def _attn_ctx_kernel(lam_ref, q_ref, kc_ref, vc_ref, g_ref, o_ref, *, out_scale):
    tq = q_ref.shape[0]
    scores, softmax, weighted = _attn_stages(q_ref[...], tq)
    m = jnp.full((1, 2 * tq), NEG_BIG, F32)
    acc = jnp.zeros((ATTN_VROWS, 2 * tq), F32)
    m, alpha, pb = softmax(m, scores(kc_ref[...]))
    acc = weighted(acc, alpha, pb, [vc_ref[0]])
    o_ref[...] = _attn_finish(lam_ref[0], acc, g_ref[...], tq, out_scale)


def _attn_kernel(lam_ref, q_ref, kl_ref, kc_ref, vl_ref, vc_ref, g_ref, o_ref,
                 s_a, s_b, p_a, p_b, acc_ref, *, n_lat_k, tk, out_scale):
    tq = q_ref.shape[0]
    scores, softmax, weighted = _attn_stages(q_ref[...], tq)
    n_sub = tk // vc_ref.shape[-1]

    def stage_scores(t, s_buf):
        start = pl.multiple_of(t * tk, tk)
        s_buf[...] = scores(kl_ref[pl.ds(start, tk), :])

    def stage_softmax(m, s_buf, p_buf):
        m, alpha, pb = softmax(m, s_buf[...])
        p_buf[...] = pb
        return m, alpha

    def stage_values(alpha, p_buf, t):
        acc_ref[...] = weighted(acc_ref[...], alpha, p_buf[...],
                                [vl_ref[t * n_sub + c] for c in range(n_sub)])

    m = jnp.full((1, 2 * tq), NEG_BIG, F32)
    m, alpha, pb = softmax(m, scores(kc_ref[...]))
    acc_ref[...] = weighted(jnp.zeros((ATTN_VROWS, 2 * tq), F32), alpha, pb, [vc_ref[0]])

    if n_lat_k % 2 == 0:
        stage_scores(0, s_a)
        stage_scores(1, s_b)
        m, alpha = stage_softmax(m, s_a, p_a)

        def pair(i, carry):
            m, alpha = carry
            k = 2 * i
            stage_scores(k + 2, s_a)
            m, alpha_n = stage_softmax(m, s_b, p_b)
            stage_values(alpha, p_a, k)
            stage_scores(k + 3, s_b)
            m, alpha_nn = stage_softmax(m, s_a, p_a)
            stage_values(alpha_n, p_b, k + 1)
            return m, alpha_nn

        m, alpha = lax.fori_loop(0, n_lat_k // 2 - 1, pair, (m, alpha))
        m, alpha_n = stage_softmax(m, s_b, p_b)
        stage_values(alpha, p_a, n_lat_k - 2)
        stage_values(alpha_n, p_b, n_lat_k - 1)
    else:
        for t in range(n_lat_k):
            stage_scores(t, s_a)
            m, alpha = stage_softmax(m, s_a, p_a)
            stage_values(alpha, p_a, t)
    o_ref[...] = _attn_finish(lam_ref[0], acc_ref[...], g_ref[...], tq, out_scale)


def _attention(q, k, vt, lam, g, nb, l_lat, l_ctx, with_ctx_queries, out_scale):
    tq_ctx = ROW_TILE
    tq = 512 if l_lat % 512 == 0 else ROW_TILE
    tk = 512 if l_lat % 512 == 0 else l_lat
    vt_tile = vt.shape[-1]
    assert l_ctx == tq_ctx and l_lat % tq == 0 and vt_tile == l_ctx and tk % vt_tile == 0
    n_lat_q = l_lat // tq
    n_vt_lat = l_lat // vt_tile
    ctx_blk0 = nb * l_lat // l_ctx
    smem = pl.BlockSpec(memory_space=pltpu.SMEM)
    gain = pl.BlockSpec((1, LANE), lambda b, h, i: (0, 0))
    kc_spec = pl.BlockSpec((l_ctx, LANE), lambda b, h, i: (ctx_blk0 + b, h))
    vc_spec = pl.BlockSpec((1, ATTN_VROWS, vt_tile), lambda b, h, i: (nb * n_vt_lat + b, h, 0))
    sem = _cparams(("arbitrary", "arbitrary", "arbitrary"))
    lat = pl.pallas_call(
        functools.partial(_attn_kernel, n_lat_k=l_lat // tk, tk=tk, out_scale=out_scale),
        grid=(nb, ATTN_HEADS, n_lat_q),
        in_specs=[smem,
                  pl.BlockSpec((tq, LANE), lambda b, h, i: (b * n_lat_q + i, h)),
                  pl.BlockSpec((l_lat, LANE), lambda b, h, i: (b, h)),
                  kc_spec,
                  pl.BlockSpec((n_vt_lat, ATTN_VROWS, vt_tile), lambda b, h, i: (b, h, 0)),
                  vc_spec, gain],
        out_specs=pl.BlockSpec((tq, LANE), lambda b, h, i: (b * n_lat_q + i, h)),
        out_shape=jax.ShapeDtypeStruct((nb * l_lat, ATTN_W), BF16),
        scratch_shapes=[pltpu.VMEM((tk, 2 * tq), F32), pltpu.VMEM((tk, 2 * tq), F32),
                        pltpu.VMEM((tk, 2 * tq), BF16), pltpu.VMEM((tk, 2 * tq), BF16),
                        pltpu.VMEM((ATTN_VROWS, 2 * tq), F32)],
        compiler_params=sem,
        name="diff_attention",
    )(lam, q, k, k, vt, vt, g)
    if not with_ctx_queries:
        return lat
    ctx = pl.pallas_call(
        functools.partial(_attn_ctx_kernel, out_scale=out_scale),
        grid=(nb, ATTN_HEADS, 1),
        in_specs=[smem,
                  pl.BlockSpec((tq_ctx, LANE), lambda b, h, i: (ctx_blk0 + b, h)),
                  kc_spec, vc_spec, gain],
        out_specs=pl.BlockSpec((tq_ctx, LANE), lambda b, h, i: (b, h)),
        out_shape=jax.ShapeDtypeStruct((nb * l_ctx, ATTN_W), BF16),
        compiler_params=sem,
        name="diff_attention_ctx",
    )(lam, q, k, vt, g)
    return jnp.concatenate([lat, ctx], axis=0)


def _ssm_kernel(ul_ref, uc_ref, m_ref, bm_ref, cm_ref, a_ref, d_ref, yl_ref, yc_ref,
                s_scr, h_scr, *, ctx_out):
    tc, pw = SSM_CHUNK, 2 * SSM_GROUP
    nl = ul_ref.shape[0] // tc
    nc = uc_ref.shape[0] // tc
    w = m_ref.shape[-1]
    half = s_scr.shape[-1] // 2
    sub = pl.program_id(1) % (LANE // pw)
    nt_dims = (((1,), (1,)), ((), ()))
    ch = lax.broadcasted_iota(jnp.int32, (LANE, 1), 0) - sub * pw
    valid = (ch >= 0) & (ch < pw)
    tgt0 = (ch >> 4) * (tc * SSM_GROUP) + (ch & (SSM_GROUP - 1))
    col = lax.broadcasted_iota(jnp.int32, (1, w), 1)

    def perm(t):
        return jnp.where(valid & (col == tgt0 + t * SSM_GROUP), 1.0, 0.0).astype(BF16)

    def fold(ref, n):
        acc = jnp.zeros((n, w), F32)
        for t in range(tc):
            acc = acc + jnp.dot(ref[pl.ds(t, n, stride=tc), :].astype(BF16), perm(t),
                                preferred_element_type=F32)
        return acc.astype(BF16)

    def unfold_add(y, ref, n):
        hi = y.astype(BF16)
        r1 = y - hi.astype(F32)
        mid = r1.astype(BF16)
        lo = (r1 - mid.astype(F32)).astype(BF16)
        for t in range(tc):
            p = perm(t)
            o = sum(lax.dot_general(part, p, nt_dims, preferred_element_type=F32)
                    for part in (hi, mid, lo))
            ref[pl.ds(t, n, stride=tc), :] += o

    @pl.when(sub == 0)
    def _():
        yl_ref[...] = ul_ref[...] * d_ref[...]
        yc_ref[...] = uc_ref[...] * d_ref[...] if ctx_out else jnp.zeros_like(yc_ref)

    ulb = fold(ul_ref, nl)
    ucb = fold(uc_ref, nc)
    yl = jnp.dot(ulb, m_ref[0], preferred_element_type=F32)
    if ctx_out:
        yc = jnp.dot(ucb, m_ref[0], preferred_element_type=F32)
    for dr in range(2):
        s_scr[dr, 0:nc, :] = jnp.dot(ucb, bm_ref[dr, 0], preferred_element_type=F32)
        s_scr[dr, nc:nc + nl, :] = jnp.dot(ulb, bm_ref[dr, 0], preferred_element_type=F32)
    a = a_ref[0]
    afr, afi, arr, ari = a[0:1], a[1:2], a[2:3], a[3:4]

    def make_body(base, n):
        def body(t, carry):
            fr, fi, rr, ri = carry
            rf = base + t
            rv = base + n - 1 - t
            h_scr[0, pl.ds(rf, 1), :] = jnp.concatenate([fr, fi], axis=-1)
            h_scr[1, pl.ds(rv, 1), :] = jnp.concatenate([rr, ri], axis=-1)
            sf = s_scr[0, pl.ds(rf, 1), :]
            sv = s_scr[1, pl.ds(rv, 1), :]
            nfr = afr * fr - afi * fi + sf[:, :half]
            nfi = afr * fi + afi * fr + sf[:, half:]
            nrr = arr * rr - ari * ri + sv[:, :half]
            nri = arr * ri + ari * rr + sv[:, half:]
            return nfr, nfi, nrr, nri
        return body

    z = jnp.zeros((1, half), F32)
    carry = lax.fori_loop(0, nc, make_body(0, nc), (z, z, z, z))
    lax.fori_loop(0, nl, make_body(nc, nl), carry)
    for dr in range(2):
        yl = yl + jnp.dot(h_scr[dr, nc:nc + nl, :].astype(BF16), cm_ref[dr, 0],
                          preferred_element_type=F32)
        if ctx_out:
            yc = yc + jnp.dot(h_scr[dr, 0:nc, :].astype(BF16), cm_ref[dr, 0],
                              preferred_element_type=F32)
    unfold_add(yl, yl_ref, nl)
    if ctx_out:
        unfold_add(yc, yc_ref, nc)


def _ssm_matrices(lam_re, lam_im, log_step, b_re, b_im, c_re, c_im):
    tc = SSM_CHUNK
    g, p, hh = SSM_GROUPS, SSM_STATE, SSM_GROUP
    npair = g // 2
    step = jnp.exp(log_step)[..., None]
    den = lam_re * lam_re + lam_im * lam_im

    def power(k):
        er = jnp.exp(lam_re * step * k)
        return er * jnp.cos(lam_im * step * k), er * jnp.sin(lam_im * step * k)

    ar, ai = power(1.0)
    nr = ar - 1.0
    cr_ = (nr * lam_re + ai * lam_im) / den
    ci_ = (ai * lam_re - nr * lam_im) / den
    bbr = cr_[..., None] * b_re - ci_[..., None] * b_im
    bbi = cr_[..., None] * b_im + ci_[..., None] * b_re
    ks = jnp.arange(tc + 1, dtype=F32)
    pw = jax.vmap(power)(ks)
    pwr, pwi = pw
    cpr = c_re[None] * pwr[:, :, :, None, :] - c_im[None] * pwi[:, :, :, None, :]
    cpi = c_re[None] * pwi[:, :, :, None, :] + c_im[None] * pwr[:, :, :, None, :]
    hp = lax.Precision.HIGHEST
    kk = (jnp.einsum('kdgop,dgpi->kdgoi', cpr[:tc], bbr, precision=hp)
          - jnp.einsum('kdgop,dgpi->kdgoi', cpi[:tc], bbi, precision=hp))
    s_idx = jnp.arange(tc)[:, None]
    t_idx = jnp.arange(tc)[None, :]
    lag_f = t_idx - s_idx
    lag_r = s_idx - t_idx
    mf = jnp.where((lag_f >= 0)[:, :, None, None, None], kk[jnp.clip(lag_f, 0, tc - 1), 0], 0.0)
    mr = jnp.where((lag_r >= 0)[:, :, None, None, None], kk[jnp.clip(lag_r, 0, tc - 1), 1], 0.0)
    mm = mf + mr
    mm = jnp.transpose(mm, (2, 0, 4, 1, 3)).reshape(g, tc * hh, tc * hh)
    zero = jnp.zeros_like(mm[0::2])
    m2 = jnp.concatenate([jnp.concatenate([mm[0::2], zero], -1),
                          jnp.concatenate([zero, mm[1::2]], -1)], -2)
    def bmat(dr, exps):
        pr = pwr[exps, dr]
        pi = pwi[exps, dr]
        re = pr[..., None] * bbr[dr][None] - pi[..., None] * bbi[dr][None]
        im = pr[..., None] * bbi[dr][None] + pi[..., None] * bbr[dr][None]
        re = jnp.transpose(re, (1, 0, 3, 2)).reshape(g, tc * hh, p)
        im = jnp.transpose(im, (1, 0, 3, 2)).reshape(g, tc * hh, p)
        z = jnp.zeros_like(re[0::2])
        top = jnp.concatenate([re[0::2], z, im[0::2], z], -1)
        bot = jnp.concatenate([z, re[1::2], z, im[1::2]], -1)
        return jnp.concatenate([top, bot], -2)
    bm = jnp.stack([bmat(0, jnp.arange(tc - 1, -1, -1)), bmat(1, jnp.arange(tc))])
    def cmat(dr, exps):
        re = jnp.transpose(cpr[exps, dr], (1, 3, 0, 2)).reshape(g, p, tc * hh)
        im = jnp.transpose(cpi[exps, dr], (1, 3, 0, 2)).reshape(g, p, tc * hh)
        z = jnp.zeros_like(re[0::2])
        return jnp.concatenate([jnp.concatenate([re[0::2], z], -1),
                                jnp.concatenate([z, re[1::2]], -1),
                                jnp.concatenate([-im[0::2], z], -1),
                                jnp.concatenate([z, -im[1::2]], -1)], -2)
    cm = jnp.stack([cmat(0, jnp.arange(1, tc + 1)), cmat(1, jnp.arange(tc, 0, -1))])
    a16r = pwr[tc].reshape(2, npair, 2 * p)
    a16i = pwi[tc].reshape(2, npair, 2 * p)
    a16 = jnp.stack([a16r[0], a16i[0], a16r[1], a16i[1]], axis=1)
    return m2.astype(BF16), bm.astype(BF16), cm.astype(BF16), a16


def _ssm(us, mats, d, nb, l_lat, l_ctx, ctx_out):
    m2, bm, cm, a16 = mats
    tc, hh = SSM_CHUNK, SSM_GROUP
    npair = SSM_GROUPS // 2
    w = 2 * tc * hh
    per_col = LANE // (2 * hh)
    nl, nc = l_lat // tc, l_ctx // tc
    ctx_blk0 = nb * l_lat // l_ctx
    yl, yc = pl.pallas_call(
        functools.partial(_ssm_kernel, ctx_out=ctx_out),
        grid=(nb, npair),
        in_specs=[pl.BlockSpec((l_lat, LANE), lambda b, q: (b, q // per_col)),
                  pl.BlockSpec((l_ctx, LANE), lambda b, q: (ctx_blk0 + b, q // per_col)),
                  pl.BlockSpec((1, w, w), lambda b, q: (q, 0, 0)),
                  pl.BlockSpec((2, 1, w, 4 * SSM_STATE), lambda b, q: (0, q, 0, 0)),
                  pl.BlockSpec((2, 1, 4 * SSM_STATE, w), lambda b, q: (0, q, 0, 0)),
                  pl.BlockSpec((1, 4, 2 * SSM_STATE), lambda b, q: (q, 0, 0)),
                  pl.BlockSpec((1, LANE), lambda b, q: (0, q // per_col))],
        out_specs=[pl.BlockSpec((l_lat, LANE), lambda b, q: (b, q // per_col)),
                   pl.BlockSpec((l_ctx, LANE), lambda b, q: (b, q // per_col))],
        out_shape=[jax.ShapeDtypeStruct((nb * l_lat, SSM_W), F32),
                   jax.ShapeDtypeStruct((nb * l_ctx, SSM_W), F32)],
        scratch_shapes=[pltpu.VMEM((2, nc + nl, 4 * SSM_STATE), F32),
                        pltpu.VMEM((2, nc + nl, 4 * SSM_STATE), F32)],
        compiler_params=_cparams(("arbitrary", "arbitrary")),
        name="s5_scan",
    )(us, us, m2, bm, cm, a16, d)
    return jnp.concatenate([yl, yc], axis=0) if ctx_out else yl


def _pool_kernel(prev_ref, cur_ref, next_ref, w_ref, scale_ref, o_ref, ext,
                 *, n_lat_tiles, tiles_per_lat, l_lat, l_ctx):
    r = cur_ref.shape[0]
    hl = POOL_HALO
    i = pl.program_id(0)
    is_lat = i < n_lat_tiles
    pos = jnp.where(is_lat, i % tiles_per_lat, 0)
    n_tiles = jnp.where(is_lat, tiles_per_lat, l_ctx // r)
    seq_len = jnp.where(is_lat, l_lat, l_ctx)
    zeros = jnp.zeros((hl, POOL_W), F32)
    ext[0:hl, :] = jnp.where(pos > 0, prev_ref[...], zeros)
    ext[hl:hl + r, :] = cur_ref[...]
    ext[hl + r:hl + r + hl, :] = jnp.where(pos < n_tiles - 1, next_ref[...], zeros)
    u = cur_ref[...]
    t = pos * r + lax.broadcasted_iota(jnp.int32, (r, 1), 0)
    lane = lax.broadcasted_iota(jnp.int32, (r, POOL_W), 1)

    def win(k):
        return ext[hl + k:hl + k + r, :]

    acc = win(-1) + u
    mean = jnp.zeros((r, POOL_W), F32)
    lo_k, hi_k = -1, 0
    for gi, wn in enumerate(POOL_WINDOWS):
        hw = wn // 2
        while lo_k > -hw:
            lo_k -= 1
            acc = acc + win(lo_k)
        while hi_k < hw - 1:
            hi_k += 1
            acc = acc + win(hi_k)
        cnt = (jnp.minimum(t + hw, seq_len) - jnp.maximum(t - hw, 0)).astype(F32)
        sel = (lane >= gi * POOL_GROUP) & (lane < (gi + 1) * POOL_GROUP)
        mean = jnp.where(sel, acc / cnt, mean)
    dlt = (mean - u).astype(BF16)
    o_ref[...] = jnp.dot(dlt, w_ref[...], preferred_element_type=F32) * scale_ref[...]


def _pool(up, w_blk, scale, nb, l_lat, l_ctx, n_rows):
    r = ROW_TILE
    hl = POOL_HALO
    n_tiles = n_rows // r
    last8 = up.shape[0] // hl - 1
    kern = functools.partial(_pool_kernel, n_lat_tiles=nb * l_lat // r, tiles_per_lat=l_lat // r,
                             l_lat=l_lat, l_ctx=l_ctx)
    return pl.pallas_call(
        kern,
        grid=(n_tiles,),
        in_specs=[pl.BlockSpec((hl, POOL_W), lambda i: (jnp.maximum(i * (r // hl) - 1, 0), 0)),
                  pl.BlockSpec((r, POOL_W), lambda i: (i, 0)),
                  pl.BlockSpec((hl, POOL_W), lambda i: (jnp.minimum((i + 1) * (r // hl), last8), 0)),
                  pl.BlockSpec((POOL_W, POOL_W), lambda i: (0, 0)),
                  pl.BlockSpec((1, POOL_W), lambda i: (0, 0))],
        out_specs=pl.BlockSpec((r, POOL_W), lambda i: (i, 0)),
        out_shape=jax.ShapeDtypeStruct((n_rows, POOL_W), F32),
        scratch_shapes=[pltpu.VMEM((r + 2 * hl, POOL_W), F32)],
        compiler_params=_cparams(("arbitrary",)),
        name="pool_mix",
    )(up, up, up, w_blk, scale)


def _mixout_kernel(x_ref, attn_ref, y_ref, pool_ref, mod_ref, wglu_ref, wout_ref, o_ref):
    yg = _gelu(y_ref[...])
    z = yg * jax.nn.sigmoid(jnp.dot(yg.astype(BF16), wglu_ref[...], preferred_element_type=F32))
    r = jnp.dot(attn_ref[...], wout_ref[0:ATTN_W, :], preferred_element_type=F32)
    r = r + jnp.dot(z.astype(BF16), wout_ref[ATTN_W:ATTN_W + SSM_W, :], preferred_element_type=F32)
    r = r + jnp.dot(pool_ref[...].astype(BF16), wout_ref[ATTN_W + SSM_W:, :],
                    preferred_element_type=F32)
    o_ref[...] = x_ref[...] + mod_ref[0, 2:3, :] * r


def _mixout(x, attn, y, pool, mods, wglu, wout, tiles_per_batch, nb, n_rows):
    d = x.shape[1]
    tm = ROW_TILE
    seg = _seg_map(tiles_per_batch, nb)
    row = lambda i: (i, 0)
    return pl.pallas_call(
        _mixout_kernel,
        grid=(n_rows // tm,),
        in_specs=[pl.BlockSpec((tm, d), row),
                  pl.BlockSpec((tm, ATTN_W), row),
                  pl.BlockSpec((tm, SSM_W), row),
                  pl.BlockSpec((tm, POOL_W), row),
                  pl.BlockSpec((1, N_MOD, d), lambda i: (seg(i), 0, 0)),
                  pl.BlockSpec((SSM_W, SSM_W), lambda i: (0, 0)),
                  pl.BlockSpec((d, d), lambda i: (0, 0))],
        out_specs=pl.BlockSpec((tm, d), row),
        out_shape=jax.ShapeDtypeStruct((n_rows, d), F32),
        compiler_params=_cparams(("arbitrary",)),
        name="mix_out",
    )(x, attn, y, pool, mods, wglu, wout)


def _oddeven_merge(lo, hi, r):
    step = r * 2
    if step < hi - lo:
        yield from _oddeven_merge(lo, hi, step)
        yield from _oddeven_merge(lo + r, hi, step)
        yield from [(i, i + r) for i in range(lo + r, hi - r, step)]
    else:
        yield (lo, lo + r)


def _oddeven_sort(lo, hi):
    if hi - lo >= 1:
        mid = lo + (hi - lo) // 2
        yield from _oddeven_sort(lo, mid)
        yield from _oddeven_sort(mid + 1, hi)
        yield from _oddeven_merge(lo, hi, 1)


_SORT16 = tuple(_oddeven_sort(0, PEER_TOPK - 1))
_BITONIC16 = tuple((i, i + d) for d in (8, 4, 2, 1) for i in range(PEER_TOPK) if not i & d)


def _compare_exchange(rows, net):
    rows = list(rows)
    for i, j in net:
        hi, lo = jnp.maximum(rows[i], rows[j]), jnp.minimum(rows[i], rows[j])
        rows[i], rows[j] = hi, lo
    return rows


def _top_sorted(s, k):
    assert k == PEER_TOPK and s.shape[0] == 8 * PEER_TOPK
    rows = _compare_exchange([s[8 * v:8 * v + 8] for v in range(PEER_TOPK)], _SORT16)
    for shift in (4, 2, 1):
        other = [pltpu.roll(r, shift, 0) for r in rows]
        rows = _compare_exchange([jnp.maximum(rows[i], other[PEER_TOPK - 1 - i])
                                  for i in range(PEER_TOPK)], _BITONIC16)
    return [r[0:1] for r in rows]


def _peer_score_kernel(x_ref, g_ref, mod_ref, wq_ref, keys_ref,
                       h_ref, beta_ref, s2_ref, e1_ref, e2_ref, q_scr):
    tm = x_ref.shape[0]
    h = _norm_mod(x_ref[...], g_ref[...], mod_ref[0, 3:4, :], mod_ref[0, 4:5, :]).astype(BF16)
    h_ref[...] = h
    q = jnp.dot(h, wq_ref[...], preferred_element_type=F32)
    for hc in range(2 * PEER_HEADS):
        q_scr[hc] = q[:, hc * PEER_KDIM:(hc + 1) * PEER_KDIM].astype(BF16)
    row8 = lax.broadcasted_iota(jnp.int32, (8, tm), 0)

    def head(hd, _):
        nt = (((1,), (1,)), ((), ()))
        s1 = lax.dot_general(keys_ref[hd, 0], q_scr[2 * hd], nt,
                             preferred_element_type=F32)
        s2 = lax.dot_general(keys_ref[hd, 1], q_scr[2 * hd + 1], nt,
                             preferred_element_type=F32)
        a = _top_sorted(s1, PEER_TOPK)
        b = _top_sorted(s2, PEER_TOPK)
        acat = jnp.concatenate(a, axis=0)
        bcat = jnp.concatenate(b, axis=0)
        pieces = [a[0] + bcat]
        for i in range(2, 9):
            piece = a[i - 1] + bcat[0:8]
            n_valid = PEER_TOPK // i
            pieces.append(piece if n_valid >= 8 else jnp.where(row8 < n_valid, piece, NEG_BIG))
        pieces.append(acat[8:16] + b[0])
        cand = jnp.concatenate(pieces, axis=0)
        work = cand
        cum = jnp.zeros((1, tm), F32)
        tau = jnp.full((1, tm), NEG_BIG, F32)
        for _ in range(PEER_TOPK):
            m = jnp.max(work, axis=0, keepdims=True)
            eq = work == m
            new = cum + jnp.sum(eq.astype(F32), axis=0, keepdims=True)
            tau = jnp.where((cum < PEER_TOPK) & (new >= PEER_TOPK), m, tau)
            work = jnp.where(eq, NEG_BIG, work)
            cum = new
        top = a[0] + b[0]
        z = jnp.sum(jnp.where(cand >= tau, jnp.exp(cand - top), 0.0), axis=0, keepdims=True)
        big = -NEG_BIG
        b_rows = [bcat] + [bcat[0:8]] * 7
        beta_r = [jnp.min(jnp.where(pc >= tau, br, big), axis=0, keepdims=True)
                  for pc, br in zip(pieces[:8], b_rows)]
        tail = jnp.where(pieces[8] >= tau, b[0], big)
        beta_r += [tail[r:r + 1] for r in range(8)]
        beta = jnp.full_like(s1, big)
        for r in range(PEER_TOPK):
            beta = jnp.where(s1 == a[r], beta_r[r], beta)
        beta_ref[hd] = beta
        s2_ref[hd] = s2
        e1_ref[hd] = jnp.exp(s1 - a[0])
        e2_ref[hd] = jnp.exp(s2 - b[0]) * (1.0 / z)
        return 0

    lax.fori_loop(0, PEER_HEADS, head, 0, unroll=4)


def _peer_scores(x, g, mods, wq, keys, tiles_per_batch, nb, n_rows):
    d = x.shape[1]
    tm = ROW_TILE
    seg = _seg_map(tiles_per_batch, nb)
    nq = wq.shape[1]
    col = lambda i: (0, 0, i)
    big = jax.ShapeDtypeStruct((PEER_HEADS, PEER_NKEYS, n_rows), F32)
    return pl.pallas_call(
        _peer_score_kernel,
        grid=(n_rows // tm,),
        in_specs=[pl.BlockSpec((tm, d), lambda i: (i, 0)),
                  pl.BlockSpec((1, d), lambda i: (0, 0)),
                  pl.BlockSpec((1, N_MOD, d), lambda i: (seg(i), 0, 0)),
                  pl.BlockSpec((d, nq), lambda i: (0, 0)),
                  pl.BlockSpec((PEER_HEADS, 2, PEER_NKEYS, PEER_KDIM), lambda i: (0, 0, 0, 0))],
        out_specs=[pl.BlockSpec((tm, d), lambda i: (i, 0)),
                   pl.BlockSpec((PEER_HEADS, PEER_NKEYS, tm), col),
                   pl.BlockSpec((PEER_HEADS, PEER_NKEYS, tm), col),
                   pl.BlockSpec((PEER_HEADS, PEER_NKEYS, tm), col),
                   pl.BlockSpec((PEER_HEADS, PEER_NKEYS, tm), col)],
        out_shape=[jax.ShapeDtypeStruct((n_rows, d), BF16), big, big, big, big],
        scratch_shapes=[pltpu.VMEM((2 * PEER_HEADS, tm, PEER_KDIM), BF16)],
        compiler_params=_cparams(("arbitrary",)),
        name="peer_scores",
    )(x, g, mods, wq, keys)


def _transpose_cast_kernel(x_ref, o_ref):
    o_ref[...] = x_ref[...].T.astype(o_ref.dtype)


def _transpose_cast(x, dtype, rows_per_step=512):
    n, d = x.shape
    return pl.pallas_call(
        _transpose_cast_kernel,
        grid=(n // rows_per_step,),
        in_specs=[pl.BlockSpec((rows_per_step, d), lambda i: (i, 0))],
        out_specs=pl.BlockSpec((d, rows_per_step), lambda i: (0, i)),
        out_shape=jax.ShapeDtypeStruct((d, n), dtype),
        compiler_params=_cparams(("arbitrary",)),
        name="transpose_cast",
    )(x)


PEER_CHUNK_KEYS = 8
PEER_TOK_TILE = 512
PEER_SUB = 256


def _peer_dense_kernel(h_ref, x_ref, mod_ref, u_ref, vt_ref, beta_ref, s2_ref, e1_ref, e2_ref,
                       fg_ref, o_ref, acc_ref, a0, a1, w0, w1, row_scr,
                       *, n_chunks, final_norm):
    n_i = PEER_CHUNK_KEYS
    tm = h_ref.shape[0]
    g = pl.program_id(0)
    nt_dims = (((1,), (1,)), ((), ()))
    n_k = u_ref.shape[1] // PEER_SUB
    n_lb = tm // LANE
    assert n_k == n_lb == vt_ref.shape[1] // PEER_SUB and n_i * PEER_NKEYS == u_ref.shape[0]
    n_mt = u_ref.shape[0] // PEER_SUB

    @pl.when(g == 0)
    def _():
        acc_ref[...] = jnp.zeros_like(acc_ref)
        a1[...] = jnp.zeros_like(a1)
        w0[...] = jnp.zeros_like(w0)

    for hd in range(PEER_HEADS):
        betab = beta_ref[hd]
        e1b = e1_ref[hd]
        for ii in range(n_i):
            row_scr[hd, ii, 0, :, 0:tm] = jnp.broadcast_to(betab[ii:ii + 1], (8, tm))
            row_scr[hd, ii, 1, :, 0:tm] = jnp.broadcast_to(e1b[ii:ii + 1], (8, tm))

    def run(a_wr, a_rd, w_wr, w_rd):
        n_trips = tm // PEER_SUB
        slabs_per_trip = PEER_NKEYS // 16 // n_trips
        for lb in range(n_lb):
            th, lo = divmod(lb * LANE, PEER_SUB)
            ls = slice(lb * LANE, (lb + 1) * LANE)
            lh = slice(lo, lo + LANE)
            ms = slice(lb * PEER_SUB, (lb + 1) * PEER_SUB)

            def trip(tr, _):
                t0 = pl.multiple_of(tr * PEER_SUB, PEER_SUB)
                part_a = lax.dot_general(u_ref[ms, :], h_ref[pl.ds(t0, PEER_SUB), :], nt_dims,
                                         preferred_element_type=F32)
                part_v = jnp.dot(vt_ref[ms, :], w_rd[tr], preferred_element_type=F32)
                for sl in range(slabs_per_trip):
                    j0 = pl.multiple_of((tr * slabs_per_trip + sl) * 16, 16)
                    gs = [[jnp.zeros((8, LANE), F32), jnp.zeros((8, LANE), F32)] for _ in range(n_i)]
                    for hd in range(PEER_HEADS):
                        s2v = s2_ref[hd, pl.ds(j0, 16), ls]
                        e2v = e2_ref[hd, pl.ds(j0, 16), ls]
                        for ii in range(n_i):
                            bs = row_scr[hd, ii, 0, :, ls]
                            be = row_scr[hd, ii, 1, :, ls]
                            for hf in range(2):
                                hs = slice(hf * 8, (hf + 1) * 8)
                                gs[ii][hf] = gs[ii][hf] + jnp.where(s2v[hs] >= bs, e2v[hs], 0.0) * be
                    for ii in range(n_i):
                        r0 = pl.multiple_of(ii * PEER_NKEYS + j0, 16)
                        g16 = jnp.concatenate(gs[ii], axis=0)
                        w_wr[th, pl.ds(r0, 16), lh] = (_gelu(a_rd[th, pl.ds(r0, 16), lh]) * g16).astype(BF16)
                a_wr[tr, ms, :] = part_a
                acc_ref[tr, ms, :] += part_v
                return 0

            lax.fori_loop(0, n_trips, trip, 0, unroll=True)

    @pl.when(g % 2 == 0)
    def _():
        run(a0, a1, w1, w0)

    @pl.when(g % 2 == 1)
    def _():
        run(a1, a0, w0, w1)

    @pl.when((g >= 2) & ((g - 2) % n_chunks == n_chunks - 1))
    def _():
        out = x_ref[...] + mod_ref[0, 5:6, :] * jnp.concatenate(
            [acc_ref[t].T for t in range(acc_ref.shape[0])], axis=0)
        if final_norm:
            ms = jnp.mean(out * out, axis=-1, keepdims=True)
            out = out * lax.rsqrt(ms + EPS) * fg_ref[...]
        o_ref[...] = out
        acc_ref[...] = jnp.zeros_like(acc_ref)


def _peer_dense(h, x, mods, u, vt, beta, s2, e1, e2, fg, tiles_per_batch, nb, n_rows, final_norm):
    d = x.shape[1]
    tm = PEER_TOK_TILE
    n_i = PEER_CHUNK_KEYS
    ne = n_i * PEER_NKEYS
    tiles_per_batch = tiles_per_batch * ROW_TILE // tm
    n_chunks = u.shape[0] // ne
    n_steps = (n_rows // tm) * n_chunks
    assert n_rows % tm == 0 and d == ne
    assert PEER_NKEYS // 16 == (tm // PEER_SUB) * (ne // PEER_SUB)

    def stage(lag):
        def split(g):
            n = jnp.clip(g - lag, 0, n_steps - 1)
            return n // n_chunks, n % n_chunks
        return split

    act, gate, val = stage(0), stage(1), stage(2)
    seg = lambda i: jnp.minimum(i // tiles_per_batch, nb)
    kern = functools.partial(_peer_dense_kernel, n_chunks=n_chunks, final_norm=final_norm)
    full = (PEER_HEADS, PEER_NKEYS, tm)
    rows = (PEER_HEADS, n_i, tm)
    return pl.pallas_call(
        kern,
        grid=(n_steps + 2,),
        in_specs=[pl.BlockSpec((tm, d), lambda g: (act(g)[0], 0)),
                  pl.BlockSpec((tm, d), lambda g: (val(g)[0], 0)),
                  pl.BlockSpec((1, N_MOD, d), lambda g: (seg(val(g)[0]), 0, 0)),
                  pl.BlockSpec((ne, d), lambda g: (act(g)[1], 0)),
                  pl.BlockSpec((d, ne), lambda g: (0, val(g)[1])),
                  pl.BlockSpec(rows, lambda g: (0, gate(g)[1], gate(g)[0])),
                  pl.BlockSpec(full, lambda g: (0, 0, gate(g)[0])),
                  pl.BlockSpec(rows, lambda g: (0, gate(g)[1], gate(g)[0])),
                  pl.BlockSpec(full, lambda g: (0, 0, gate(g)[0])),
                  pl.BlockSpec((1, d), lambda g: (0, 0))],
        out_specs=pl.BlockSpec((tm, d), lambda g: (val(g)[0], 0)),
        out_shape=jax.ShapeDtypeStruct((n_rows, d), F32),
        scratch_shapes=[pltpu.VMEM((tm // PEER_SUB, d, PEER_SUB), F32),
                        pltpu.VMEM((tm // PEER_SUB, ne, PEER_SUB), F32),
                        pltpu.VMEM((tm // PEER_SUB, ne, PEER_SUB), F32),
                        pltpu.VMEM((tm // PEER_SUB, ne, PEER_SUB), BF16),
                        pltpu.VMEM((tm // PEER_SUB, ne, PEER_SUB), BF16),
                        pltpu.VMEM((PEER_HEADS, n_i, 2, 8, tm + LANE), F32)],
        compiler_params=_cparams(("arbitrary",)),
        name="peer_experts",
    )(h, x, mods, u, vt, beta, s2, e1, e2, fg)


def _rope_tables(l_lat, nb, n_ctx_rows):
    rows = l_lat // GRID_W
    r = jnp.repeat(jnp.arange(rows), GRID_W)
    col = jnp.tile(jnp.arange(GRID_W), rows)
    pos = jnp.stack([r, col], axis=-1).astype(F32)
    nf = ATTN_HD // 4
    inv = 1.0 / (ROPE_BASE ** (jnp.arange(nf, dtype=F32) / nf))
    ang = pos[:, :, None] * inv
    cos, sin = jnp.cos(ang), jnp.sin(ang)
    c64 = jnp.concatenate([cos[:, 0], cos[:, 0], cos[:, 1], cos[:, 1]], axis=-1)
    s64 = jnp.concatenate([-sin[:, 0], sin[:, 0], -sin[:, 1], sin[:, 1]], axis=-1)
    c = jnp.tile(c64, (nb, LANE // ATTN_HD))
    s = jnp.tile(s64, (nb, LANE // ATTN_HD))
    c = jnp.concatenate([c, jnp.ones((n_ctx_rows, LANE), F32)], axis=0)
    s = jnp.concatenate([s, jnp.zeros((n_ctx_rows, LANE), F32)], axis=0)
    return c, s


def kernel(x, c, ctx, c_ctx, w_mod, b_mod, norm1_g, norm2_g, w_in, w_out, lam_q1, lam_k1, lam_q2, lam_k2, subln_g, ssm_lambda_re, ssm_lambda_im, ssm_log_step, ssm_b_re, ssm_b_im, ssm_c_re, ssm_c_im, ssm_d, ssm_w_glu, pool_w, pool_scale, peer_wq, peer_keys, peer_u, peer_v, final_g):
    nb, l_lat, d = x.shape
    l_ctx = ctx.shape[1]
    depth = w_mod.shape[0]
    n_lat = nb * l_lat
    n_all = n_lat + nb * l_ctx
    tiles_per_batch = l_lat // ROW_TILE
    assert l_lat % ROW_TILE == 0 and l_ctx == ROW_TILE and nb + 1 <= 8

    cs = jnp.concatenate([c, c_ctx[None], jnp.zeros((8 - nb - 1, d), F32)], axis=0)
    mod_all = _mod_vectors(cs, w_mod, b_mod)
    cos_t, sin_t = _rope_tables(l_lat, nb, nb * l_ctx)
    xs = jnp.concatenate([x.reshape(n_lat, d), ctx.reshape(nb * l_ctx, d)], axis=0)

    for l in range(depth):
        last = l == depth - 1
        n_rows = n_lat if last else n_all
        mods = mod_all[l, :nb + 1].reshape(nb + 1, N_MOD, d)
        q, k, v, us, up = _inproj(xs, norm1_g[l][None], mods, w_in[l].astype(BF16), cos_t, sin_t,
                                  tiles_per_batch, nb)
        lam_init = 0.8 - 0.6 * math.exp(-0.3 * l)
        lam = (jnp.exp(jnp.sum(lam_q1[l] * lam_k1[l])) - jnp.exp(jnp.sum(lam_q2[l] * lam_k2[l]))
               + lam_init).reshape(1).astype(F32)
        attn = _attention(q, k, v, lam, subln_g[l][None], nb, l_lat, l_ctx, not last,
                          1.0 - lam_init)
        mats = _ssm_matrices(ssm_lambda_re[l], ssm_lambda_im[l], ssm_log_step[l], ssm_b_re[l],
                             ssm_b_im[l], ssm_c_re[l], ssm_c_im[l])
        y = _ssm(us, mats, ssm_d[l][None], nb, l_lat, l_ctx, not last)
        w_blk = jax.scipy.linalg.block_diag(*[pool_w[l, gi] for gi in range(len(POOL_WINDOWS))])
        pool = _pool(up, w_blk.astype(BF16), pool_scale[l][None], nb, l_lat, l_ctx, n_rows)
        xs = _mixout(xs, attn, y, pool, mods, ssm_w_glu[l].astype(BF16), w_out[l].astype(BF16),
                     tiles_per_batch, nb, n_rows)
        h2, beta, s2, e1, e2 = _peer_scores(xs, norm2_g[l][None], mods, peer_wq[l].astype(BF16),
                                               peer_keys[l].astype(BF16), tiles_per_batch, nb, n_rows)
        xs = _peer_dense(h2, xs, mods, peer_u[l].astype(BF16), _transpose_cast(peer_v[l], BF16),
                         beta, s2, e1, e2, final_g[None], tiles_per_batch, nb, n_rows, last)
    return xs.reshape(nb, l_lat, d)
```

```python
import functools
import math

import numpy as np
import jax
import jax.numpy as jnp
from jax import lax
from jax.experimental import pallas as pl
from jax.experimental.pallas import tpu as pltpu

F32 = jnp.float32
BF16 = jnp.bfloat16

EPS = 1e-6
GRID_W = 64
N_MOD = 6
ATTN_HD = 64
ATTN_VD = 128
ATTN_VROWS = ATTN_VD + 16
LOG2E = float(np.log2(np.e))
ATTN_HEADS = 4
ATTN_W = 512
ROPE_BASE = 10000.0
SSM_W = 256
SSM_GROUP = 16
SSM_GROUPS = 16
SSM_STATE = 64
SSM_CHUNK = 16
POOL_W = 256
POOL_WINDOWS = (2, 4, 8, 16)
POOL_GROUP = 64
POOL_HALO = 8
SSM_OFF = 3 * ATTN_W
POOL_OFF = SSM_OFF + SSM_W
IN_W = POOL_OFF + POOL_W
PEER_HEADS = 8
PEER_NKEYS = 128
PEER_KDIM = 128
PEER_TOPK = 16
NEG_BIG = -3.0e38
SQRT_HALF = float(np.sqrt(0.5).astype(np.float32))

LANE = 128
ROW_TILE = 256
VMEM_LIMIT = 56 * 1024 * 1024


def _cparams(sem):
    return pltpu.CompilerParams(dimension_semantics=sem, vmem_limit_bytes=VMEM_LIMIT)


def _gelu(x):
    return 0.5 * x * (1.0 + lax.erf(x * SQRT_HALF))


def _norm_mod(x, g, shift, scale):
    ms = jnp.mean(x * x, axis=-1, keepdims=True)
    y = x * lax.rsqrt(ms + EPS) * g
    return y * (1.0 + scale) + shift


def _mod_kernel(s_ref, w_ref, b_ref, o_ref):
    s = s_ref[...]
    s = s * jax.nn.sigmoid(s)
    o_ref[0] = jnp.dot(s.astype(BF16), w_ref[0].astype(BF16), preferred_element_type=F32) + b_ref[0]


def _mod_vectors(cs, w_mod, b_mod):
    depth, d, n = w_mod.shape
    tn = 1536
    return pl.pallas_call(
        _mod_kernel,
        grid=(depth, n // tn),
        in_specs=[pl.BlockSpec((8, d), lambda l, j: (0, 0)),
                  pl.BlockSpec((1, d, tn), lambda l, j: (l, 0, j)),
                  pl.BlockSpec((1, 1, tn), lambda l, j: (l, 0, j))],
        out_specs=pl.BlockSpec((1, 8, tn), lambda l, j: (l, 0, j)),
        out_shape=jax.ShapeDtypeStruct((depth, 8, n), F32),
        compiler_params=_cparams(("arbitrary", "arbitrary")),
        name="mod_vectors",
    )(cs, w_mod, b_mod.reshape(depth, 1, n))


def _stream_tile(xa_ref, xb_ref, n_a):
    return jnp.where(pl.program_id(0) < n_a, xa_ref[...], xb_ref[...])


def _stream_specs(xa, xb, n_a, tm):
    d = xa.shape[1]
    return [pl.BlockSpec((tm, d), lambda i: (jnp.minimum(i, n_a - 1), 0)),
            pl.BlockSpec((tm, d), lambda i: (jnp.maximum(i - n_a, 0), 0))]


def _inproj_kernel(xa_ref, xb_ref, g_ref, mod_ref, w_ref, cos_ref, sin_ref,
                   q_ref, k_ref, v_ref, us_ref, up_ref, *, n_a):
    tm = xa_ref.shape[0]
    h = _norm_mod(_stream_tile(xa_ref, xb_ref, n_a), g_ref[...], mod_ref[0, 0:1, :], mod_ref[0, 1:2, :])
    p = jnp.dot(h.astype(BF16), w_ref[...], preferred_element_type=F32)
    c = cos_ref[...]
    s = sin_ref[...]
    lane = lax.broadcasted_iota(jnp.int32, (tm, LANE), 1)
    first = (lane % 32) < 16
    for off, ref, sc in ((0, q_ref, ATTN_HD ** -0.5 * LOG2E), (ATTN_W, k_ref, 1.0)):
        for blk in range(ATTN_W // LANE):
            xb = p[:, off + LANE * blk: off + LANE * (blk + 1)]
            partner = jnp.where(first, pltpu.roll(xb, LANE - 16, 1), pltpu.roll(xb, 16, 1))
            ref[:, LANE * blk: LANE * (blk + 1)] = ((xb * c + partner * s) * sc).astype(BF16)
    vt = p[:, 2 * ATTN_W:SSM_OFF].T.astype(BF16)
    ones_rows = (lax.broadcasted_iota(jnp.int32, (ATTN_VROWS - ATTN_VD, tm), 0) == 0).astype(BF16)
    for hd in range(ATTN_HEADS):
        v_ref[0, hd * ATTN_VROWS:hd * ATTN_VROWS + ATTN_VD, :] = vt[hd * ATTN_VD:(hd + 1) * ATTN_VD]
        v_ref[0, hd * ATTN_VROWS + ATTN_VD:(hd + 1) * ATTN_VROWS, :] = ones_rows
    us_ref[...] = p[:, SSM_OFF:POOL_OFF]
    up_ref[...] = p[:, POOL_OFF:IN_W]


def _seg_map(tiles_per_batch, nb):
    def seg(i):
        return jnp.minimum(i // tiles_per_batch, nb)
    return seg


def _inproj(xa, xb, n_a, t, g, mods, w, cos_t, sin_t, tiles_per_batch, nb):
    d = xa.shape[1]
    tm = ROW_TILE
    seg = _seg_map(tiles_per_batch, nb)
    row = lambda i: (i, 0)
    rope_row = lambda i: (jnp.where(i < nb * tiles_per_batch, i % tiles_per_batch, tiles_per_batch), 0)
    return pl.pallas_call(
        functools.partial(_inproj_kernel, n_a=n_a),
        grid=(t // tm,),
        in_specs=_stream_specs(xa, xb, n_a, tm) + [
                  pl.BlockSpec((1, d), lambda i: (0, 0)),
                  pl.BlockSpec((1, N_MOD, d), lambda i: (seg(i), 0, 0)),
                  pl.BlockSpec((d, IN_W), lambda i: (0, 0)),
                  pl.BlockSpec((tm, LANE), rope_row),
                  pl.BlockSpec((tm, LANE), rope_row)],
        out_specs=[pl.BlockSpec((tm, ATTN_W), row),
                   pl.BlockSpec((tm, ATTN_W), row),
                   pl.BlockSpec((1, ATTN_HEADS * ATTN_VROWS, tm), lambda i: (i, 0, 0)),
                   pl.BlockSpec((tm, SSM_W), row),
                   pl.BlockSpec((tm, POOL_W), row)],
        out_shape=[jax.ShapeDtypeStruct((t, ATTN_W), BF16),
                   jax.ShapeDtypeStruct((t, ATTN_W), BF16),
                   jax.ShapeDtypeStruct((t // tm, ATTN_HEADS * ATTN_VROWS, tm), BF16),
                   jax.ShapeDtypeStruct((t, SSM_W), F32),
                   jax.ShapeDtypeStruct((t, POOL_W), F32)],
        compiler_params=_cparams(("arbitrary",)),
        name="inproj",
    )(xa, xb, g, mods, w, cos_t, sin_t)


def _attn_stages(q, tq):
    lane = lax.broadcasted_iota(jnp.int32, (tq, LANE), 1)
    zero = jnp.zeros_like(q)
    qs = jnp.concatenate([jnp.where(lane < ATTN_HD, q, zero),
                          jnp.where(lane >= ATTN_HD, q, zero)], axis=0)

    def scores(kb):
        return lax.dot_general(kb, qs, (((1,), (1,)), ((), ())), preferred_element_type=F32)

    def softmax(m, s):
        m_new = jnp.maximum(m, jnp.max(s, axis=0, keepdims=True))
        return m_new, jnp.exp2(m - m_new), jnp.exp2(s - m_new).astype(BF16)

    def weighted(acc, alpha, pb, vts):
        acc = alpha * acc
        rows = pb.shape[0] // len(vts)
        for c, vt in enumerate(vts):
            acc = acc + jnp.dot(vt, pb[c * rows:(c + 1) * rows], preferred_element_type=F32)
        return acc

    return scores, softmax, weighted


def _attn_finish(lam, acc, g, tq, out_scale):
    o = acc[:ATTN_VD] / acc[ATTN_VD:ATTN_VD + 1]
    o = (o[:, :tq] - lam * o[:, tq:]).T
    ms = jnp.mean(o * o, axis=-1, keepdims=True)
    return (o * lax.rsqrt(ms + EPS) * g * out_scale).astype(BF16)


def _attn_ctx_kernel(lam_ref, q_ref, kc_ref, vc_ref, g_ref, o_ref, *, out_scale):
    tq = q_ref.shape[0]
    scores, softmax, weighted = _attn_stages(q_ref[...], tq)
    m = jnp.full((1, 2 * tq), NEG_BIG, F32)
    acc = jnp.zeros((ATTN_VROWS, 2 * tq), F32)
    m, alpha, pb = softmax(m, scores(kc_ref[...]))
    acc = weighted(acc, alpha, pb, [vc_ref[0]])
    o_ref[...] = _attn_finish(lam_ref[0], acc, g_ref[...], tq, out_scale)


def _attn_kernel(lam_ref, q_ref, kl_ref, kc_ref, vl_ref, vc_ref, g_ref, o_ref,
                 s_a, s_b, p_a, p_b, acc_ref, *, n_lat_k, tk, out_scale):
    tq = q_ref.shape[0]
    scores, softmax, weighted = _attn_stages(q_ref[...], tq)
    n_sub = tk // vc_ref.shape[-1]

    def stage_scores(t, s_buf):
        start = pl.multiple_of(t * tk, tk)
        s_buf[...] = scores(kl_ref[pl.ds(start, tk), :])

    def stage_softmax(m, s_buf, p_buf):
        m, alpha, pb = softmax(m, s_buf[...])
        p_buf[...] = pb
        return m, alpha

    def stage_values(alpha, p_buf, t):
        acc_ref[...] = weighted(acc_ref[...], alpha, p_buf[...],
                                [vl_ref[t * n_sub + c] for c in range(n_sub)])

    m = jnp.full((1, 2 * tq), NEG_BIG, F32)
    m, alpha, pb = softmax(m, scores(kc_ref[...]))
    acc_ref[...] = weighted(jnp.zeros((ATTN_VROWS, 2 * tq), F32), alpha, pb, [vc_ref[0]])

    if n_lat_k % 2 == 0:
        stage_scores(0, s_a)
        stage_scores(1, s_b)
        m, alpha = stage_softmax(m, s_a, p_a)

        def pair(i, carry):
            m, alpha = carry
            k = 2 * i
            stage_scores(k + 2, s_a)
            m, alpha_n = stage_softmax(m, s_b, p_b)
            stage_values(alpha, p_a, k)
            stage_scores(k + 3, s_b)
            m, alpha_nn = stage_softmax(m, s_a, p_a)
            stage_values(alpha_n, p_b, k + 1)
            return m, alpha_nn

        m, alpha = lax.fori_loop(0, n_lat_k // 2 - 1, pair, (m, alpha))
        m, alpha_n = stage_softmax(m, s_b, p_b)
        stage_values(alpha, p_a, n_lat_k - 2)
        stage_values(alpha_n, p_b, n_lat_k - 1)
    else:
        for t in range(n_lat_k):
            stage_scores(t, s_a)
            m, alpha = stage_softmax(m, s_a, p_a)
            stage_values(alpha, p_a, t)
    o_ref[...] = _attn_finish(lam_ref[0], acc_ref[...], g_ref[...], tq, out_scale)


def _attention(q, k, vt, lam, g, nb, l_lat, l_ctx, with_ctx_queries, out_scale):
    tq_ctx = ROW_TILE
    tq = 512 if l_lat % 512 == 0 else ROW_TILE
    tk = 512 if l_lat % 512 == 0 else l_lat
    vt_tile = vt.shape[-1]
    assert l_ctx == tq_ctx and l_lat % tq == 0 and vt_tile == l_ctx and tk % vt_tile == 0
    n_lat_q = l_lat // tq
    n_vt_lat = l_lat // vt_tile
    ctx_blk0 = nb * l_lat // l_ctx
    smem = pl.BlockSpec(memory_space=pltpu.SMEM)
    gain = pl.BlockSpec((1, LANE), lambda b, h, i: (0, 0))
    kc_spec = pl.BlockSpec((l_ctx, LANE), lambda b, h, i: (ctx_blk0 + b, h))
    vc_spec = pl.BlockSpec((1, ATTN_VROWS, vt_tile), lambda b, h, i: (nb * n_vt_lat + b, h, 0))
    sem = _cparams(("arbitrary", "arbitrary", "arbitrary"))
    lat = pl.pallas_call(
        functools.partial(_attn_kernel, n_lat_k=l_lat // tk, tk=tk, out_scale=out_scale),
        grid=(nb, ATTN_HEADS, n_lat_q),
        in_specs=[smem,
                  pl.BlockSpec((tq, LANE), lambda b, h, i: (b * n_lat_q + i, h)),
                  pl.BlockSpec((l_lat, LANE), lambda b, h, i: (b, h)),
                  kc_spec,
                  pl.BlockSpec((n_vt_lat, ATTN_VROWS, vt_tile), lambda b, h, i: (b, h, 0)),
                  vc_spec, gain],
        out_specs=pl.BlockSpec((tq, LANE), lambda b, h, i: (b * n_lat_q + i, h)),
        out_shape=jax.ShapeDtypeStruct((nb * l_lat, ATTN_W), BF16),
        scratch_shapes=[pltpu.VMEM((tk, 2 * tq), F32), pltpu.VMEM((tk, 2 * tq), F32),
                        pltpu.VMEM((tk, 2 * tq), BF16), pltpu.VMEM((tk, 2 * tq), BF16),
                        pltpu.VMEM((ATTN_VROWS, 2 * tq), F32)],
        compiler_params=sem,
        name="diff_attention",
    )(lam, q, k, k, vt, vt, g)
    if not with_ctx_queries:
        return lat
    ctx = pl.pallas_call(
        functools.partial(_attn_ctx_kernel, out_scale=out_scale),
        grid=(nb, ATTN_HEADS, 1),
        in_specs=[smem,
                  pl.BlockSpec((tq_ctx, LANE), lambda b, h, i: (ctx_blk0 + b, h)),
                  kc_spec, vc_spec, gain],
        out_specs=pl.BlockSpec((tq_ctx, LANE), lambda b, h, i: (b, h)),
        out_shape=jax.ShapeDtypeStruct((nb * l_ctx, ATTN_W), BF16),
        compiler_params=sem,
        name="diff_attention_ctx",
    )(lam, q, k, vt, g)
    return jnp.concatenate([lat, ctx], axis=0)


def _ssm_kernel(ul_ref, uc_ref, m_ref, bm_ref, cm_ref, a_ref, d_ref, yl_ref, yc_ref,
                s_scr, h_scr, *, ctx_out):
    tc, pw = SSM_CHUNK, 2 * SSM_GROUP
    nl = ul_ref.shape[0] // tc
    nc = uc_ref.shape[0] // tc
    w = m_ref.shape[-1]
    half = s_scr.shape[-1] // 2
    sub = pl.program_id(1) % (LANE // pw)
    nt_dims = (((1,), (1,)), ((), ()))
    ch = lax.broadcasted_iota(jnp.int32, (LANE, 1), 0) - sub * pw
    valid = (ch >= 0) & (ch < pw)
    tgt0 = (ch >> 4) * (tc * SSM_GROUP) + (ch & (SSM_GROUP - 1))
    col = lax.broadcasted_iota(jnp.int32, (1, w), 1)

    def perm(t):
        return jnp.where(valid & (col == tgt0 + t * SSM_GROUP), 1.0, 0.0).astype(BF16)

    def fold(ref, n):
        acc = jnp.zeros((n, w), F32)
        for t in range(tc):
            acc = acc + jnp.dot(ref[pl.ds(t, n, stride=tc), :].astype(BF16), perm(t),
                                preferred_element_type=F32)
        return acc.astype(BF16)

    def unfold_add(y, ref, n):
        hi = y.astype(BF16)
        r1 = y - hi.astype(F32)
        mid = r1.astype(BF16)
        lo = (r1 - mid.astype(F32)).astype(BF16)
        for t in range(tc):
            p = perm(t)
            o = sum(lax.dot_general(part, p, nt_dims, preferred_element_type=F32)
                    for part in (hi, mid, lo))
            ref[pl.ds(t, n, stride=tc), :] += o

    @pl.when(sub == 0)
    def _():
        yl_ref[...] = ul_ref[...] * d_ref[...]
        yc_ref[...] = uc_ref[...] * d_ref[...] if ctx_out else jnp.zeros_like(yc_ref)

    ulb = fold(ul_ref, nl)
    ucb = fold(uc_ref, nc)
    yl = jnp.dot(ulb, m_ref[0], preferred_element_type=F32)
    if ctx_out:
        yc = jnp.dot(ucb, m_ref[0], preferred_element_type=F32)
    for dr in range(2):
        s_scr[dr, 0:nc, :] = jnp.dot(ucb, bm_ref[dr, 0], preferred_element_type=F32)
        s_scr[dr, nc:nc + nl, :] = jnp.dot(ulb, bm_ref[dr, 0], preferred_element_type=F32)
    a = a_ref[0]
    afr, afi, arr, ari = a[0:1], a[1:2], a[2:3], a[3:4]

    def make_body(base, n):
        def body(t, carry):
            fr, fi, rr, ri = carry
            rf = base + t
            rv = base + n - 1 - t
            h_scr[0, pl.ds(rf, 1), :] = jnp.concatenate([fr, fi], axis=-1)
            h_scr[1, pl.ds(rv, 1), :] = jnp.concatenate([rr, ri], axis=-1)
            sf = s_scr[0, pl.ds(rf, 1), :]
            sv = s_scr[1, pl.ds(rv, 1), :]
            nfr = afr * fr - afi * fi + sf[:, :half]
            nfi = afr * fi + afi * fr + sf[:, half:]
            nrr = arr * rr - ari * ri + sv[:, :half]
            nri = arr * ri + ari * rr + sv[:, half:]
            return nfr, nfi, nrr, nri
        return body

    z = jnp.zeros((1, half), F32)
    carry = lax.fori_loop(0, nc, make_body(0, nc), (z, z, z, z))
    lax.fori_loop(0, nl, make_body(nc, nl), carry)
    for dr in range(2):
        yl = yl + jnp.dot(h_scr[dr, nc:nc + nl, :].astype(BF16), cm_ref[dr, 0],
                          preferred_element_type=F32)
        if ctx_out:
            yc = yc + jnp.dot(h_scr[dr, 0:nc, :].astype(BF16), cm_ref[dr, 0],
                              preferred_element_type=F32)
    unfold_add(yl, yl_ref, nl)
    if ctx_out:
        unfold_add(yc, yc_ref, nc)


def _ssm_matrices(lam_re, lam_im, log_step, b_re, b_im, c_re, c_im):
    tc = SSM_CHUNK
    g, p, hh = SSM_GROUPS, SSM_STATE, SSM_GROUP
    npair = g // 2
    step = jnp.exp(log_step)[..., None]
    den = lam_re * lam_re + lam_im * lam_im

    def power(k):
        er = jnp.exp(lam_re * step * k)
        return er * jnp.cos(lam_im * step * k), er * jnp.sin(lam_im * step * k)

    ar, ai = power(1.0)
    nr = ar - 1.0
    cr_ = (nr * lam_re + ai * lam_im) / den
    ci_ = (ai * lam_re - nr * lam_im) / den
    bbr = cr_[..., None] * b_re - ci_[..., None] * b_im
    bbi = cr_[..., None] * b_im + ci_[..., None] * b_re
    ks = jnp.arange(tc + 1, dtype=F32)
    pw = jax.vmap(power)(ks)
    pwr, pwi = pw
    cpr = c_re[None] * pwr[:, :, :, None, :] - c_im[None] * pwi[:, :, :, None, :]
    cpi = c_re[None] * pwi[:, :, :, None, :] + c_im[None] * pwr[:, :, :, None, :]
    hp = lax.Precision.HIGHEST
    kk = (jnp.einsum('kdgop,dgpi->kdgoi', cpr[:tc], bbr, precision=hp)
          - jnp.einsum('kdgop,dgpi->kdgoi', cpi[:tc], bbi, precision=hp))
    s_idx = jnp.arange(tc)[:, None]
    t_idx = jnp.arange(tc)[None, :]
    lag_f = t_idx - s_idx
    lag_r = s_idx - t_idx
    mf = jnp.where((lag_f >= 0)[:, :, None, None, None], kk[jnp.clip(lag_f, 0, tc - 1), 0], 0.0)
    mr = jnp.where((lag_r >= 0)[:, :, None, None, None], kk[jnp.clip(lag_r, 0, tc - 1), 1], 0.0)
    mm = mf + mr
    mm = jnp.transpose(mm, (2, 0, 4, 1, 3)).reshape(g, tc * hh, tc * hh)
    zero = jnp.zeros_like(mm[0::2])
    m2 = jnp.concatenate([jnp.concatenate([mm[0::2], zero], -1),
                          jnp.concatenate([zero, mm[1::2]], -1)], -2)
    def bmat(dr, exps):
        pr = pwr[exps, dr]
        pi = pwi[exps, dr]
        re = pr[..., None] * bbr[dr][None] - pi[..., None] * bbi[dr][None]
        im = pr[..., None] * bbi[dr][None] + pi[..., None] * bbr[dr][None]
        re = jnp.transpose(re, (1, 0, 3, 2)).reshape(g, tc * hh, p)
        im = jnp.transpose(im, (1, 0, 3, 2)).reshape(g, tc * hh, p)
        z = jnp.zeros_like(re[0::2])
        top = jnp.concatenate([re[0::2], z, im[0::2], z], -1)
        bot = jnp.concatenate([z, re[1::2], z, im[1::2]], -1)
        return jnp.concatenate([top, bot], -2)
    bm = jnp.stack([bmat(0, jnp.arange(tc - 1, -1, -1)), bmat(1, jnp.arange(tc))])
    def cmat(dr, exps):
        re = jnp.transpose(cpr[exps, dr], (1, 3, 0, 2)).reshape(g, p, tc * hh)
        im = jnp.transpose(cpi[exps, dr], (1, 3, 0, 2)).reshape(g, p, tc * hh)
        z = jnp.zeros_like(re[0::2])
        return jnp.concatenate([jnp.concatenate([re[0::2], z], -1),
                                jnp.concatenate([z, re[1::2]], -1),
                                jnp.concatenate([-im[0::2], z], -1),
                                jnp.concatenate([z, -im[1::2]], -1)], -2)
    cm = jnp.stack([cmat(0, jnp.arange(1, tc + 1)), cmat(1, jnp.arange(tc, 0, -1))])
    a16r = pwr[tc].reshape(2, npair, 2 * p)
    a16i = pwi[tc].reshape(2, npair, 2 * p)
    a16 = jnp.stack([a16r[0], a16i[0], a16r[1], a16i[1]], axis=1)
    return m2.astype(BF16), bm.astype(BF16), cm.astype(BF16), a16


def _ssm(us, mats, d, nb, l_lat, l_ctx, ctx_out):
    m2, bm, cm, a16 = mats
    tc, hh = SSM_CHUNK, SSM_GROUP
    npair = SSM_GROUPS // 2
    w = 2 * tc * hh
    per_col = LANE // (2 * hh)
    nl, nc = l_lat // tc, l_ctx // tc
    ctx_blk0 = nb * l_lat // l_ctx
    yl, yc = pl.pallas_call(
        functools.partial(_ssm_kernel, ctx_out=ctx_out),
        grid=(nb, npair),
        in_specs=[pl.BlockSpec((l_lat, LANE), lambda b, q: (b, q // per_col)),
                  pl.BlockSpec((l_ctx, LANE), lambda b, q: (ctx_blk0 + b, q // per_col)),
                  pl.BlockSpec((1, w, w), lambda b, q: (q, 0, 0)),
                  pl.BlockSpec((2, 1, w, 4 * SSM_STATE), lambda b, q: (0, q, 0, 0)),
                  pl.BlockSpec((2, 1, 4 * SSM_STATE, w), lambda b, q: (0, q, 0, 0)),
                  pl.BlockSpec((1, 4, 2 * SSM_STATE), lambda b, q: (q, 0, 0)),
                  pl.BlockSpec((1, LANE), lambda b, q: (0, q // per_col))],
        out_specs=[pl.BlockSpec((l_lat, LANE), lambda b, q: (b, q // per_col)),
                   pl.BlockSpec((l_ctx, LANE), lambda b, q: (b, q // per_col))],
        out_shape=[jax.ShapeDtypeStruct((nb * l_lat, SSM_W), F32),
                   jax.ShapeDtypeStruct((nb * l_ctx, SSM_W), F32)],
        scratch_shapes=[pltpu.VMEM((2, nc + nl, 4 * SSM_STATE), F32),
                        pltpu.VMEM((2, nc + nl, 4 * SSM_STATE), F32)],
        compiler_params=_cparams(("arbitrary", "arbitrary")),
        name="s5_scan",
    )(us, us, m2, bm, cm, a16, d)
    return jnp.concatenate([yl, yc], axis=0) if ctx_out else yl


def _pool_kernel(prev_ref, cur_ref, next_ref, w_ref, scale_ref, o_ref, ext,
                 *, n_lat_tiles, tiles_per_lat, l_lat, l_ctx):
    r = cur_ref.shape[0]
    hl = POOL_HALO
    i = pl.program_id(0)
    is_lat = i < n_lat_tiles
    pos = jnp.where(is_lat, i % tiles_per_lat, 0)
    n_tiles = jnp.where(is_lat, tiles_per_lat, l_ctx // r)
    seq_len = jnp.where(is_lat, l_lat, l_ctx)
    zeros = jnp.zeros((hl, POOL_W), F32)
    ext[0:hl, :] = jnp.where(pos > 0, prev_ref[...], zeros)
    ext[hl:hl + r, :] = cur_ref[...]
    ext[hl + r:hl + r + hl, :] = jnp.where(pos < n_tiles - 1, next_ref[...], zeros)
    u = cur_ref[...]
    t = pos * r + lax.broadcasted_iota(jnp.int32, (r, 1), 0)
    lane = lax.broadcasted_iota(jnp.int32, (r, POOL_W), 1)

    def win(k):
        return ext[hl + k:hl + k + r, :]

    acc = win(-1) + u
    mean = jnp.zeros((r, POOL_W), F32)
    lo_k, hi_k = -1, 0
    for gi, wn in enumerate(POOL_WINDOWS):
        hw = wn // 2
        while lo_k > -hw:
            lo_k -= 1
            acc = acc + win(lo_k)
        while hi_k < hw - 1:
            hi_k += 1
            acc = acc + win(hi_k)
        cnt = (jnp.minimum(t + hw, seq_len) - jnp.maximum(t - hw, 0)).astype(F32)
        sel = (lane >= gi * POOL_GROUP) & (lane < (gi + 1) * POOL_GROUP)
        mean = jnp.where(sel, acc / cnt, mean)
    dlt = (mean - u).astype(BF16)
    o_ref[...] = jnp.dot(dlt, w_ref[...], preferred_element_type=F32) * scale_ref[...]


def _pool(up, w_blk, scale, nb, l_lat, l_ctx, n_rows):
    r = ROW_TILE
    hl = POOL_HALO
    n_tiles = n_rows // r
    last8 = up.shape[0] // hl - 1
    kern = functools.partial(_pool_kernel, n_lat_tiles=nb * l_lat // r, tiles_per_lat=l_lat // r,
                             l_lat=l_lat, l_ctx=l_ctx)
    return pl.pallas_call(
        kern,
        grid=(n_tiles,),
        in_specs=[pl.BlockSpec((hl, POOL_W), lambda i: (jnp.maximum(i * (r // hl) - 1, 0), 0)),
                  pl.BlockSpec((r, POOL_W), lambda i: (i, 0)),
                  pl.BlockSpec((hl, POOL_W), lambda i: (jnp.minimum((i + 1) * (r // hl), last8), 0)),
                  pl.BlockSpec((POOL_W, POOL_W), lambda i: (0, 0)),
                  pl.BlockSpec((1, POOL_W), lambda i: (0, 0))],
        out_specs=pl.BlockSpec((r, POOL_W), lambda i: (i, 0)),
        out_shape=jax.ShapeDtypeStruct((n_rows, POOL_W), F32),
        scratch_shapes=[pltpu.VMEM((r + 2 * hl, POOL_W), F32)],
        compiler_params=_cparams(("arbitrary",)),
        name="pool_mix",
    )(up, up, up, w_blk, scale)


def _mixout_kernel(xa_ref, xb_ref, attn_ref, y_ref, pool_ref, mod_ref, wglu_ref, wout_ref, o_ref,
                   *, n_a):
    yg = _gelu(y_ref[...])
    z = yg * jax.nn.sigmoid(jnp.dot(yg.astype(BF16), wglu_ref[...], preferred_element_type=F32))
    r = jnp.dot(attn_ref[...], wout_ref[0:ATTN_W, :], preferred_element_type=F32)
    r = r + jnp.dot(z.astype(BF16), wout_ref[ATTN_W:ATTN_W + SSM_W, :], preferred_element_type=F32)
    r = r + jnp.dot(pool_ref[...].astype(BF16), wout_ref[ATTN_W + SSM_W:, :],
                    preferred_element_type=F32)
    o_ref[...] = _stream_tile(xa_ref, xb_ref, n_a) + mod_ref[0, 2:3, :] * r


def _mixout(xa, xb, n_a, attn, y, pool, mods, wglu, wout, tiles_per_batch, nb, n_rows):
    d = xa.shape[1]
    tm = ROW_TILE
    seg = _seg_map(tiles_per_batch, nb)
    row = lambda i: (i, 0)
    return pl.pallas_call(
        functools.partial(_mixout_kernel, n_a=n_a),
        grid=(n_rows // tm,),
        in_specs=_stream_specs(xa, xb, n_a, tm) + [
                  pl.BlockSpec((tm, ATTN_W), row),
                  pl.BlockSpec((tm, SSM_W), row),
                  pl.BlockSpec((tm, POOL_W), row),
                  pl.BlockSpec((1, N_MOD, d), lambda i: (seg(i), 0, 0)),
                  pl.BlockSpec((SSM_W, SSM_W), lambda i: (0, 0)),
                  pl.BlockSpec((d, d), lambda i: (0, 0))],
        out_specs=pl.BlockSpec((tm, d), row),
        out_shape=jax.ShapeDtypeStruct((n_rows, d), F32),
        compiler_params=_cparams(("arbitrary",)),
        name="mix_out",
    )(xa, xb, attn, y, pool, mods, wglu, wout)


def _oddeven_merge(lo, hi, r):
    step = r * 2
    if step < hi - lo:
        yield from _oddeven_merge(lo, hi, step)
        yield from _oddeven_merge(lo + r, hi, step)
        yield from [(i, i + r) for i in range(lo + r, hi - r, step)]
    else:
        yield (lo, lo + r)


def _oddeven_sort(lo, hi):
    if hi - lo >= 1:
        mid = lo + (hi - lo) // 2
        yield from _oddeven_sort(lo, mid)
        yield from _oddeven_sort(mid + 1, hi)
        yield from _oddeven_merge(lo, hi, 1)


_SORT16 = tuple(_oddeven_sort(0, PEER_TOPK - 1))
_BITONIC16 = tuple((i, i + d) for d in (8, 4, 2, 1) for i in range(PEER_TOPK) if not i & d)


def _compare_exchange(rows, net):
    rows = list(rows)
    for i, j in net:
        hi, lo = jnp.maximum(rows[i], rows[j]), jnp.minimum(rows[i], rows[j])
        rows[i], rows[j] = hi, lo
    return rows


def _top_sorted(s, k):
    assert k == PEER_TOPK and s.shape[0] == 8 * PEER_TOPK
    rows = _compare_exchange([s[8 * v:8 * v + 8] for v in range(PEER_TOPK)], _SORT16)
    for shift in (4, 2, 1):
        other = [pltpu.roll(r, shift, 0) for r in rows]
        rows = _compare_exchange([jnp.maximum(rows[i], other[PEER_TOPK - 1 - i])
                                  for i in range(PEER_TOPK)], _BITONIC16)
    return [r[0:1] for r in rows]


def _peer_score_kernel(x_ref, g_ref, mod_ref, wq_ref, keys_ref,
                       h_ref, beta_ref, s2_ref, e1_ref, e2_ref, q_scr):
    tm = x_ref.shape[0]
    h = _norm_mod(x_ref[...], g_ref[...], mod_ref[0, 3:4, :], mod_ref[0, 4:5, :]).astype(BF16)
    h_ref[...] = h
    q = jnp.dot(h, wq_ref[...], preferred_element_type=F32)
    for hc in range(2 * PEER_HEADS):
        q_scr[hc] = q[:, hc * PEER_KDIM:(hc + 1) * PEER_KDIM].astype(BF16)
    row8 = lax.broadcasted_iota(jnp.int32, (8, tm), 0)

    def head(hd, _):
        nt = (((1,), (1,)), ((), ()))
        s1 = lax.dot_general(keys_ref[hd, 0], q_scr[2 * hd], nt,
                             preferred_element_type=F32)
        s2 = lax.dot_general(keys_ref[hd, 1], q_scr[2 * hd + 1], nt,
                             preferred_element_type=F32)
        a = _top_sorted(s1, PEER_TOPK)
        b = _top_sorted(s2, PEER_TOPK)
        acat = jnp.concatenate(a, axis=0)
        bcat = jnp.concatenate(b, axis=0)
        pieces = [a[0] + bcat]
        for i in range(2, 9):
            piece = a[i - 1] + bcat[0:8]
            n_valid = PEER_TOPK // i
            pieces.append(piece if n_valid >= 8 else jnp.where(row8 < n_valid, piece, NEG_BIG))
        pieces.append(acat[8:16] + b[0])
        cand = jnp.concatenate(pieces, axis=0)
        work = cand
        cum = jnp.zeros((1, tm), F32)
        tau = jnp.full((1, tm), NEG_BIG, F32)
        for _ in range(PEER_TOPK):
            m = jnp.max(work, axis=0, keepdims=True)
            eq = work == m
            new = cum + jnp.sum(eq.astype(F32), axis=0, keepdims=True)
            tau = jnp.where((cum < PEER_TOPK) & (new >= PEER_TOPK), m, tau)
            work = jnp.where(eq, NEG_BIG, work)
            cum = new
        top = a[0] + b[0]
        z = jnp.sum(jnp.where(cand >= tau, jnp.exp(cand - top), 0.0), axis=0, keepdims=True)
        big = -NEG_BIG
        b_rows = [bcat] + [bcat[0:8]] * 7
        beta_r = [jnp.min(jnp.where(pc >= tau, br, big), axis=0, keepdims=True)
                  for pc, br in zip(pieces[:8], b_rows)]
        tail = jnp.where(pieces[8] >= tau, b[0], big)
        beta_r += [tail[r:r + 1] for r in range(8)]
        beta = jnp.full_like(s1, big)
        for r in range(PEER_TOPK):
            beta = jnp.where(s1 == a[r], beta_r[r], beta)
        beta_ref[hd] = beta
        s2_ref[hd] = s2
        e1_ref[hd] = jnp.exp(s1 - a[0])
        e2_ref[hd] = jnp.exp(s2 - b[0]) * (1.0 / z)
        return 0

    lax.fori_loop(0, PEER_HEADS, head, 0, unroll=4)


def _peer_scores(x, g, mods, wq, keys, tiles_per_batch, nb, n_rows):
    d = x.shape[1]
    tm = ROW_TILE
    seg = _seg_map(tiles_per_batch, nb)
    nq = wq.shape[1]
    col = lambda i: (0, 0, i)
    big = jax.ShapeDtypeStruct((PEER_HEADS, PEER_NKEYS, n_rows), F32)
    return pl.pallas_call(
        _peer_score_kernel,
        grid=(n_rows // tm,),
        in_specs=[pl.BlockSpec((tm, d), lambda i: (i, 0)),
                  pl.BlockSpec((1, d), lambda i: (0, 0)),
                  pl.BlockSpec((1, N_MOD, d), lambda i: (seg(i), 0, 0)),
                  pl.BlockSpec((d, nq), lambda i: (0, 0)),
                  pl.BlockSpec((PEER_HEADS, 2, PEER_NKEYS, PEER_KDIM), lambda i: (0, 0, 0, 0))],
        out_specs=[pl.BlockSpec((tm, d), lambda i: (i, 0)),
                   pl.BlockSpec((PEER_HEADS, PEER_NKEYS, tm), col),
                   pl.BlockSpec((PEER_HEADS, PEER_NKEYS, tm), col),
                   pl.BlockSpec((PEER_HEADS, PEER_NKEYS, tm), col),
                   pl.BlockSpec((PEER_HEADS, PEER_NKEYS, tm), col)],
        out_shape=[jax.ShapeDtypeStruct((n_rows, d), BF16), big, big, big, big],
        scratch_shapes=[pltpu.VMEM((2 * PEER_HEADS, tm, PEER_KDIM), BF16)],
        compiler_params=_cparams(("arbitrary",)),
        name="peer_scores",
    )(x, g, mods, wq, keys)


def _transpose_cast_kernel(x_ref, o_ref):
    o_ref[...] = x_ref[...].T.astype(o_ref.dtype)


def _transpose_cast(x, dtype, rows_per_step=512):
    n, d = x.shape
    return pl.pallas_call(
        _transpose_cast_kernel,
        grid=(n // rows_per_step,),
        in_specs=[pl.BlockSpec((rows_per_step, d), lambda i: (i, 0))],
        out_specs=pl.BlockSpec((d, rows_per_step), lambda i: (0, i)),
        out_shape=jax.ShapeDtypeStruct((d, n), dtype),
        compiler_params=_cparams(("arbitrary",)),
        name="transpose_cast",
    )(x)


PEER_CHUNK_KEYS = 8
PEER_TOK_TILE = 512
PEER_SUB = 256


def _peer_dense_kernel(h_ref, x_ref, mod_ref, u_ref, vt_ref, beta_ref, s2_ref, e1_ref, e2_ref,
                       fg_ref, o_ref, acc_ref, a0, a1, w0, w1, row_scr,
                       *, n_chunks, final_norm):
    n_i = PEER_CHUNK_KEYS
    tm = h_ref.shape[0]
    g = pl.program_id(0)
    nt_dims = (((1,), (1,)), ((), ()))
    n_k = u_ref.shape[1] // PEER_SUB
    n_lb = tm // LANE
    assert n_k == n_lb == vt_ref.shape[1] // PEER_SUB and n_i * PEER_NKEYS == u_ref.shape[0]
    n_mt = u_ref.shape[0] // PEER_SUB

    @pl.when(g == 0)
    def _():
        acc_ref[...] = jnp.zeros_like(acc_ref)
        a1[...] = jnp.zeros_like(a1)
        w0[...] = jnp.zeros_like(w0)

    for hd in range(PEER_HEADS):
        betab = beta_ref[hd]
        e1b = e1_ref[hd]
        for ii in range(n_i):
            row_scr[hd, ii, 0, :, 0:tm] = jnp.broadcast_to(betab[ii:ii + 1], (8, tm))
            row_scr[hd, ii, 1, :, 0:tm] = jnp.broadcast_to(e1b[ii:ii + 1], (8, tm))

    def run(a_wr, a_rd, w_wr, w_rd):
        n_trips = tm // PEER_SUB
        slabs_per_trip = PEER_NKEYS // 16 // n_trips
        for lb in range(n_lb):
            th, lo = divmod(lb * LANE, PEER_SUB)
            ls = slice(lb * LANE, (lb + 1) * LANE)
            lh = slice(lo, lo + LANE)
            ms = slice(lb * PEER_SUB, (lb + 1) * PEER_SUB)

            def trip(tr, _):
                t0 = pl.multiple_of(tr * PEER_SUB, PEER_SUB)
                part_a = lax.dot_general(u_ref[ms, :], h_ref[pl.ds(t0, PEER_SUB), :], nt_dims,
                                         preferred_element_type=F32)
                part_v = jnp.dot(vt_ref[ms, :], w_rd[tr], preferred_element_type=F32)
                for sl in range(slabs_per_trip):
                    j0 = pl.multiple_of((tr * slabs_per_trip + sl) * 16, 16)
                    gs = [[jnp.zeros((8, LANE), F32), jnp.zeros((8, LANE), F32)] for _ in range(n_i)]
                    for hd in range(PEER_HEADS):
                        s2v = s2_ref[hd, pl.ds(j0, 16), ls]
                        e2v = e2_ref[hd, pl.ds(j0, 16), ls]
                        for ii in range(n_i):
                            bs = row_scr[hd, ii, 0, :, ls]
                            be = row_scr[hd, ii, 1, :, ls]
                            for hf in range(2):
                                hs = slice(hf * 8, (hf + 1) * 8)
                                gs[ii][hf] = gs[ii][hf] + jnp.where(s2v[hs] >= bs, e2v[hs], 0.0) * be
                    for ii in range(n_i):
                        r0 = pl.multiple_of(ii * PEER_NKEYS + j0, 16)
                        g16 = jnp.concatenate(gs[ii], axis=0)
                        w_wr[th, pl.ds(r0, 16), lh] = (_gelu(a_rd[th, pl.ds(r0, 16), lh]) * g16).astype(BF16)
                a_wr[tr, ms, :] = part_a
                acc_ref[tr, ms, :] += part_v
                return 0

            lax.fori_loop(0, n_trips, trip, 0, unroll=True)

    @pl.when(g % 2 == 0)
    def _():
        run(a0, a1, w1, w0)

    @pl.when(g % 2 == 1)
    def _():
        run(a1, a0, w0, w1)

    @pl.when((g >= 2) & ((g - 2) % n_chunks == n_chunks - 1))
    def _():
        out = x_ref[...] + mod_ref[0, 5:6, :] * jnp.concatenate(
            [acc_ref[t].T for t in range(acc_ref.shape[0])], axis=0)
        if final_norm:
            ms = jnp.mean(out * out, axis=-1, keepdims=True)
            out = out * lax.rsqrt(ms + EPS) * fg_ref[...]
        o_ref[...] = out
        acc_ref[...] = jnp.zeros_like(acc_ref)


def _peer_dense(h, x, mods, u, vt, beta, s2, e1, e2, fg, tiles_per_batch, nb, n_rows, final_norm):
    d = x.shape[1]
    tm = PEER_TOK_TILE
    n_i = PEER_CHUNK_KEYS
    ne = n_i * PEER_NKEYS
    tiles_per_batch = tiles_per_batch * ROW_TILE // tm
    n_chunks = u.shape[0] // ne
    n_steps = (n_rows // tm) * n_chunks
    assert n_rows % tm == 0 and d == ne
    assert PEER_NKEYS // 16 == (tm // PEER_SUB) * (ne // PEER_SUB)

    def stage(lag):
        def split(g):
            n = jnp.clip(g - lag, 0, n_steps - 1)
            return n // n_chunks, n % n_chunks
        return split

    act, gate, val = stage(0), stage(1), stage(2)
    seg = lambda i: jnp.minimum(i // tiles_per_batch, nb)
    kern = functools.partial(_peer_dense_kernel, n_chunks=n_chunks, final_norm=final_norm)
    full = (PEER_HEADS, PEER_NKEYS, tm)
    rows = (PEER_HEADS, n_i, tm)
    return pl.pallas_call(
        kern,
        grid=(n_steps + 2,),
        in_specs=[pl.BlockSpec((tm, d), lambda g: (act(g)[0], 0)),
                  pl.BlockSpec((tm, d), lambda g: (val(g)[0], 0)),
                  pl.BlockSpec((1, N_MOD, d), lambda g: (seg(val(g)[0]), 0, 0)),
                  pl.BlockSpec((ne, d), lambda g: (act(g)[1], 0)),
                  pl.BlockSpec((d, ne), lambda g: (0, val(g)[1])),
                  pl.BlockSpec(rows, lambda g: (0, gate(g)[1], gate(g)[0])),
                  pl.BlockSpec(full, lambda g: (0, 0, gate(g)[0])),
                  pl.BlockSpec(rows, lambda g: (0, gate(g)[1], gate(g)[0])),
                  pl.BlockSpec(full, lambda g: (0, 0, gate(g)[0])),
                  pl.BlockSpec((1, d), lambda g: (0, 0))],
        out_specs=pl.BlockSpec((tm, d), lambda g: (val(g)[0], 0)),
        out_shape=jax.ShapeDtypeStruct((n_rows, d), F32),
        scratch_shapes=[pltpu.VMEM((tm // PEER_SUB, d, PEER_SUB), F32),
                        pltpu.VMEM((tm // PEER_SUB, ne, PEER_SUB), F32),
                        pltpu.VMEM((tm // PEER_SUB, ne, PEER_SUB), F32),
                        pltpu.VMEM((tm // PEER_SUB, ne, PEER_SUB), BF16),
                        pltpu.VMEM((tm // PEER_SUB, ne, PEER_SUB), BF16),
                        pltpu.VMEM((PEER_HEADS, n_i, 2, 8, tm + LANE), F32)],
        compiler_params=_cparams(("arbitrary",)),
        name="peer_experts",
    )(h, x, mods, u, vt, beta, s2, e1, e2, fg)


def _rope_tables(l_lat, nb, n_ctx_rows):
    rows = l_lat // GRID_W
    r = jnp.repeat(jnp.arange(rows), GRID_W)
    col = jnp.tile(jnp.arange(GRID_W), rows)
    pos = jnp.stack([r, col], axis=-1).astype(F32)
    nf = ATTN_HD // 4
    inv = 1.0 / (ROPE_BASE ** (jnp.arange(nf, dtype=F32) / nf))
    ang = pos[:, :, None] * inv
    cos, sin = jnp.cos(ang), jnp.sin(ang)
    c64 = jnp.concatenate([cos[:, 0], cos[:, 0], cos[:, 1], cos[:, 1]], axis=-1)
    s64 = jnp.concatenate([-sin[:, 0], sin[:, 0], -sin[:, 1], sin[:, 1]], axis=-1)
    c = jnp.tile(c64, (nb, LANE // ATTN_HD))
    s = jnp.tile(s64, (nb, LANE // ATTN_HD))
    c = jnp.concatenate([c, jnp.ones((n_ctx_rows, LANE), F32)], axis=0)
    s = jnp.concatenate([s, jnp.zeros((n_ctx_rows, LANE), F32)], axis=0)
    return c, s


def kernel(x, c, ctx, c_ctx, w_mod, b_mod, norm1_g, norm2_g, w_in, w_out, lam_q1, lam_k1, lam_q2, lam_k2, subln_g, ssm_lambda_re, ssm_lambda_im, ssm_log_step, ssm_b_re, ssm_b_im, ssm_c_re, ssm_c_im, ssm_d, ssm_w_glu, pool_w, pool_scale, peer_wq, peer_keys, peer_u, peer_v, final_g):
    nb, l_lat, d = x.shape
    l_ctx = ctx.shape[1]
    depth = w_mod.shape[0]
    n_lat = nb * l_lat
    n_all = n_lat + nb * l_ctx
    tiles_per_batch = l_lat // ROW_TILE
    assert l_lat % ROW_TILE == 0 and l_ctx == ROW_TILE and nb + 1 <= 8

    cs = jnp.concatenate([c, c_ctx[None], jnp.zeros((8 - nb - 1, d), F32)], axis=0)
    mod_all = _mod_vectors(cs, w_mod, b_mod)
    cos_t, sin_t = _rope_tables(l_lat, 1, ROW_TILE)
    ssm_mats = jax.vmap(_ssm_matrices)(ssm_lambda_re, ssm_lambda_im, ssm_log_step, ssm_b_re, ssm_b_im,
                                       ssm_c_re, ssm_c_im)
    stream = (x.reshape(n_lat, d), ctx.reshape(nb * l_ctx, d), n_lat // ROW_TILE)

    for l in range(depth):
        last = l == depth - 1
        n_rows = n_lat if last else n_all
        mods = mod_all[l, :nb + 1].reshape(nb + 1, N_MOD, d)
        q, k, v, us, up = _inproj(*stream, n_all, norm1_g[l][None], mods, w_in[l].astype(BF16),
                                  cos_t, sin_t, tiles_per_batch, nb)
        lam_init = 0.8 - 0.6 * math.exp(-0.3 * l)
        lam = (jnp.exp(jnp.sum(lam_q1[l] * lam_k1[l])) - jnp.exp(jnp.sum(lam_q2[l] * lam_k2[l]))
               + lam_init).reshape(1).astype(F32)
        attn = _attention(q, k, v, lam, subln_g[l][None], nb, l_lat, l_ctx, not last,
                          1.0 - lam_init)
        y = _ssm(us, tuple(m[l] for m in ssm_mats), ssm_d[l][None], nb, l_lat, l_ctx, not last)
        w_blk = jax.scipy.linalg.block_diag(*[pool_w[l, gi] for gi in range(len(POOL_WINDOWS))])
        pool = _pool(up, w_blk.astype(BF16), pool_scale[l][None], nb, l_lat, l_ctx, n_rows)
        xs = _mixout(*stream, attn, y, pool, mods, ssm_w_glu[l].astype(BF16), w_out[l].astype(BF16),
                     tiles_per_batch, nb, n_rows)
        h2, beta, s2, e1, e2 = _peer_scores(xs, norm2_g[l][None], mods, peer_wq[l].astype(BF16),
                                               peer_keys[l].astype(BF16), tiles_per_batch, nb, n_rows)
        xs = _peer_dense(h2, xs, mods, peer_u[l].astype(BF16), _transpose_cast(peer_v[l], BF16),
                         beta, s2, e1, e2, final_g[None], tiles_per_batch, nb, n_rows, last)
        stream = (xs, xs, n_rows // ROW_TILE)
    return xs.reshape(nb, l_lat, d)
```

```python
import functools
import math

import numpy as np
import jax
import jax.numpy as jnp
from jax import lax
from jax.experimental import pallas as pl
from jax.experimental.pallas import tpu as pltpu

F32 = jnp.float32
BF16 = jnp.bfloat16

EPS = 1e-6
GRID_W = 64
N_MOD = 6
ATTN_HD = 64
ATTN_VD = 128
ATTN_VROWS = ATTN_VD + 16
LOG2E = float(np.log2(np.e))
ATTN_HEADS = 4
ATTN_W = 512
ROPE_BASE = 10000.0
SSM_W = 256
SSM_GROUP = 16
SSM_GROUPS = 16
SSM_STATE = 64
SSM_CHUNK = 16
POOL_W = 256
POOL_WINDOWS = (2, 4, 8, 16)
POOL_GROUP = 64
POOL_HALO = 8
SSM_OFF = 3 * ATTN_W
POOL_OFF = SSM_OFF + SSM_W
IN_W = POOL_OFF + POOL_W
PEER_HEADS = 8
PEER_NKEYS = 128
PEER_KDIM = 128
PEER_TOPK = 16
NEG_BIG = -3.0e38
SQRT_HALF = float(np.sqrt(0.5).astype(np.float32))

LANE = 128
ROW_TILE = 256
VMEM_LIMIT = 56 * 1024 * 1024


def _cparams(sem):
    return pltpu.CompilerParams(dimension_semantics=sem, vmem_limit_bytes=VMEM_LIMIT)


def _gelu(x):
    return 0.5 * x * (1.0 + lax.erf(x * SQRT_HALF))


def _norm_mod(x, g, shift, scale):
    ms = jnp.mean(x * x, axis=-1, keepdims=True)
    y = x * lax.rsqrt(ms + EPS) * g
    return y * (1.0 + scale) + shift


def _mod_kernel(s_ref, w_ref, b_ref, o_ref):
    s = s_ref[...]
    s = s * jax.nn.sigmoid(s)
    o_ref[0] = jnp.dot(s.astype(BF16), w_ref[0].astype(BF16), preferred_element_type=F32) + b_ref[0]


def _mod_vectors(cs, w_mod, b_mod):
    depth, d, n = w_mod.shape
    tn = 1536
    return pl.pallas_call(
        _mod_kernel,
        grid=(depth, n // tn),
        in_specs=[pl.BlockSpec((8, d), lambda l, j: (0, 0)),
                  pl.BlockSpec((1, d, tn), lambda l, j: (l, 0, j)),
                  pl.BlockSpec((1, 1, tn), lambda l, j: (l, 0, j))],
        out_specs=pl.BlockSpec((1, 8, tn), lambda l, j: (l, 0, j)),
        out_shape=jax.ShapeDtypeStruct((depth, 8, n), F32),
        compiler_params=_cparams(("arbitrary", "arbitrary")),
        name="mod_vectors",
    )(cs, w_mod, b_mod.reshape(depth, 1, n))


def _stream_tile(xa_ref, xb_ref, n_a):
    return jnp.where(pl.program_id(0) < n_a, xa_ref[...], xb_ref[...])


def _stream_specs(xa, xb, n_a, tm):
    d = xa.shape[1]
    return [pl.BlockSpec((tm, d), lambda i: (jnp.minimum(i, n_a - 1), 0)),
            pl.BlockSpec((tm, d), lambda i: (jnp.maximum(i - n_a, 0), 0))]


def _inproj_kernel(xa_ref, xb_ref, g_ref, mod_ref, w_ref, cos_ref, sin_ref,
                   q_ref, k_ref, v_ref, us_ref, up_ref, *, n_a):
    tm = xa_ref.shape[0]
    h = _norm_mod(_stream_tile(xa_ref, xb_ref, n_a), g_ref[...], mod_ref[0, 0:1, :], mod_ref[0, 1:2, :])
    p = jnp.dot(h.astype(BF16), w_ref[...], preferred_element_type=F32)
    c = cos_ref[...]
    s = sin_ref[...]
    lane = lax.broadcasted_iota(jnp.int32, (tm, LANE), 1)
    first = (lane % 32) < 16
    for off, ref, sc in ((0, q_ref, ATTN_HD ** -0.5 * LOG2E), (ATTN_W, k_ref, 1.0)):
        for blk in range(ATTN_W // LANE):
            xb = p[:, off + LANE * blk: off + LANE * (blk + 1)]
            partner = jnp.where(first, pltpu.roll(xb, LANE - 16, 1), pltpu.roll(xb, 16, 1))
            ref[:, LANE * blk: LANE * (blk + 1)] = ((xb * c + partner * s) * sc).astype(BF16)
    vt = p[:, 2 * ATTN_W:SSM_OFF].T.astype(BF16)
    ones_rows = (lax.broadcasted_iota(jnp.int32, (ATTN_VROWS - ATTN_VD, tm), 0) == 0).astype(BF16)
    for hd in range(ATTN_HEADS):
        v_ref[0, hd * ATTN_VROWS:hd * ATTN_VROWS + ATTN_VD, :] = vt[hd * ATTN_VD:(hd + 1) * ATTN_VD]
        v_ref[0, hd * ATTN_VROWS + ATTN_VD:(hd + 1) * ATTN_VROWS, :] = ones_rows
    us_ref[...] = p[:, SSM_OFF:POOL_OFF]
    up_ref[...] = p[:, POOL_OFF:IN_W]


def _seg_map(tiles_per_batch, nb):
    def seg(i):
        return jnp.minimum(i // tiles_per_batch, nb)
    return seg


def _inproj(xa, xb, n_a, t, g, mods, w, cos_t, sin_t, tiles_per_batch, nb):
    d = xa.shape[1]
    tm = ROW_TILE
    seg = _seg_map(tiles_per_batch, nb)
    row = lambda i: (i, 0)
    rope_row = lambda i: (jnp.where(i < nb * tiles_per_batch, i % tiles_per_batch, tiles_per_batch), 0)
    return pl.pallas_call(
        functools.partial(_inproj_kernel, n_a=n_a),
        grid=(t // tm,),
        in_specs=_stream_specs(xa, xb, n_a, tm) + [
                  pl.BlockSpec((1, d), lambda i: (0, 0)),
                  pl.BlockSpec((1, N_MOD, d), lambda i: (seg(i), 0, 0)),
                  pl.BlockSpec((d, IN_W), lambda i: (0, 0)),
                  pl.BlockSpec((tm, LANE), rope_row),
                  pl.BlockSpec((tm, LANE), rope_row)],
        out_specs=[pl.BlockSpec((tm, ATTN_W), row),
                   pl.BlockSpec((tm, ATTN_W), row),
                   pl.BlockSpec((1, ATTN_HEADS * ATTN_VROWS, tm), lambda i: (i, 0, 0)),
                   pl.BlockSpec((tm, SSM_W), row),
                   pl.BlockSpec((tm, POOL_W), row)],
        out_shape=[jax.ShapeDtypeStruct((t, ATTN_W), BF16),
                   jax.ShapeDtypeStruct((t, ATTN_W), BF16),
                   jax.ShapeDtypeStruct((t // tm, ATTN_HEADS * ATTN_VROWS, tm), BF16),
                   jax.ShapeDtypeStruct((t, SSM_W), F32),
                   jax.ShapeDtypeStruct((t, POOL_W), F32)],
        compiler_params=_cparams(("arbitrary",)),
        name="inproj",
    )(xa, xb, g, mods, w, cos_t, sin_t)


def _attn_stages(q, tq):
    lane = lax.broadcasted_iota(jnp.int32, (tq, LANE), 1)
    zero = jnp.zeros_like(q)
    qs = jnp.concatenate([jnp.where(lane < ATTN_HD, q, zero),
                          jnp.where(lane >= ATTN_HD, q, zero)], axis=0)

    def scores(kb):
        return lax.dot_general(kb, qs, (((1,), (1,)), ((), ())), preferred_element_type=F32)

    def softmax(m, s):
        m_new = jnp.maximum(m, jnp.max(s, axis=0, keepdims=True))
        return m_new, jnp.exp2(m - m_new), jnp.exp2(s - m_new).astype(BF16)

    def weighted(acc, alpha, pb, vts):
        acc = alpha * acc
        rows = pb.shape[0] // len(vts)
        for c, vt in enumerate(vts):
            acc = acc + jnp.dot(vt, pb[c * rows:(c + 1) * rows], preferred_element_type=F32)
        return acc

    return scores, softmax, weighted


def _attn_finish(lam, acc, g, tq, out_scale):
    o = acc[:ATTN_VD] / acc[ATTN_VD:ATTN_VD + 1]
    o = (o[:, :tq] - lam * o[:, tq:]).T
    ms = jnp.mean(o * o, axis=-1, keepdims=True)
    return (o * lax.rsqrt(ms + EPS) * g * out_scale).astype(BF16)


def _attn_ctx_kernel(lam_ref, q_ref, kc_ref, vc_ref, g_ref, o_ref, *, out_scale):
    tq = q_ref.shape[0]
    scores, softmax, weighted = _attn_stages(q_ref[...], tq)
    m = jnp.full((1, 2 * tq), NEG_BIG, F32)
    acc = jnp.zeros((ATTN_VROWS, 2 * tq), F32)
    m, alpha, pb = softmax(m, scores(kc_ref[...]))
    acc = weighted(acc, alpha, pb, [vc_ref[0]])
    o_ref[...] = _attn_finish(lam_ref[0], acc, g_ref[...], tq, out_scale)


def _attn_kernel(lam_ref, q_ref, kl_ref, kc_ref, vl_ref, vc_ref, g_ref, o_ref,
                 s_a, s_b, p_a, p_b, acc_ref, *, n_lat_k, tk, out_scale):
    tq = q_ref.shape[0]
    scores, softmax, weighted = _attn_stages(q_ref[...], tq)
    n_sub = tk // vc_ref.shape[-1]

    def stage_scores(t, s_buf):
        start = pl.multiple_of(t * tk, tk)
        s_buf[...] = scores(kl_ref[pl.ds(start, tk), :])

    def stage_softmax(m, s_buf, p_buf):
        m, alpha, pb = softmax(m, s_buf[...])
        p_buf[...] = pb
        return m, alpha

    def stage_values(alpha, p_buf, t):
        acc_ref[...] = weighted(acc_ref[...], alpha, p_buf[...],
                                [vl_ref[t * n_sub + c] for c in range(n_sub)])

    m = jnp.full((1, 2 * tq), NEG_BIG, F32)
    m, alpha, pb = softmax(m, scores(kc_ref[...]))
    acc_ref[...] = weighted(jnp.zeros((ATTN_VROWS, 2 * tq), F32), alpha, pb, [vc_ref[0]])

    if n_lat_k % 2 == 0:
        stage_scores(0, s_a)
        stage_scores(1, s_b)
        m, alpha = stage_softmax(m, s_a, p_a)

        def pair(i, carry):
            m, alpha = carry
            k = 2 * i
            stage_scores(k + 2, s_a)
            m, alpha_n = stage_softmax(m, s_b, p_b)
            stage_values(alpha, p_a, k)
            stage_scores(k + 3, s_b)
            m, alpha_nn = stage_softmax(m, s_a, p_a)
            stage_values(alpha_n, p_b, k + 1)
            return m, alpha_nn

        m, alpha = lax.fori_loop(0, n_lat_k // 2 - 1, pair, (m, alpha))
        m, alpha_n = stage_softmax(m, s_b, p_b)
        stage_values(alpha, p_a, n_lat_k - 2)
        stage_values(alpha_n, p_b, n_lat_k - 1)
    else:
        for t in range(n_lat_k):
            stage_scores(t, s_a)
            m, alpha = stage_softmax(m, s_a, p_a)
            stage_values(alpha, p_a, t)
    o_ref[...] = _attn_finish(lam_ref[0], acc_ref[...], g_ref[...], tq, out_scale)


def _attention(q, k, vt, lam, g, nb, l_lat, l_ctx, with_ctx_queries, out_scale):
    tq_ctx = ROW_TILE
    tq = 512 if l_lat % 512 == 0 else ROW_TILE
    tk = 512 if l_lat % 512 == 0 else l_lat
    vt_tile = vt.shape[-1]
    assert l_ctx == tq_ctx and l_lat % tq == 0 and vt_tile == l_ctx and tk % vt_tile == 0
    n_lat_q = l_lat // tq
    n_vt_lat = l_lat // vt_tile
    ctx_blk0 = nb * l_lat // l_ctx
    smem = pl.BlockSpec(memory_space=pltpu.SMEM)
    gain = pl.BlockSpec((1, LANE), lambda b, h, i: (0, 0))
    kc_spec = pl.BlockSpec((l_ctx, LANE), lambda b, h, i: (ctx_blk0 + b, h))
    vc_spec = pl.BlockSpec((1, ATTN_VROWS, vt_tile), lambda b, h, i: (nb * n_vt_lat + b, h, 0))
    sem = _cparams(("arbitrary", "arbitrary", "arbitrary"))
    lat = pl.pallas_call(
        functools.partial(_attn_kernel, n_lat_k=l_lat // tk, tk=tk, out_scale=out_scale),
        grid=(nb, ATTN_HEADS, n_lat_q),
        in_specs=[smem,
                  pl.BlockSpec((tq, LANE), lambda b, h, i: (b * n_lat_q + i, h)),
                  pl.BlockSpec((l_lat, LANE), lambda b, h, i: (b, h)),
                  kc_spec,
                  pl.BlockSpec((n_vt_lat, ATTN_VROWS, vt_tile), lambda b, h, i: (b, h, 0)),
                  vc_spec, gain],
        out_specs=pl.BlockSpec((tq, LANE), lambda b, h, i: (b * n_lat_q + i, h)),
        out_shape=jax.ShapeDtypeStruct((nb * l_lat, ATTN_W), BF16),
        scratch_shapes=[pltpu.VMEM((tk, 2 * tq), F32), pltpu.VMEM((tk, 2 * tq), F32),
                        pltpu.VMEM((tk, 2 * tq), BF16), pltpu.VMEM((tk, 2 * tq), BF16),
                        pltpu.VMEM((ATTN_VROWS, 2 * tq), F32)],
        compiler_params=sem,
        name="diff_attention",
    )(lam, q, k, k, vt, vt, g)
    if not with_ctx_queries:
        return lat
    ctx = pl.pallas_call(
        functools.partial(_attn_ctx_kernel, out_scale=out_scale),
        grid=(nb, ATTN_HEADS, 1),
        in_specs=[smem,
                  pl.BlockSpec((tq_ctx, LANE), lambda b, h, i: (ctx_blk0 + b, h)),
                  kc_spec, vc_spec, gain],
        out_specs=pl.BlockSpec((tq_ctx, LANE), lambda b, h, i: (b, h)),
        out_shape=jax.ShapeDtypeStruct((nb * l_ctx, ATTN_W), BF16),
        compiler_params=sem,
        name="diff_attention_ctx",
    )(lam, q, k, vt, g)
    return jnp.concatenate([lat, ctx], axis=0)


def _ssm_kernel(ul_ref, uc_ref, m_ref, bm_ref, cm_ref, a_ref, d_ref, yl_ref, yc_ref,
                s_scr, h_scr, *, ctx_out):
    tc, pw = SSM_CHUNK, 2 * SSM_GROUP
    nl = ul_ref.shape[0] // tc
    nc = uc_ref.shape[0] // tc
    w = m_ref.shape[-1]
    half = s_scr.shape[-1] // 2
    sub = pl.program_id(1) % (LANE // pw)
    nt_dims = (((1,), (1,)), ((), ()))
    ch = lax.broadcasted_iota(jnp.int32, (LANE, 1), 0) - sub * pw
    valid = (ch >= 0) & (ch < pw)
    tgt0 = (ch >> 4) * (tc * SSM_GROUP) + (ch & (SSM_GROUP - 1))
    col = lax.broadcasted_iota(jnp.int32, (1, w), 1)

    def perm(t):
        return jnp.where(valid & (col == tgt0 + t * SSM_GROUP), 1.0, 0.0).astype(BF16)

    def fold(ref, n):
        acc = jnp.zeros((n, w), F32)
        for t in range(tc):
            acc = acc + jnp.dot(ref[pl.ds(t, n, stride=tc), :].astype(BF16), perm(t),
                                preferred_element_type=F32)
        return acc.astype(BF16)

    def unfold_add(y, ref, n):
        hi = y.astype(BF16)
        r1 = y - hi.astype(F32)
        mid = r1.astype(BF16)
        lo = (r1 - mid.astype(F32)).astype(BF16)
        for t in range(tc):
            p = perm(t)
            o = sum(lax.dot_general(part, p, nt_dims, preferred_element_type=F32)
                    for part in (hi, mid, lo))
            ref[pl.ds(t, n, stride=tc), :] += o

    @pl.when(sub == 0)
    def _():
        yl_ref[...] = ul_ref[...] * d_ref[...]
        yc_ref[...] = uc_ref[...] * d_ref[...] if ctx_out else jnp.zeros_like(yc_ref)

    ulb = fold(ul_ref, nl)
    ucb = fold(uc_ref, nc)
    yl = jnp.dot(ulb, m_ref[0], preferred_element_type=F32)
    if ctx_out:
        yc = jnp.dot(ucb, m_ref[0], preferred_element_type=F32)
    for dr in range(2):
        s_scr[dr, 0:nc, :] = jnp.dot(ucb, bm_ref[dr, 0], preferred_element_type=F32)
        s_scr[dr, nc:nc + nl, :] = jnp.dot(ulb, bm_ref[dr, 0], preferred_element_type=F32)
    a = a_ref[0]
    afr, afi, arr, ari = a[0:1], a[1:2], a[2:3], a[3:4]

    def make_body(base, n):
        def body(t, carry):
            fr, fi, rr, ri = carry
            rf = base + t
            rv = base + n - 1 - t
            h_scr[0, pl.ds(rf, 1), :] = jnp.concatenate([fr, fi], axis=-1)
            h_scr[1, pl.ds(rv, 1), :] = jnp.concatenate([rr, ri], axis=-1)
            sf = s_scr[0, pl.ds(rf, 1), :]
            sv = s_scr[1, pl.ds(rv, 1), :]
            nfr = afr * fr - afi * fi + sf[:, :half]
            nfi = afr * fi + afi * fr + sf[:, half:]
            nrr = arr * rr - ari * ri + sv[:, :half]
            nri = arr * ri + ari * rr + sv[:, half:]
            return nfr, nfi, nrr, nri
        return body

    z = jnp.zeros((1, half), F32)
    carry = lax.fori_loop(0, nc, make_body(0, nc), (z, z, z, z))
    lax.fori_loop(0, nl, make_body(nc, nl), carry)
    for dr in range(2):
        yl = yl + jnp.dot(h_scr[dr, nc:nc + nl, :].astype(BF16), cm_ref[dr, 0],
                          preferred_element_type=F32)
        if ctx_out:
            yc = yc + jnp.dot(h_scr[dr, 0:nc, :].astype(BF16), cm_ref[dr, 0],
                              preferred_element_type=F32)
    unfold_add(yl, yl_ref, nl)
    if ctx_out:
        unfold_add(yc, yc_ref, nc)


def _ssm_matrices(lam_re, lam_im, log_step, b_re, b_im, c_re, c_im):
    tc = SSM_CHUNK
    g, p, hh = SSM_GROUPS, SSM_STATE, SSM_GROUP
    npair = g // 2
    step = jnp.exp(log_step)[..., None]
    den = lam_re * lam_re + lam_im * lam_im

    def power(k):
        er = jnp.exp(lam_re * step * k)
        return er * jnp.cos(lam_im * step * k), er * jnp.sin(lam_im * step * k)

    ar, ai = power(1.0)
    nr = ar - 1.0
    cr_ = (nr * lam_re + ai * lam_im) / den
    ci_ = (ai * lam_re - nr * lam_im) / den
    bbr = cr_[..., None] * b_re - ci_[..., None] * b_im
    bbi = cr_[..., None] * b_im + ci_[..., None] * b_re
    ks = jnp.arange(tc + 1, dtype=F32)
    pw = jax.vmap(power)(ks)
    pwr, pwi = pw
    cpr = c_re[None] * pwr[:, :, :, None, :] - c_im[None] * pwi[:, :, :, None, :]
    cpi = c_re[None] * pwi[:, :, :, None, :] + c_im[None] * pwr[:, :, :, None, :]
    hp = lax.Precision.HIGHEST
    kk = (jnp.einsum('kdgop,dgpi->kdgoi', cpr[:tc], bbr, precision=hp)
          - jnp.einsum('kdgop,dgpi->kdgoi', cpi[:tc], bbi, precision=hp))
    zpad = jnp.zeros((tc - 1,) + kk.shape[2:], F32)
    lagtab = (jnp.concatenate([zpad, kk[:, 0]], 0)
              + jnp.concatenate([kk[::-1, 1], zpad], 0))
    lagtab = jnp.transpose(lagtab, (1, 3, 0, 2)).reshape(g, hh, (2 * tc - 1) * hh)
    mm = jnp.stack([lagtab[:, :, (tc - 1 - s) * hh:(2 * tc - 1 - s) * hh] for s in range(tc)], axis=1)
    mm = mm.reshape(g, tc * hh, tc * hh)
    zero = jnp.zeros_like(mm[0::2])
    m2 = jnp.concatenate([jnp.concatenate([mm[0::2], zero], -1),
                          jnp.concatenate([zero, mm[1::2]], -1)], -2)
    def bmat(dr, exps):
        pr = pwr[exps, dr]
        pi = pwi[exps, dr]
        re = pr[..., None] * bbr[dr][None] - pi[..., None] * bbi[dr][None]
        im = pr[..., None] * bbi[dr][None] + pi[..., None] * bbr[dr][None]
        re = jnp.transpose(re, (1, 0, 3, 2)).reshape(g, tc * hh, p)
        im = jnp.transpose(im, (1, 0, 3, 2)).reshape(g, tc * hh, p)
        z = jnp.zeros_like(re[0::2])
        top = jnp.concatenate([re[0::2], z, im[0::2], z], -1)
        bot = jnp.concatenate([z, re[1::2], z, im[1::2]], -1)
        return jnp.concatenate([top, bot], -2)
    bm = jnp.stack([bmat(0, jnp.arange(tc - 1, -1, -1)), bmat(1, jnp.arange(tc))])
    def cmat(dr, exps):
        re = jnp.transpose(cpr[exps, dr], (1, 3, 0, 2)).reshape(g, p, tc * hh)
        im = jnp.transpose(cpi[exps, dr], (1, 3, 0, 2)).reshape(g, p, tc * hh)
        z = jnp.zeros_like(re[0::2])
        return jnp.concatenate([jnp.concatenate([re[0::2], z], -1),
                                jnp.concatenate([z, re[1::2]], -1),
                                jnp.concatenate([-im[0::2], z], -1),
                                jnp.concatenate([z, -im[1::2]], -1)], -2)
    cm = jnp.stack([cmat(0, jnp.arange(1, tc + 1)), cmat(1, jnp.arange(tc, 0, -1))])
    a16r = pwr[tc].reshape(2, npair, 2 * p)
    a16i = pwi[tc].reshape(2, npair, 2 * p)
    a16 = jnp.stack([a16r[0], a16i[0], a16r[1], a16i[1]], axis=1)
    return m2.astype(BF16), bm.astype(BF16), cm.astype(BF16), a16


def _ssm(us, mats, d, nb, l_lat, l_ctx, ctx_out):
    m2, bm, cm, a16 = mats
    tc, hh = SSM_CHUNK, SSM_GROUP
    npair = SSM_GROUPS // 2
    w = 2 * tc * hh
    per_col = LANE // (2 * hh)
    nl, nc = l_lat // tc, l_ctx // tc
    ctx_blk0 = nb * l_lat // l_ctx
    yl, yc = pl.pallas_call(
        functools.partial(_ssm_kernel, ctx_out=ctx_out),
        grid=(nb, npair),
        in_specs=[pl.BlockSpec((l_lat, LANE), lambda b, q: (b, q // per_col)),
                  pl.BlockSpec((l_ctx, LANE), lambda b, q: (ctx_blk0 + b, q // per_col)),
                  pl.BlockSpec((1, w, w), lambda b, q: (q, 0, 0)),
                  pl.BlockSpec((2, 1, w, 4 * SSM_STATE), lambda b, q: (0, q, 0, 0)),
                  pl.BlockSpec((2, 1, 4 * SSM_STATE, w), lambda b, q: (0, q, 0, 0)),
                  pl.BlockSpec((1, 4, 2 * SSM_STATE), lambda b, q: (q, 0, 0)),
                  pl.BlockSpec((1, LANE), lambda b, q: (0, q // per_col))],
        out_specs=[pl.BlockSpec((l_lat, LANE), lambda b, q: (b, q // per_col)),
                   pl.BlockSpec((l_ctx, LANE), lambda b, q: (b, q // per_col))],
        out_shape=[jax.ShapeDtypeStruct((nb * l_lat, SSM_W), F32),
                   jax.ShapeDtypeStruct((nb * l_ctx, SSM_W), F32)],
        scratch_shapes=[pltpu.VMEM((2, nc + nl, 4 * SSM_STATE), F32),
                        pltpu.VMEM((2, nc + nl, 4 * SSM_STATE), F32)],
        compiler_params=_cparams(("arbitrary", "arbitrary")),
        name="s5_scan",
    )(us, us, m2, bm, cm, a16, d)
    return jnp.concatenate([yl, yc], axis=0) if ctx_out else yl


def _pool_kernel(prev_ref, cur_ref, next_ref, w_ref, scale_ref, o_ref, ext,
                 *, n_lat_tiles, tiles_per_lat, l_lat, l_ctx):
    r = cur_ref.shape[0]
    hl = POOL_HALO
    i = pl.program_id(0)
    is_lat = i < n_lat_tiles
    pos = jnp.where(is_lat, i % tiles_per_lat, 0)
    n_tiles = jnp.where(is_lat, tiles_per_lat, l_ctx // r)
    seq_len = jnp.where(is_lat, l_lat, l_ctx)
    zeros = jnp.zeros((hl, POOL_W), F32)
    ext[0:hl, :] = jnp.where(pos > 0, prev_ref[...], zeros)
    ext[hl:hl + r, :] = cur_ref[...]
    ext[hl + r:hl + r + hl, :] = jnp.where(pos < n_tiles - 1, next_ref[...], zeros)
    u = cur_ref[...]
    t = pos * r + lax.broadcasted_iota(jnp.int32, (r, 1), 0)
    lane = lax.broadcasted_iota(jnp.int32, (r, POOL_W), 1)

    def win(k):
        return ext[hl + k:hl + k + r, :]

    acc = win(-1) + u
    mean = jnp.zeros((r, POOL_W), F32)
    lo_k, hi_k = -1, 0
    for gi, wn in enumerate(POOL_WINDOWS):
        hw = wn // 2
        while lo_k > -hw:
            lo_k -= 1
            acc = acc + win(lo_k)
        while hi_k < hw - 1:
            hi_k += 1
            acc = acc + win(hi_k)
        cnt = (jnp.minimum(t + hw, seq_len) - jnp.maximum(t - hw, 0)).astype(F32)
        sel = (lane >= gi * POOL_GROUP) & (lane < (gi + 1) * POOL_GROUP)
        mean = jnp.where(sel, acc / cnt, mean)
    dlt = (mean - u).astype(BF16)
    o_ref[...] = jnp.dot(dlt, w_ref[...], preferred_element_type=F32) * scale_ref[...]


def _pool(up, w_blk, scale, nb, l_lat, l_ctx, n_rows):
    r = ROW_TILE
    hl = POOL_HALO
    n_tiles = n_rows // r
    last8 = up.shape[0] // hl - 1
    kern = functools.partial(_pool_kernel, n_lat_tiles=nb * l_lat // r, tiles_per_lat=l_lat // r,
                             l_lat=l_lat, l_ctx=l_ctx)
    return pl.pallas_call(
        kern,
        grid=(n_tiles,),
        in_specs=[pl.BlockSpec((hl, POOL_W), lambda i: (jnp.maximum(i * (r // hl) - 1, 0), 0)),
                  pl.BlockSpec((r, POOL_W), lambda i: (i, 0)),
                  pl.BlockSpec((hl, POOL_W), lambda i: (jnp.minimum((i + 1) * (r // hl), last8), 0)),
                  pl.BlockSpec((POOL_W, POOL_W), lambda i: (0, 0)),
                  pl.BlockSpec((1, POOL_W), lambda i: (0, 0))],
        out_specs=pl.BlockSpec((r, POOL_W), lambda i: (i, 0)),
        out_shape=jax.ShapeDtypeStruct((n_rows, POOL_W), F32),
        scratch_shapes=[pltpu.VMEM((r + 2 * hl, POOL_W), F32)],
        compiler_params=_cparams(("arbitrary",)),
        name="pool_mix",
    )(up, up, up, w_blk, scale)


def _mixout_kernel(xa_ref, xb_ref, attn_ref, y_ref, pool_ref, mod_ref, wglu_ref, wout_ref, o_ref,
                   *, n_a):
    yg = _gelu(y_ref[...])
    z = yg * jax.nn.sigmoid(jnp.dot(yg.astype(BF16), wglu_ref[...], preferred_element_type=F32))
    r = jnp.dot(attn_ref[...], wout_ref[0:ATTN_W, :], preferred_element_type=F32)
    r = r + jnp.dot(z.astype(BF16), wout_ref[ATTN_W:ATTN_W + SSM_W, :], preferred_element_type=F32)
    r = r + jnp.dot(pool_ref[...].astype(BF16), wout_ref[ATTN_W + SSM_W:, :],
                    preferred_element_type=F32)
    o_ref[...] = _stream_tile(xa_ref, xb_ref, n_a) + mod_ref[0, 2:3, :] * r


def _mixout(xa, xb, n_a, attn, y, pool, mods, wglu, wout, tiles_per_batch, nb, n_rows):
    d = xa.shape[1]
    tm = ROW_TILE
    seg = _seg_map(tiles_per_batch, nb)
    row = lambda i: (i, 0)
    return pl.pallas_call(
        functools.partial(_mixout_kernel, n_a=n_a),
        grid=(n_rows // tm,),
        in_specs=_stream_specs(xa, xb, n_a, tm) + [
                  pl.BlockSpec((tm, ATTN_W), row),
                  pl.BlockSpec((tm, SSM_W), row),
                  pl.BlockSpec((tm, POOL_W), row),
                  pl.BlockSpec((1, N_MOD, d), lambda i: (seg(i), 0, 0)),
                  pl.BlockSpec((SSM_W, SSM_W), lambda i: (0, 0)),
                  pl.BlockSpec((d, d), lambda i: (0, 0))],
        out_specs=pl.BlockSpec((tm, d), row),
        out_shape=jax.ShapeDtypeStruct((n_rows, d), F32),
        compiler_params=_cparams(("arbitrary",)),
        name="mix_out",
    )(xa, xb, attn, y, pool, mods, wglu, wout)


def _oddeven_merge(lo, hi, r):
    step = r * 2
    if step < hi - lo:
        yield from _oddeven_merge(lo, hi, step)
        yield from _oddeven_merge(lo + r, hi, step)
        yield from [(i, i + r) for i in range(lo + r, hi - r, step)]
    else:
        yield (lo, lo + r)


def _oddeven_sort(lo, hi):
    if hi - lo >= 1:
        mid = lo + (hi - lo) // 2
        yield from _oddeven_sort(lo, mid)
        yield from _oddeven_sort(mid + 1, hi)
        yield from _oddeven_merge(lo, hi, 1)


_SORT16 = tuple(_oddeven_sort(0, PEER_TOPK - 1))
_BITONIC16 = tuple((i, i + d) for d in (8, 4, 2, 1) for i in range(PEER_TOPK) if not i & d)


def _compare_exchange(rows, net):
    rows = list(rows)
    for i, j in net:
        hi, lo = jnp.maximum(rows[i], rows[j]), jnp.minimum(rows[i], rows[j])
        rows[i], rows[j] = hi, lo
    return rows


def _top_sorted(s, k):
    assert k == PEER_TOPK and s.shape[0] == 8 * PEER_TOPK
    rows = _compare_exchange([s[8 * v:8 * v + 8] for v in range(PEER_TOPK)], _SORT16)
    for shift in (4, 2, 1):
        other = [pltpu.roll(r, shift, 0) for r in rows]
        rows = _compare_exchange([jnp.maximum(rows[i], other[PEER_TOPK - 1 - i])
                                  for i in range(PEER_TOPK)], _BITONIC16)
    return [r[0:1] for r in rows]


def _peer_score_kernel(x_ref, g_ref, mod_ref, wq_ref, keys_ref,
                       h_ref, beta_ref, s2_ref, e1_ref, e2_ref, q_scr):
    tm = x_ref.shape[0]
    h = _norm_mod(x_ref[...], g_ref[...], mod_ref[0, 3:4, :], mod_ref[0, 4:5, :]).astype(BF16)
    h_ref[...] = h
    q = jnp.dot(h, wq_ref[...], preferred_element_type=F32)
    for hc in range(2 * PEER_HEADS):
        q_scr[hc] = q[:, hc * PEER_KDIM:(hc + 1) * PEER_KDIM].astype(BF16)
    row8 = lax.broadcasted_iota(jnp.int32, (8, tm), 0)

    def head(hd, _):
        nt = (((1,), (1,)), ((), ()))
        s1 = lax.dot_general(keys_ref[hd, 0], q_scr[2 * hd], nt,
                             preferred_element_type=F32)
        s2 = lax.dot_general(keys_ref[hd, 1], q_scr[2 * hd + 1], nt,
                             preferred_element_type=F32)
        a = _top_sorted(s1, PEER_TOPK)
        b = _top_sorted(s2, PEER_TOPK)
        acat = jnp.concatenate(a, axis=0)
        bcat = jnp.concatenate(b, axis=0)
        pieces = [a[0] + bcat]
        for i in range(2, 9):
            piece = a[i - 1] + bcat[0:8]
            n_valid = PEER_TOPK // i
            pieces.append(piece if n_valid >= 8 else jnp.where(row8 < n_valid, piece, NEG_BIG))
        pieces.append(acat[8:16] + b[0])
        cand = jnp.concatenate(pieces, axis=0)
        work = cand
        cum = jnp.zeros((1, tm), F32)
        tau = jnp.full((1, tm), NEG_BIG, F32)
        for _ in range(PEER_TOPK):
            m = jnp.max(work, axis=0, keepdims=True)
            eq = work == m
            new = cum + jnp.sum(eq.astype(F32), axis=0, keepdims=True)
            tau = jnp.where((cum < PEER_TOPK) & (new >= PEER_TOPK), m, tau)
            work = jnp.where(eq, NEG_BIG, work)
            cum = new
        top = a[0] + b[0]
        z = jnp.sum(jnp.where(cand >= tau, jnp.exp(cand - top), 0.0), axis=0, keepdims=True)
        big = -NEG_BIG
        b_rows = [bcat] + [bcat[0:8]] * 7
        beta_r = [jnp.min(jnp.where(pc >= tau, br, big), axis=0, keepdims=True)
                  for pc, br in zip(pieces[:8], b_rows)]
        tail = jnp.where(pieces[8] >= tau, b[0], big)
        beta_r += [tail[r:r + 1] for r in range(8)]
        beta = jnp.full_like(s1, big)
        for r in range(PEER_TOPK):
            beta = jnp.where(s1 == a[r], beta_r[r], beta)
        beta_ref[hd] = beta
        s2_ref[hd] = s2
        e1_ref[hd] = jnp.exp(s1 - a[0])
        e2_ref[hd] = jnp.exp(s2 - b[0]) * (1.0 / z)
        return 0

    lax.fori_loop(0, PEER_HEADS, head, 0, unroll=4)


def _peer_scores(x, g, mods, wq, keys, tiles_per_batch, nb, n_rows):
    d = x.shape[1]
    tm = ROW_TILE
    seg = _seg_map(tiles_per_batch, nb)
    nq = wq.shape[1]
    col = lambda i: (0, 0, i)
    big = jax.ShapeDtypeStruct((PEER_HEADS, PEER_NKEYS, n_rows), F32)
    return pl.pallas_call(
        _peer_score_kernel,
        grid=(n_rows // tm,),
        in_specs=[pl.BlockSpec((tm, d), lambda i: (i, 0)),
                  pl.BlockSpec((1, d), lambda i: (0, 0)),
                  pl.BlockSpec((1, N_MOD, d), lambda i: (seg(i), 0, 0)),
                  pl.BlockSpec((d, nq), lambda i: (0, 0)),
                  pl.BlockSpec((PEER_HEADS, 2, PEER_NKEYS, PEER_KDIM), lambda i: (0, 0, 0, 0))],
        out_specs=[pl.BlockSpec((tm, d), lambda i: (i, 0)),
                   pl.BlockSpec((PEER_HEADS, PEER_NKEYS, tm), col),
                   pl.BlockSpec((PEER_HEADS, PEER_NKEYS, tm), col),
                   pl.BlockSpec((PEER_HEADS, PEER_NKEYS, tm), col),
                   pl.BlockSpec((PEER_HEADS, PEER_NKEYS, tm), col)],
        out_shape=[jax.ShapeDtypeStruct((n_rows, d), BF16), big, big, big, big],
        scratch_shapes=[pltpu.VMEM((2 * PEER_HEADS, tm, PEER_KDIM), BF16)],
        compiler_params=_cparams(("arbitrary",)),
        name="peer_scores",
    )(x, g, mods, wq, keys)


def _transpose_cast_kernel(x_ref, o_ref):
    o_ref[...] = x_ref[...].T.astype(o_ref.dtype)


def _transpose_cast(x, dtype, rows_per_step=512):
    n, d = x.shape
    return pl.pallas_call(
        _transpose_cast_kernel,
        grid=(n // rows_per_step,),
        in_specs=[pl.BlockSpec((rows_per_step, d), lambda i: (i, 0))],
        out_specs=pl.BlockSpec((d, rows_per_step), lambda i: (0, i)),
        out_shape=jax.ShapeDtypeStruct((d, n), dtype),
        compiler_params=_cparams(("arbitrary",)),
        name="transpose_cast",
    )(x)


PEER_CHUNK_KEYS = 8
PEER_TOK_TILE = 512
PEER_SUB = 256


def _peer_dense_kernel(h_ref, x_ref, mod_ref, u_ref, vt_ref, beta_ref, s2_ref, e1_ref, e2_ref,
                       fg_ref, o_ref, acc_ref, a0, a1, w0, w1, row_scr,
                       *, n_chunks, final_norm):
    n_i = PEER_CHUNK_KEYS
    tm = h_ref.shape[0]
    g = pl.program_id(0)
    nt_dims = (((1,), (1,)), ((), ()))
    n_k = u_ref.shape[1] // PEER_SUB
    n_lb = tm // LANE
    assert n_k == n_lb == vt_ref.shape[1] // PEER_SUB and n_i * PEER_NKEYS == u_ref.shape[0]
    n_mt = u_ref.shape[0] // PEER_SUB

    @pl.when(g == 0)
    def _():
        acc_ref[...] = jnp.zeros_like(acc_ref)
        a1[...] = jnp.zeros_like(a1)
        w0[...] = jnp.zeros_like(w0)

    for hd in range(PEER_HEADS):
        betab = beta_ref[hd]
        e1b = e1_ref[hd]
        for ii in range(n_i):
            row_scr[hd, ii, 0, :, 0:tm] = jnp.broadcast_to(betab[ii:ii + 1], (8, tm))
            row_scr[hd, ii, 1, :, 0:tm] = jnp.broadcast_to(e1b[ii:ii + 1], (8, tm))

    def run(a_wr, a_rd, w_wr, w_rd):
        n_trips = tm // PEER_SUB
        slabs_per_trip = PEER_NKEYS // 16 // n_trips
        for lb in range(n_lb):
            th, lo = divmod(lb * LANE, PEER_SUB)
            ls = slice(lb * LANE, (lb + 1) * LANE)
            lh = slice(lo, lo + LANE)
            ms = slice(lb * PEER_SUB, (lb + 1) * PEER_SUB)

            def trip(tr, _):
                t0 = pl.multiple_of(tr * PEER_SUB, PEER_SUB)
                part_a = lax.dot_general(u_ref[ms, :], h_ref[pl.ds(t0, PEER_SUB), :], nt_dims,
                                         preferred_element_type=F32)
                part_v = jnp.dot(vt_ref[ms, :], w_rd[tr], preferred_element_type=F32)
                for sl in range(slabs_per_trip):
                    j0 = pl.multiple_of((tr * slabs_per_trip + sl) * 16, 16)
                    gs = [[jnp.zeros((8, LANE), F32), jnp.zeros((8, LANE), F32)] for _ in range(n_i)]
                    for hd in range(PEER_HEADS):
                        s2v = s2_ref[hd, pl.ds(j0, 16), ls]
                        e2v = e2_ref[hd, pl.ds(j0, 16), ls]
                        for ii in range(n_i):
                            bs = row_scr[hd, ii, 0, :, ls]
                            be = row_scr[hd, ii, 1, :, ls]
                            for hf in range(2):
                                hs = slice(hf * 8, (hf + 1) * 8)
                                gs[ii][hf] = gs[ii][hf] + jnp.where(s2v[hs] >= bs, e2v[hs], 0.0) * be
                    for ii in range(n_i):
                        r0 = pl.multiple_of(ii * PEER_NKEYS + j0, 16)
                        g16 = jnp.concatenate(gs[ii], axis=0)
                        w_wr[th, pl.ds(r0, 16), lh] = (_gelu(a_rd[th, pl.ds(r0, 16), lh]) * g16).astype(BF16)
                a_wr[tr, ms, :] = part_a
                acc_ref[tr, ms, :] += part_v
                return 0

            lax.fori_loop(0, n_trips, trip, 0, unroll=True)

    @pl.when(g % 2 == 0)
    def _():
        run(a0, a1, w1, w0)

    @pl.when(g % 2 == 1)
    def _():
        run(a1, a0, w0, w1)

    @pl.when((g >= 2) & ((g - 2) % n_chunks == n_chunks - 1))
    def _():
        out = x_ref[...] + mod_ref[0, 5:6, :] * jnp.concatenate(
            [acc_ref[t].T for t in range(acc_ref.shape[0])], axis=0)
        if final_norm:
            ms = jnp.mean(out * out, axis=-1, keepdims=True)
            out = out * lax.rsqrt(ms + EPS) * fg_ref[...]
        o_ref[...] = out
        acc_ref[...] = jnp.zeros_like(acc_ref)


def _peer_dense(h, x, mods, u, vt, beta, s2, e1, e2, fg, tiles_per_batch, nb, n_rows, final_norm):
    d = x.shape[1]
    tm = PEER_TOK_TILE
    n_i = PEER_CHUNK_KEYS
    ne = n_i * PEER_NKEYS
    tiles_per_batch = tiles_per_batch * ROW_TILE // tm
    n_chunks = u.shape[0] // ne
    n_steps = (n_rows // tm) * n_chunks
    assert n_rows % tm == 0 and d == ne
    assert PEER_NKEYS // 16 == (tm // PEER_SUB) * (ne // PEER_SUB)

    def stage(lag):
        def split(g):
            n = jnp.clip(g - lag, 0, n_steps - 1)
            return n // n_chunks, n % n_chunks
        return split

    act, gate, val = stage(0), stage(1), stage(2)
    seg = lambda i: jnp.minimum(i // tiles_per_batch, nb)
    kern = functools.partial(_peer_dense_kernel, n_chunks=n_chunks, final_norm=final_norm)
    full = (PEER_HEADS, PEER_NKEYS, tm)
    rows = (PEER_HEADS, n_i, tm)
    return pl.pallas_call(
        kern,
        grid=(n_steps + 2,),
        in_specs=[pl.BlockSpec((tm, d), lambda g: (act(g)[0], 0)),
                  pl.BlockSpec((tm, d), lambda g: (val(g)[0], 0)),
                  pl.BlockSpec((1, N_MOD, d), lambda g: (seg(val(g)[0]), 0, 0)),
                  pl.BlockSpec((ne, d), lambda g: (act(g)[1], 0)),
                  pl.BlockSpec((d, ne), lambda g: (0, val(g)[1])),
                  pl.BlockSpec(rows, lambda g: (0, gate(g)[1], gate(g)[0])),
                  pl.BlockSpec(full, lambda g: (0, 0, gate(g)[0])),
                  pl.BlockSpec(rows, lambda g: (0, gate(g)[1], gate(g)[0])),
                  pl.BlockSpec(full, lambda g: (0, 0, gate(g)[0])),
                  pl.BlockSpec((1, d), lambda g: (0, 0))],
        out_specs=pl.BlockSpec((tm, d), lambda g: (val(g)[0], 0)),
        out_shape=jax.ShapeDtypeStruct((n_rows, d), F32),
        scratch_shapes=[pltpu.VMEM((tm // PEER_SUB, d, PEER_SUB), F32),
                        pltpu.VMEM((tm // PEER_SUB, ne, PEER_SUB), F32),
                        pltpu.VMEM((tm // PEER_SUB, ne, PEER_SUB), F32),
                        pltpu.VMEM((tm // PEER_SUB, ne, PEER_SUB), BF16),
                        pltpu.VMEM((tm // PEER_SUB, ne, PEER_SUB), BF16),
                        pltpu.VMEM((PEER_HEADS, n_i, 2, 8, tm + LANE), F32)],
        compiler_params=_cparams(("arbitrary",)),
        name="peer_experts",
    )(h, x, mods, u, vt, beta, s2, e1, e2, fg)


def _rope_tables(l_lat, nb, n_ctx_rows):
    rows = l_lat // GRID_W
    r = jnp.repeat(jnp.arange(rows), GRID_W)
    col = jnp.tile(jnp.arange(GRID_W), rows)
    pos = jnp.stack([r, col], axis=-1).astype(F32)
    nf = ATTN_HD // 4
    inv = 1.0 / (ROPE_BASE ** (jnp.arange(nf, dtype=F32) / nf))
    ang = pos[:, :, None] * inv
    cos, sin = jnp.cos(ang), jnp.sin(ang)
    c64 = jnp.concatenate([cos[:, 0], cos[:, 0], cos[:, 1], cos[:, 1]], axis=-1)
    s64 = jnp.concatenate([-sin[:, 0], sin[:, 0], -sin[:, 1], sin[:, 1]], axis=-1)
    c = jnp.tile(c64, (nb, LANE // ATTN_HD))
    s = jnp.tile(s64, (nb, LANE // ATTN_HD))
    c = jnp.concatenate([c, jnp.ones((n_ctx_rows, LANE), F32)], axis=0)
    s = jnp.concatenate([s, jnp.zeros((n_ctx_rows, LANE), F32)], axis=0)
    return c, s


def kernel(x, c, ctx, c_ctx, w_mod, b_mod, norm1_g, norm2_g, w_in, w_out, lam_q1, lam_k1, lam_q2, lam_k2, subln_g, ssm_lambda_re, ssm_lambda_im, ssm_log_step, ssm_b_re, ssm_b_im, ssm_c_re, ssm_c_im, ssm_d, ssm_w_glu, pool_w, pool_scale, peer_wq, peer_keys, peer_u, peer_v, final_g):
    nb, l_lat, d = x.shape
    l_ctx = ctx.shape[1]
    depth = w_mod.shape[0]
    n_lat = nb * l_lat
    n_all = n_lat + nb * l_ctx
    tiles_per_batch = l_lat // ROW_TILE
    assert l_lat % ROW_TILE == 0 and l_ctx == ROW_TILE and nb + 1 <= 8

    cs = jnp.concatenate([c, c_ctx[None], jnp.zeros((8 - nb - 1, d), F32)], axis=0)
    mod_all = _mod_vectors(cs, w_mod, b_mod)
    cos_t, sin_t = _rope_tables(l_lat, 1, ROW_TILE)
    ssm_mats = jax.vmap(_ssm_matrices)(ssm_lambda_re, ssm_lambda_im, ssm_log_step, ssm_b_re, ssm_b_im,
                                       ssm_c_re, ssm_c_im)
    stream = (x.reshape(n_lat, d), ctx.reshape(nb * l_ctx, d), n_lat // ROW_TILE)

    for l in range(depth):
        last = l == depth - 1
        n_rows = n_lat if last else n_all
        mods = mod_all[l, :nb + 1].reshape(nb + 1, N_MOD, d)
        q, k, v, us, up = _inproj(*stream, n_all, norm1_g[l][None], mods, w_in[l].astype(BF16),
                                  cos_t, sin_t, tiles_per_batch, nb)
        lam_init = 0.8 - 0.6 * math.exp(-0.3 * l)
        lam = (jnp.exp(jnp.sum(lam_q1[l] * lam_k1[l])) - jnp.exp(jnp.sum(lam_q2[l] * lam_k2[l]))
               + lam_init).reshape(1).astype(F32)
        attn = _attention(q, k, v, lam, subln_g[l][None], nb, l_lat, l_ctx, not last,
                          1.0 - lam_init)
        y = _ssm(us, tuple(m[l] for m in ssm_mats), ssm_d[l][None], nb, l_lat, l_ctx, not last)
        w_blk = jax.scipy.linalg.block_diag(*[pool_w[l, gi] for gi in range(len(POOL_WINDOWS))])
        pool = _pool(up, w_blk.astype(BF16), pool_scale[l][None], nb, l_lat, l_ctx, n_rows)
        xs = _mixout(*stream, attn, y, pool, mods, ssm_w_glu[l].astype(BF16), w_out[l].astype(BF16),
                     tiles_per_batch, nb, n_rows)
        h2, beta, s2, e1, e2 = _peer_scores(xs, norm2_g[l][None], mods, peer_wq[l].astype(BF16),
                                               peer_keys[l].astype(BF16), tiles_per_batch, nb, n_rows)
        xs = _peer_dense(h2, xs, mods, peer_u[l].astype(BF16), _transpose_cast(peer_v[l], BF16),
                         beta, s2, e1, e2, final_g[None], tiles_per_batch, nb, n_rows, last)
        stream = (xs, xs, n_rows // ROW_TILE)
    return xs.reshape(nb, l_lat, d)
```

```python
import functools
import math

import numpy as np
import jax
import jax.numpy as jnp
from jax import lax
from jax.experimental import pallas as pl
from jax.experimental.pallas import tpu as pltpu

F32 = jnp.float32
BF16 = jnp.bfloat16

EPS = 1e-6
GRID_W = 64
N_MOD = 6
ATTN_HD = 64
ATTN_VD = 128
ATTN_VROWS = ATTN_VD + 16
LOG2E = float(np.log2(np.e))
ATTN_HEADS = 4
ATTN_W = 512
ROPE_BASE = 10000.0
SSM_W = 256
SSM_GROUP = 16
SSM_GROUPS = 16
SSM_STATE = 64
SSM_CHUNK = 16
POOL_W = 256
POOL_WINDOWS = (2, 4, 8, 16)
POOL_GROUP = 64
POOL_HALO = 8
SSM_OFF = 3 * ATTN_W
POOL_OFF = SSM_OFF + SSM_W
IN_W = POOL_OFF + POOL_W
PEER_HEADS = 8
PEER_NKEYS = 128
PEER_KDIM = 128
PEER_TOPK = 16
NEG_BIG = -3.0e38
SQRT_HALF = float(np.sqrt(0.5).astype(np.float32))

LANE = 128
ROW_TILE = 256
VMEM_LIMIT = 56 * 1024 * 1024


def _cparams(sem):
    return pltpu.CompilerParams(dimension_semantics=sem, vmem_limit_bytes=VMEM_LIMIT)


def _gelu(x):
    return 0.5 * x * (1.0 + lax.erf(x * SQRT_HALF))


def _norm_mod(x, g, shift, scale):
    ms = jnp.mean(x * x, axis=-1, keepdims=True)
    y = x * lax.rsqrt(ms + EPS) * g
    return y * (1.0 + scale) + shift


def _mod_kernel(s_ref, w_ref, b_ref, o_ref):
    s = s_ref[...]
    s = s * jax.nn.sigmoid(s)
    o_ref[0] = jnp.dot(s.astype(BF16), w_ref[0].astype(BF16), preferred_element_type=F32) + b_ref[0]


def _mod_vectors(cs, w_mod, b_mod):
    depth, d, n = w_mod.shape
    tn = 1536
    return pl.pallas_call(
        _mod_kernel,
        grid=(depth, n // tn),
        in_specs=[pl.BlockSpec((8, d), lambda l, j: (0, 0)),
                  pl.BlockSpec((1, d, tn), lambda l, j: (l, 0, j)),
                  pl.BlockSpec((1, 1, tn), lambda l, j: (l, 0, j))],
        out_specs=pl.BlockSpec((1, 8, tn), lambda l, j: (l, 0, j)),
        out_shape=jax.ShapeDtypeStruct((depth, 8, n), F32),
        compiler_params=_cparams(("arbitrary", "arbitrary")),
        name="mod_vectors",
    )(cs, w_mod, b_mod.reshape(depth, 1, n))


def _stream_tile(xa_ref, xb_ref, n_a):
    return jnp.where(pl.program_id(0) < n_a, xa_ref[...], xb_ref[...])


def _stream_specs(xa, xb, n_a, tm):
    d = xa.shape[1]
    return [pl.BlockSpec((tm, d), lambda i: (jnp.minimum(i, n_a - 1), 0)),
            pl.BlockSpec((tm, d), lambda i: (jnp.maximum(i - n_a, 0), 0))]


def _inproj_kernel(xa_ref, xb_ref, g_ref, mod_ref, w_ref, cos_ref, sin_ref,
                   q_ref, k_ref, v_ref, us_ref, up_ref, *, n_a):
    tm = xa_ref.shape[0]
    h = _norm_mod(_stream_tile(xa_ref, xb_ref, n_a), g_ref[...], mod_ref[0, 0:1, :], mod_ref[0, 1:2, :])
    p = jnp.dot(h.astype(BF16), w_ref[...], preferred_element_type=F32)
    c = cos_ref[...]
    s = sin_ref[...]
    lane = lax.broadcasted_iota(jnp.int32, (tm, LANE), 1)
    first = (lane % 32) < 16
    for off, ref, sc in ((0, q_ref, ATTN_HD ** -0.5 * LOG2E), (ATTN_W, k_ref, 1.0)):
        for blk in range(ATTN_W // LANE):
            xb = p[:, off + LANE * blk: off + LANE * (blk + 1)]
            partner = jnp.where(first, pltpu.roll(xb, LANE - 16, 1), pltpu.roll(xb, 16, 1))
            ref[:, LANE * blk: LANE * (blk + 1)] = ((xb * c + partner * s) * sc).astype(BF16)
    vt = p[:, 2 * ATTN_W:SSM_OFF].T.astype(BF16)
    ones_rows = (lax.broadcasted_iota(jnp.int32, (ATTN_VROWS - ATTN_VD, tm), 0) == 0).astype(BF16)
    for hd in range(ATTN_HEADS):
        v_ref[0, hd * ATTN_VROWS:hd * ATTN_VROWS + ATTN_VD, :] = vt[hd * ATTN_VD:(hd + 1) * ATTN_VD]
        v_ref[0, hd * ATTN_VROWS + ATTN_VD:(hd + 1) * ATTN_VROWS, :] = ones_rows
    us_ref[...] = p[:, SSM_OFF:POOL_OFF]
    up_ref[...] = p[:, POOL_OFF:IN_W]


def _seg_map(tiles_per_batch, nb):
    def seg(i):
        return jnp.minimum(i // tiles_per_batch, nb)
    return seg


def _inproj(xa, xb, n_a, t, g, mods, w, cos_t, sin_t, tiles_per_batch, nb):
    d = xa.shape[1]
    tm = ROW_TILE
    seg = _seg_map(tiles_per_batch, nb)
    row = lambda i: (i, 0)
    rope_row = lambda i: (jnp.where(i < nb * tiles_per_batch, i % tiles_per_batch, tiles_per_batch), 0)
    return pl.pallas_call(
        functools.partial(_inproj_kernel, n_a=n_a),
        grid=(t // tm,),
        in_specs=_stream_specs(xa, xb, n_a, tm) + [
                  pl.BlockSpec((1, d), lambda i: (0, 0)),
                  pl.BlockSpec((1, N_MOD, d), lambda i: (seg(i), 0, 0)),
                  pl.BlockSpec((d, IN_W), lambda i: (0, 0)),
                  pl.BlockSpec((tm, LANE), rope_row),
                  pl.BlockSpec((tm, LANE), rope_row)],
        out_specs=[pl.BlockSpec((tm, ATTN_W), row),
                   pl.BlockSpec((tm, ATTN_W), row),
                   pl.BlockSpec((1, ATTN_HEADS * ATTN_VROWS, tm), lambda i: (i, 0, 0)),
                   pl.BlockSpec((tm, SSM_W), row),
                   pl.BlockSpec((tm, POOL_W), row)],
        out_shape=[jax.ShapeDtypeStruct((t, ATTN_W), BF16),
                   jax.ShapeDtypeStruct((t, ATTN_W), BF16),
                   jax.ShapeDtypeStruct((t // tm, ATTN_HEADS * ATTN_VROWS, tm), BF16),
                   jax.ShapeDtypeStruct((t, SSM_W), F32),
                   jax.ShapeDtypeStruct((t, POOL_W), F32)],
        compiler_params=_cparams(("arbitrary",)),
        name="inproj",
    )(xa, xb, g, mods, w, cos_t, sin_t)


def _attn_stages(q, tq):
    lane = lax.broadcasted_iota(jnp.int32, (tq, LANE), 1)
    zero = jnp.zeros_like(q)
    qs = jnp.concatenate([jnp.where(lane < ATTN_HD, q, zero),
                          jnp.where(lane >= ATTN_HD, q, zero)], axis=0)

    def scores(kb):
        return lax.dot_general(kb, qs, (((1,), (1,)), ((), ())), preferred_element_type=F32)

    def softmax(m, s):
        m_new = jnp.maximum(m, jnp.max(s, axis=0, keepdims=True))
        return m_new, jnp.exp2(m - m_new), jnp.exp2(s - m_new).astype(BF16)

    def weighted(acc, alpha, pb, vts):
        acc = alpha * acc
        rows = pb.shape[0] // len(vts)
        for c, vt in enumerate(vts):
            acc = acc + jnp.dot(vt, pb[c * rows:(c + 1) * rows], preferred_element_type=F32)
        return acc

    return scores, softmax, weighted


def _attn_finish(lam, acc, g, tq, out_scale):
    o = acc[:ATTN_VD] / acc[ATTN_VD:ATTN_VD + 1]
    o = (o[:, :tq] - lam * o[:, tq:]).T
    ms = jnp.mean(o * o, axis=-1, keepdims=True)
    return (o * lax.rsqrt(ms + EPS) * g * out_scale).astype(BF16)


def _attn_ctx_kernel(lam_ref, q_ref, kc_ref, vc_ref, g_ref, o_ref, *, out_scale):
    tq = q_ref.shape[0]
    scores, softmax, weighted = _attn_stages(q_ref[...], tq)
    m = jnp.full((1, 2 * tq), NEG_BIG, F32)
    acc = jnp.zeros((ATTN_VROWS, 2 * tq), F32)
    m, alpha, pb = softmax(m, scores(kc_ref[...]))
    acc = weighted(acc, alpha, pb, [vc_ref[0]])
    o_ref[...] = _attn_finish(lam_ref[0], acc, g_ref[...], tq, out_scale)


def _attn_kernel(lam_ref, q_ref, kl_ref, kc_ref, vl_ref, vc_ref, g_ref, o_ref,
                 s_a, s_b, p_a, p_b, acc_ref, *, n_lat_k, tk, out_scale):
    tq = q_ref.shape[0]
    scores, softmax, weighted = _attn_stages(q_ref[...], tq)
    n_sub = tk // vc_ref.shape[-1]

    def stage_scores(t, s_buf):
        start = pl.multiple_of(t * tk, tk)
        s_buf[...] = scores(kl_ref[pl.ds(start, tk), :])

    def stage_softmax(m, s_buf, p_buf):
        m, alpha, pb = softmax(m, s_buf[...])
        p_buf[...] = pb
        return m, alpha

    def stage_values(alpha, p_buf, t):
        acc_ref[...] = weighted(acc_ref[...], alpha, p_buf[...],
                                [vl_ref[t * n_sub + c] for c in range(n_sub)])

    m = jnp.full((1, 2 * tq), NEG_BIG, F32)
    m, alpha, pb = softmax(m, scores(kc_ref[...]))
    acc_ref[...] = weighted(jnp.zeros((ATTN_VROWS, 2 * tq), F32), alpha, pb, [vc_ref[0]])

    if n_lat_k % 2 == 0:
        stage_scores(0, s_a)
        stage_scores(1, s_b)
        m, alpha = stage_softmax(m, s_a, p_a)

        def pair(i, carry):
            m, alpha = carry
            k = 2 * i
            stage_scores(k + 2, s_a)
            m, alpha_n = stage_softmax(m, s_b, p_b)
            stage_values(alpha, p_a, k)
            stage_scores(k + 3, s_b)
            m, alpha_nn = stage_softmax(m, s_a, p_a)
            stage_values(alpha_n, p_b, k + 1)
            return m, alpha_nn

        m, alpha = lax.fori_loop(0, n_lat_k // 2 - 1, pair, (m, alpha))
        m, alpha_n = stage_softmax(m, s_b, p_b)
        stage_values(alpha, p_a, n_lat_k - 2)
        stage_values(alpha_n, p_b, n_lat_k - 1)
    else:
        for t in range(n_lat_k):
            stage_scores(t, s_a)
            m, alpha = stage_softmax(m, s_a, p_a)
            stage_values(alpha, p_a, t)
    o_ref[...] = _attn_finish(lam_ref[0], acc_ref[...], g_ref[...], tq, out_scale)


def _attention(q, k, vt, lam, g, nb, l_lat, l_ctx, with_ctx_queries, out_scale):
    tq_ctx = ROW_TILE
    tq = 512 if l_lat % 512 == 0 else ROW_TILE
    tk = 512 if l_lat % 512 == 0 else l_lat
    vt_tile = vt.shape[-1]
    assert l_ctx == tq_ctx and l_lat % tq == 0 and vt_tile == l_ctx and tk % vt_tile == 0
    n_lat_q = l_lat // tq
    n_vt_lat = l_lat // vt_tile
    ctx_blk0 = nb * l_lat // l_ctx
    smem = pl.BlockSpec(memory_space=pltpu.SMEM)
    gain = pl.BlockSpec((1, LANE), lambda b, h, i: (0, 0))
    kc_spec = pl.BlockSpec((l_ctx, LANE), lambda b, h, i: (ctx_blk0 + b, h))
    vc_spec = pl.BlockSpec((1, ATTN_VROWS, vt_tile), lambda b, h, i: (nb * n_vt_lat + b, h, 0))
    sem = _cparams(("arbitrary", "arbitrary", "arbitrary"))
    lat = pl.pallas_call(
        functools.partial(_attn_kernel, n_lat_k=l_lat // tk, tk=tk, out_scale=out_scale),
        grid=(nb, ATTN_HEADS, n_lat_q),
        in_specs=[smem,
                  pl.BlockSpec((tq, LANE), lambda b, h, i: (b * n_lat_q + i, h)),
                  pl.BlockSpec((l_lat, LANE), lambda b, h, i: (b, h)),
                  kc_spec,
                  pl.BlockSpec((n_vt_lat, ATTN_VROWS, vt_tile), lambda b, h, i: (b, h, 0)),
                  vc_spec, gain],
        out_specs=pl.BlockSpec((tq, LANE), lambda b, h, i: (b * n_lat_q + i, h)),
        out_shape=jax.ShapeDtypeStruct((nb * l_lat, ATTN_W), BF16),
        scratch_shapes=[pltpu.VMEM((tk, 2 * tq), F32), pltpu.VMEM((tk, 2 * tq), F32),
                        pltpu.VMEM((tk, 2 * tq), BF16), pltpu.VMEM((tk, 2 * tq), BF16),
                        pltpu.VMEM((ATTN_VROWS, 2 * tq), F32)],
        compiler_params=sem,
        name="diff_attention",
    )(lam, q, k, k, vt, vt, g)
    if not with_ctx_queries:
        return lat, None
    ctx = pl.pallas_call(
        functools.partial(_attn_ctx_kernel, out_scale=out_scale),
        grid=(nb, ATTN_HEADS, 1),
        in_specs=[smem,
                  pl.BlockSpec((tq_ctx, LANE), lambda b, h, i: (ctx_blk0 + b, h)),
                  kc_spec, vc_spec, gain],
        out_specs=pl.BlockSpec((tq_ctx, LANE), lambda b, h, i: (b, h)),
        out_shape=jax.ShapeDtypeStruct((nb * l_ctx, ATTN_W), BF16),
        compiler_params=sem,
        name="diff_attention_ctx",
    )(lam, q, k, vt, g)
    return lat, ctx


def _ssm_kernel(ul_ref, uc_ref, m_ref, bm_ref, cm_ref, a_ref, d_ref, yl_ref, yc_ref,
                s_scr, h_scr, *, ctx_out):
    tc, pw = SSM_CHUNK, 2 * SSM_GROUP
    nl = ul_ref.shape[0] // tc
    nc = uc_ref.shape[0] // tc
    w = m_ref.shape[-1]
    half = s_scr.shape[-1] // 2
    sub = pl.program_id(1) % (LANE // pw)
    nt_dims = (((1,), (1,)), ((), ()))
    ch = lax.broadcasted_iota(jnp.int32, (LANE, 1), 0) - sub * pw
    valid = (ch >= 0) & (ch < pw)
    tgt0 = (ch >> 4) * (tc * SSM_GROUP) + (ch & (SSM_GROUP - 1))
    col = lax.broadcasted_iota(jnp.int32, (1, w), 1)

    def perm(t):
        return jnp.where(valid & (col == tgt0 + t * SSM_GROUP), 1.0, 0.0).astype(BF16)

    def fold(ref, n):
        acc = jnp.zeros((n, w), F32)
        for t in range(tc):
            acc = acc + jnp.dot(ref[pl.ds(t, n, stride=tc), :].astype(BF16), perm(t),
                                preferred_element_type=F32)
        return acc.astype(BF16)

    def unfold_add(y, ref, n):
        hi = y.astype(BF16)
        r1 = y - hi.astype(F32)
        mid = r1.astype(BF16)
        lo = (r1 - mid.astype(F32)).astype(BF16)
        for t in range(tc):
            p = perm(t)
            o = sum(lax.dot_general(part, p, nt_dims, preferred_element_type=F32)
                    for part in (hi, mid, lo))
            ref[pl.ds(t, n, stride=tc), :] += o

    @pl.when(sub == 0)
    def _():
        yl_ref[...] = ul_ref[...] * d_ref[...]
        yc_ref[...] = uc_ref[...] * d_ref[...] if ctx_out else jnp.zeros_like(yc_ref)

    ulb = fold(ul_ref, nl)
    ucb = fold(uc_ref, nc)
    yl = jnp.dot(ulb, m_ref[0], preferred_element_type=F32)
    if ctx_out:
        yc = jnp.dot(ucb, m_ref[0], preferred_element_type=F32)
    for dr in range(2):
        s_scr[dr, 0:nc, :] = jnp.dot(ucb, bm_ref[dr, 0], preferred_element_type=F32)
        s_scr[dr, nc:nc + nl, :] = jnp.dot(ulb, bm_ref[dr, 0], preferred_element_type=F32)
    a = a_ref[0]
    afr, afi, arr, ari = a[0:1], a[1:2], a[2:3], a[3:4]

    def make_body(base, n):
        def body(t, carry):
            fr, fi, rr, ri = carry
            rf = base + t
            rv = base + n - 1 - t
            h_scr[0, pl.ds(rf, 1), :] = jnp.concatenate([fr, fi], axis=-1)
            h_scr[1, pl.ds(rv, 1), :] = jnp.concatenate([rr, ri], axis=-1)
            sf = s_scr[0, pl.ds(rf, 1), :]
            sv = s_scr[1, pl.ds(rv, 1), :]
            nfr = afr * fr - afi * fi + sf[:, :half]
            nfi = afr * fi + afi * fr + sf[:, half:]
            nrr = arr * rr - ari * ri + sv[:, :half]
            nri = arr * ri + ari * rr + sv[:, half:]
            return nfr, nfi, nrr, nri
        return body

    z = jnp.zeros((1, half), F32)
    carry = lax.fori_loop(0, nc, make_body(0, nc), (z, z, z, z))
    lax.fori_loop(0, nl, make_body(nc, nl), carry)
    for dr in range(2):
        yl = yl + jnp.dot(h_scr[dr, nc:nc + nl, :].astype(BF16), cm_ref[dr, 0],
                          preferred_element_type=F32)
        if ctx_out:
            yc = yc + jnp.dot(h_scr[dr, 0:nc, :].astype(BF16), cm_ref[dr, 0],
                              preferred_element_type=F32)
    unfold_add(yl, yl_ref, nl)
    if ctx_out:
        unfold_add(yc, yc_ref, nc)


def _ssm_matrices(lam_re, lam_im, log_step, b_re, b_im, c_re, c_im):
    tc = SSM_CHUNK
    g, p, hh = SSM_GROUPS, SSM_STATE, SSM_GROUP
    npair = g // 2
    step = jnp.exp(log_step)[..., None]
    den = lam_re * lam_re + lam_im * lam_im

    def power(k):
        er = jnp.exp(lam_re * step * k)
        return er * jnp.cos(lam_im * step * k), er * jnp.sin(lam_im * step * k)

    ar, ai = power(1.0)
    nr = ar - 1.0
    cr_ = (nr * lam_re + ai * lam_im) / den
    ci_ = (ai * lam_re - nr * lam_im) / den
    bbr = cr_[..., None] * b_re - ci_[..., None] * b_im
    bbi = cr_[..., None] * b_im + ci_[..., None] * b_re
    ks = jnp.arange(tc + 1, dtype=F32)
    pw = jax.vmap(power)(ks)
    pwr, pwi = pw
    cpr = c_re[None] * pwr[:, :, :, None, :] - c_im[None] * pwi[:, :, :, None, :]
    cpi = c_re[None] * pwi[:, :, :, None, :] + c_im[None] * pwr[:, :, :, None, :]
    hp = lax.Precision.HIGHEST
    kk = (jnp.einsum('kdgop,dgpi->kdgoi', cpr[:tc], bbr, precision=hp)
          - jnp.einsum('kdgop,dgpi->kdgoi', cpi[:tc], bbi, precision=hp))
    zpad = jnp.zeros((tc - 1,) + kk.shape[2:], F32)
    lagtab = (jnp.concatenate([zpad, kk[:, 0]], 0)
              + jnp.concatenate([kk[::-1, 1], zpad], 0))
    lagtab = jnp.transpose(lagtab, (1, 3, 0, 2)).reshape(g, hh, (2 * tc - 1) * hh)
    mm = jnp.stack([lagtab[:, :, (tc - 1 - s) * hh:(2 * tc - 1 - s) * hh] for s in range(tc)], axis=1)
    mm = mm.reshape(g, tc * hh, tc * hh)
    zero = jnp.zeros_like(mm[0::2])
    m2 = jnp.concatenate([jnp.concatenate([mm[0::2], zero], -1),
                          jnp.concatenate([zero, mm[1::2]], -1)], -2)
    def bmat(dr, exps):
        pr = pwr[exps, dr]
        pi = pwi[exps, dr]
        re = pr[..., None] * bbr[dr][None] - pi[..., None] * bbi[dr][None]
        im = pr[..., None] * bbi[dr][None] + pi[..., None] * bbr[dr][None]
        re = jnp.transpose(re, (1, 0, 3, 2)).reshape(g, tc * hh, p)
        im = jnp.transpose(im, (1, 0, 3, 2)).reshape(g, tc * hh, p)
        z = jnp.zeros_like(re[0::2])
        top = jnp.concatenate([re[0::2], z, im[0::2], z], -1)
        bot = jnp.concatenate([z, re[1::2], z, im[1::2]], -1)
        return jnp.concatenate([top, bot], -2)
    bm = jnp.stack([bmat(0, jnp.arange(tc - 1, -1, -1)), bmat(1, jnp.arange(tc))])
    def cmat(dr, exps):
        re = jnp.transpose(cpr[exps, dr], (1, 3, 0, 2)).reshape(g, p, tc * hh)
        im = jnp.transpose(cpi[exps, dr], (1, 3, 0, 2)).reshape(g, p, tc * hh)
        z = jnp.zeros_like(re[0::2])
        return jnp.concatenate([jnp.concatenate([re[0::2], z], -1),
                                jnp.concatenate([z, re[1::2]], -1),
                                jnp.concatenate([-im[0::2], z], -1),
                                jnp.concatenate([z, -im[1::2]], -1)], -2)
    cm = jnp.stack([cmat(0, jnp.arange(1, tc + 1)), cmat(1, jnp.arange(tc, 0, -1))])
    a16r = pwr[tc].reshape(2, npair, 2 * p)
    a16i = pwi[tc].reshape(2, npair, 2 * p)
    a16 = jnp.stack([a16r[0], a16i[0], a16r[1], a16i[1]], axis=1)
    return m2.astype(BF16), bm.astype(BF16), cm.astype(BF16), a16


def _ssm(us, mats, d, nb, l_lat, l_ctx, ctx_out):
    m2, bm, cm, a16 = mats
    tc, hh = SSM_CHUNK, SSM_GROUP
    npair = SSM_GROUPS // 2
    w = 2 * tc * hh
    per_col = LANE // (2 * hh)
    nl, nc = l_lat // tc, l_ctx // tc
    ctx_blk0 = nb * l_lat // l_ctx
    yl, yc = pl.pallas_call(
        functools.partial(_ssm_kernel, ctx_out=ctx_out),
        grid=(nb, npair),
        in_specs=[pl.BlockSpec((l_lat, LANE), lambda b, q: (b, q // per_col)),
                  pl.BlockSpec((l_ctx, LANE), lambda b, q: (ctx_blk0 + b, q // per_col)),
                  pl.BlockSpec((1, w, w), lambda b, q: (q, 0, 0)),
                  pl.BlockSpec((2, 1, w, 4 * SSM_STATE), lambda b, q: (0, q, 0, 0)),
                  pl.BlockSpec((2, 1, 4 * SSM_STATE, w), lambda b, q: (0, q, 0, 0)),
                  pl.BlockSpec((1, 4, 2 * SSM_STATE), lambda b, q: (q, 0, 0)),
                  pl.BlockSpec((1, LANE), lambda b, q: (0, q // per_col))],
        out_specs=[pl.BlockSpec((l_lat, LANE), lambda b, q: (b, q // per_col)),
                   pl.BlockSpec((l_ctx, LANE), lambda b, q: (b, q // per_col))],
        out_shape=[jax.ShapeDtypeStruct((nb * l_lat, SSM_W), F32),
                   jax.ShapeDtypeStruct((nb * l_ctx, SSM_W), F32)],
        scratch_shapes=[pltpu.VMEM((2, nc + nl, 4 * SSM_STATE), F32),
                        pltpu.VMEM((2, nc + nl, 4 * SSM_STATE), F32)],
        compiler_params=_cparams(("arbitrary", "arbitrary")),
        name="s5_scan",
    )(us, us, m2, bm, cm, a16, d)
    return yl, (yc if ctx_out else None)


def _pool_kernel(prev_ref, cur_ref, next_ref, w_ref, scale_ref, o_ref, ext,
                 *, n_lat_tiles, tiles_per_lat, l_lat, l_ctx):
    r = cur_ref.shape[0]
    hl = POOL_HALO
    i = pl.program_id(0)
    is_lat = i < n_lat_tiles
    pos = jnp.where(is_lat, i % tiles_per_lat, 0)
    n_tiles = jnp.where(is_lat, tiles_per_lat, l_ctx // r)
    seq_len = jnp.where(is_lat, l_lat, l_ctx)
    zeros = jnp.zeros((hl, POOL_W), F32)
    ext[0:hl, :] = jnp.where(pos > 0, prev_ref[...], zeros)
    ext[hl:hl + r, :] = cur_ref[...]
    ext[hl + r:hl + r + hl, :] = jnp.where(pos < n_tiles - 1, next_ref[...], zeros)
    u = cur_ref[...]
    t = pos * r + lax.broadcasted_iota(jnp.int32, (r, 1), 0)
    lane = lax.broadcasted_iota(jnp.int32, (r, POOL_W), 1)

    def win(k):
        return ext[hl + k:hl + k + r, :]

    acc = win(-1) + u
    mean = jnp.zeros((r, POOL_W), F32)
    lo_k, hi_k = -1, 0
    for gi, wn in enumerate(POOL_WINDOWS):
        hw = wn // 2
        while lo_k > -hw:
            lo_k -= 1
            acc = acc + win(lo_k)
        while hi_k < hw - 1:
            hi_k += 1
            acc = acc + win(hi_k)
        cnt = (jnp.minimum(t + hw, seq_len) - jnp.maximum(t - hw, 0)).astype(F32)
        sel = (lane >= gi * POOL_GROUP) & (lane < (gi + 1) * POOL_GROUP)
        mean = jnp.where(sel, acc / cnt, mean)
    dlt = (mean - u).astype(BF16)
    o_ref[...] = jnp.dot(dlt, w_ref[...], preferred_element_type=F32) * scale_ref[...]


def _pool(up, w_blk, scale, nb, l_lat, l_ctx, n_rows):
    r = ROW_TILE
    hl = POOL_HALO
    n_tiles = n_rows // r
    last8 = up.shape[0] // hl - 1
    kern = functools.partial(_pool_kernel, n_lat_tiles=nb * l_lat // r, tiles_per_lat=l_lat // r,
                             l_lat=l_lat, l_ctx=l_ctx)
    return pl.pallas_call(
        kern,
        grid=(n_tiles,),
        in_specs=[pl.BlockSpec((hl, POOL_W), lambda i: (jnp.maximum(i * (r // hl) - 1, 0), 0)),
                  pl.BlockSpec((r, POOL_W), lambda i: (i, 0)),
                  pl.BlockSpec((hl, POOL_W), lambda i: (jnp.minimum((i + 1) * (r // hl), last8), 0)),
                  pl.BlockSpec((POOL_W, POOL_W), lambda i: (0, 0)),
                  pl.BlockSpec((1, POOL_W), lambda i: (0, 0))],
        out_specs=pl.BlockSpec((r, POOL_W), lambda i: (i, 0)),
        out_shape=jax.ShapeDtypeStruct((n_rows, POOL_W), F32),
        scratch_shapes=[pltpu.VMEM((r + 2 * hl, POOL_W), F32)],
        compiler_params=_cparams(("arbitrary",)),
        name="pool_mix",
    )(up, up, up, w_blk, scale)


def _mixout_kernel(xa_ref, xb_ref, aa_ref, ab_ref, ya_ref, yb_ref, pool_ref, mod_ref, wglu_ref,
                   wout_ref, o_ref, *, n_a, n_lat):
    yg = _gelu(_stream_tile(ya_ref, yb_ref, n_lat))
    z = yg * jax.nn.sigmoid(jnp.dot(yg.astype(BF16), wglu_ref[...], preferred_element_type=F32))
    r = jnp.dot(_stream_tile(aa_ref, ab_ref, n_lat), wout_ref[0:ATTN_W, :],
                preferred_element_type=F32)
    r = r + jnp.dot(z.astype(BF16), wout_ref[ATTN_W:ATTN_W + SSM_W, :], preferred_element_type=F32)
    r = r + jnp.dot(pool_ref[...].astype(BF16), wout_ref[ATTN_W + SSM_W:, :],
                    preferred_element_type=F32)
    o_ref[...] = _stream_tile(xa_ref, xb_ref, n_a) + mod_ref[0, 2:3, :] * r


def _mixout(xa, xb, n_a, attn, y, pool, mods, wglu, wout, tiles_per_batch, nb, n_rows):
    d = xa.shape[1]
    tm = ROW_TILE
    seg = _seg_map(tiles_per_batch, nb)
    row = lambda i: (i, 0)
    n_lat = nb * tiles_per_batch
    attn = (attn[0], attn[0] if attn[1] is None else attn[1])
    y = (y[0], y[0] if y[1] is None else y[1])
    return pl.pallas_call(
        functools.partial(_mixout_kernel, n_a=n_a, n_lat=n_lat),
        grid=(n_rows // tm,),
        in_specs=_stream_specs(xa, xb, n_a, tm) + _stream_specs(*attn, n_lat, tm)
                 + _stream_specs(*y, n_lat, tm) + [
                  pl.BlockSpec((tm, POOL_W), row),
                  pl.BlockSpec((1, N_MOD, d), lambda i: (seg(i), 0, 0)),
                  pl.BlockSpec((SSM_W, SSM_W), lambda i: (0, 0)),
                  pl.BlockSpec((d, d), lambda i: (0, 0))],
        out_specs=pl.BlockSpec((tm, d), row),
        out_shape=jax.ShapeDtypeStruct((n_rows, d), F32),
        compiler_params=_cparams(("arbitrary",)),
        name="mix_out",
    )(xa, xb, *attn, *y, pool, mods, wglu, wout)


def _oddeven_merge(lo, hi, r):
    step = r * 2
    if step < hi - lo:
        yield from _oddeven_merge(lo, hi, step)
        yield from _oddeven_merge(lo + r, hi, step)
        yield from [(i, i + r) for i in range(lo + r, hi - r, step)]
    else:
        yield (lo, lo + r)


def _oddeven_sort(lo, hi):
    if hi - lo >= 1:
        mid = lo + (hi - lo) // 2
        yield from _oddeven_sort(lo, mid)
        yield from _oddeven_sort(mid + 1, hi)
        yield from _oddeven_merge(lo, hi, 1)


_SORT16 = tuple(_oddeven_sort(0, PEER_TOPK - 1))
_BITONIC16 = tuple((i, i + d) for d in (8, 4, 2, 1) for i in range(PEER_TOPK) if not i & d)


def _compare_exchange(rows, net):
    rows = list(rows)
    for i, j in net:
        hi, lo = jnp.maximum(rows[i], rows[j]), jnp.minimum(rows[i], rows[j])
        rows[i], rows[j] = hi, lo
    return rows


def _top_sorted(s, k):
    assert k == PEER_TOPK and s.shape[0] == 8 * PEER_TOPK
    rows = _compare_exchange([s[8 * v:8 * v + 8] for v in range(PEER_TOPK)], _SORT16)
    for shift in (4, 2, 1):
        other = [pltpu.roll(r, shift, 0) for r in rows]
        rows = _compare_exchange([jnp.maximum(rows[i], other[PEER_TOPK - 1 - i])
                                  for i in range(PEER_TOPK)], _BITONIC16)
    return [r[0:1] for r in rows]


def _peer_score_kernel(x_ref, g_ref, mod_ref, wq_ref, keys_ref,
                       h_ref, beta_ref, s2_ref, e1_ref, e2_ref, q_scr):
    tm = x_ref.shape[0]
    h = _norm_mod(x_ref[...], g_ref[...], mod_ref[0, 3:4, :], mod_ref[0, 4:5, :]).astype(BF16)
    h_ref[...] = h
    q = jnp.dot(h, wq_ref[...], preferred_element_type=F32)
    for hc in range(2 * PEER_HEADS):
        q_scr[hc] = q[:, hc * PEER_KDIM:(hc + 1) * PEER_KDIM].astype(BF16)
    row8 = lax.broadcasted_iota(jnp.int32, (8, tm), 0)

    def head(hd, _):
        nt = (((1,), (1,)), ((), ()))
        s1 = lax.dot_general(keys_ref[hd, 0], q_scr[2 * hd], nt,
                             preferred_element_type=F32)
        s2 = lax.dot_general(keys_ref[hd, 1], q_scr[2 * hd + 1], nt,
                             preferred_element_type=F32)
        a = _top_sorted(s1, PEER_TOPK)
        b = _top_sorted(s2, PEER_TOPK)
        acat = jnp.concatenate(a, axis=0)
        bcat = jnp.concatenate(b, axis=0)
        pieces = [a[0] + bcat]
        for i in range(2, 9):
            piece = a[i - 1] + bcat[0:8]
            n_valid = PEER_TOPK // i
            pieces.append(piece if n_valid >= 8 else jnp.where(row8 < n_valid, piece, NEG_BIG))
        pieces.append(acat[8:16] + b[0])
        cand = jnp.concatenate(pieces, axis=0)
        work = cand
        cum = jnp.zeros((1, tm), F32)
        tau = jnp.full((1, tm), NEG_BIG, F32)
        for _ in range(PEER_TOPK):
            m = jnp.max(work, axis=0, keepdims=True)
            eq = work == m
            new = cum + jnp.sum(eq.astype(F32), axis=0, keepdims=True)
            tau = jnp.where((cum < PEER_TOPK) & (new >= PEER_TOPK), m, tau)
            work = jnp.where(eq, NEG_BIG, work)
            cum = new
        top = a[0] + b[0]
        z = jnp.sum(jnp.where(cand >= tau, jnp.exp(cand - top), 0.0), axis=0, keepdims=True)
        big = -NEG_BIG
        b_rows = [bcat] + [bcat[0:8]] * 7
        beta_r = [jnp.min(jnp.where(pc >= tau, br, big), axis=0, keepdims=True)
                  for pc, br in zip(pieces[:8], b_rows)]
        tail = jnp.where(pieces[8] >= tau, b[0], big)
        beta_r += [tail[r:r + 1] for r in range(8)]
        beta = jnp.full_like(s1, big)
        for r in range(PEER_TOPK):
            beta = jnp.where(s1 == a[r], beta_r[r], beta)
        beta_ref[hd] = beta
        s2_ref[hd] = s2
        e1_ref[hd] = jnp.exp(s1 - a[0])
        e2_ref[hd] = jnp.exp(s2 - b[0]) * (1.0 / z)
        return 0

    lax.fori_loop(0, PEER_HEADS, head, 0, unroll=4)


def _peer_scores(x, g, mods, wq, keys, tiles_per_batch, nb, n_rows):
    d = x.shape[1]
    tm = ROW_TILE
    seg = _seg_map(tiles_per_batch, nb)
    nq = wq.shape[1]
    col = lambda i: (0, 0, i)
    big = jax.ShapeDtypeStruct((PEER_HEADS, PEER_NKEYS, n_rows), F32)
    return pl.pallas_call(
        _peer_score_kernel,
        grid=(n_rows // tm,),
        in_specs=[pl.BlockSpec((tm, d), lambda i: (i, 0)),
                  pl.BlockSpec((1, d), lambda i: (0, 0)),
                  pl.BlockSpec((1, N_MOD, d), lambda i: (seg(i), 0, 0)),
                  pl.BlockSpec((d, nq), lambda i: (0, 0)),
                  pl.BlockSpec((PEER_HEADS, 2, PEER_NKEYS, PEER_KDIM), lambda i: (0, 0, 0, 0))],
        out_specs=[pl.BlockSpec((tm, d), lambda i: (i, 0)),
                   pl.BlockSpec((PEER_HEADS, PEER_NKEYS, tm), col),
                   pl.BlockSpec((PEER_HEADS, PEER_NKEYS, tm), col),
                   pl.BlockSpec((PEER_HEADS, PEER_NKEYS, tm), col),
                   pl.BlockSpec((PEER_HEADS, PEER_NKEYS, tm), col)],
        out_shape=[jax.ShapeDtypeStruct((n_rows, d), BF16), big, big, big, big],
        scratch_shapes=[pltpu.VMEM((2 * PEER_HEADS, tm, PEER_KDIM), BF16)],
        compiler_params=_cparams(("arbitrary",)),
        name="peer_scores",
    )(x, g, mods, wq, keys)


def _transpose_cast_kernel(x_ref, o_ref):
    o_ref[...] = x_ref[...].T.astype(o_ref.dtype)


def _transpose_cast(x, dtype, rows_per_step=512):
    n, d = x.shape
    return pl.pallas_call(
        _transpose_cast_kernel,
        grid=(n // rows_per_step,),
        in_specs=[pl.BlockSpec((rows_per_step, d), lambda i: (i, 0))],
        out_specs=pl.BlockSpec((d, rows_per_step), lambda i: (0, i)),
        out_shape=jax.ShapeDtypeStruct((d, n), dtype),
        compiler_params=_cparams(("arbitrary",)),
        name="transpose_cast",
    )(x)


PEER_CHUNK_KEYS = 8
PEER_TOK_TILE = 512
PEER_SUB = 256


def _peer_dense_kernel(h_ref, x_ref, mod_ref, u_ref, vt_ref, beta_ref, s2_ref, e1_ref, e2_ref,
                       fg_ref, o_ref, acc_ref, a0, a1, w0, w1, row_scr,
                       *, n_chunks, final_norm):
    n_i = PEER_CHUNK_KEYS
    tm = h_ref.shape[0]
    g = pl.program_id(0)
    nt_dims = (((1,), (1,)), ((), ()))
    n_k = u_ref.shape[1] // PEER_SUB
    n_lb = tm // LANE
    assert n_k == n_lb == vt_ref.shape[1] // PEER_SUB and n_i * PEER_NKEYS == u_ref.shape[0]
    n_mt = u_ref.shape[0] // PEER_SUB

    @pl.when(g == 0)
    def _():
        acc_ref[...] = jnp.zeros_like(acc_ref)
        a1[...] = jnp.zeros_like(a1)
        w0[...] = jnp.zeros_like(w0)

    for hd in range(PEER_HEADS):
        betab = beta_ref[hd]
        e1b = e1_ref[hd]
        for ii in range(n_i):
            row_scr[hd, ii, 0, :, 0:tm] = jnp.broadcast_to(betab[ii:ii + 1], (8, tm))
            row_scr[hd, ii, 1, :, 0:tm] = jnp.broadcast_to(e1b[ii:ii + 1], (8, tm))

    def run(a_wr, a_rd, w_wr, w_rd):
        n_trips = tm // PEER_SUB
        slabs_per_trip = PEER_NKEYS // 16 // n_trips
        for lb in range(n_lb):
            th, lo = divmod(lb * LANE, PEER_SUB)
            ls = slice(lb * LANE, (lb + 1) * LANE)
            lh = slice(lo, lo + LANE)
            ms = slice(lb * PEER_SUB, (lb + 1) * PEER_SUB)

            def trip(tr, _):
                t0 = pl.multiple_of(tr * PEER_SUB, PEER_SUB)
                part_a = lax.dot_general(u_ref[ms, :], h_ref[pl.ds(t0, PEER_SUB), :], nt_dims,
                                         preferred_element_type=F32)
                part_v = jnp.dot(vt_ref[ms, :], w_rd[tr], preferred_element_type=F32)
                for sl in range(slabs_per_trip):
                    j0 = pl.multiple_of((tr * slabs_per_trip + sl) * 16, 16)
                    gs = [[jnp.zeros((8, LANE), F32), jnp.zeros((8, LANE), F32)] for _ in range(n_i)]
                    for hd in range(PEER_HEADS):
                        s2v = s2_ref[hd, pl.ds(j0, 16), ls]
                        e2v = e2_ref[hd, pl.ds(j0, 16), ls]
                        for ii in range(n_i):
                            bs = row_scr[hd, ii, 0, :, ls]
                            be = row_scr[hd, ii, 1, :, ls]
                            for hf in range(2):
                                hs = slice(hf * 8, (hf + 1) * 8)
                                gs[ii][hf] = gs[ii][hf] + jnp.where(s2v[hs] >= bs, e2v[hs], 0.0) * be
                    for ii in range(n_i):
                        r0 = pl.multiple_of(ii * PEER_NKEYS + j0, 16)
                        g16 = jnp.concatenate(gs[ii], axis=0)
                        w_wr[th, pl.ds(r0, 16), lh] = (_gelu(a_rd[th, pl.ds(r0, 16), lh]) * g16).astype(BF16)
                a_wr[tr, ms, :] = part_a
                acc_ref[tr, ms, :] += part_v
                return 0

            lax.fori_loop(0, n_trips, trip, 0, unroll=True)

    @pl.when(g % 2 == 0)
    def _():
        run(a0, a1, w1, w0)

    @pl.when(g % 2 == 1)
    def _():
        run(a1, a0, w0, w1)

    @pl.when((g >= 2) & ((g - 2) % n_chunks == n_chunks - 1))
    def _():
        out = x_ref[...] + mod_ref[0, 5:6, :] * jnp.concatenate(
            [acc_ref[t].T for t in range(acc_ref.shape[0])], axis=0)
        if final_norm:
            ms = jnp.mean(out * out, axis=-1, keepdims=True)
            out = out * lax.rsqrt(ms + EPS) * fg_ref[...]
        o_ref[...] = out
        acc_ref[...] = jnp.zeros_like(acc_ref)


def _peer_dense(h, x, mods, u, vt, beta, s2, e1, e2, fg, tiles_per_batch, nb, n_rows, final_norm):
    d = x.shape[1]
    tm = PEER_TOK_TILE
    n_i = PEER_CHUNK_KEYS
    ne = n_i * PEER_NKEYS
    tiles_per_batch = tiles_per_batch * ROW_TILE // tm
    n_chunks = u.shape[0] // ne
    n_steps = (n_rows // tm) * n_chunks
    assert n_rows % tm == 0 and d == ne
    assert PEER_NKEYS // 16 == (tm // PEER_SUB) * (ne // PEER_SUB)

    def stage(lag):
        def split(g):
            n = jnp.clip(g - lag, 0, n_steps - 1)
            return n // n_chunks, n % n_chunks
        return split

    act, gate, val = stage(0), stage(1), stage(2)
    seg = lambda i: jnp.minimum(i // tiles_per_batch, nb)
    kern = functools.partial(_peer_dense_kernel, n_chunks=n_chunks, final_norm=final_norm)
    full = (PEER_HEADS, PEER_NKEYS, tm)
    rows = (PEER_HEADS, n_i, tm)
    return pl.pallas_call(
        kern,
        grid=(n_steps + 2,),
        in_specs=[pl.BlockSpec((tm, d), lambda g: (act(g)[0], 0)),
                  pl.BlockSpec((tm, d), lambda g: (val(g)[0], 0)),
                  pl.BlockSpec((1, N_MOD, d), lambda g: (seg(val(g)[0]), 0, 0)),
                  pl.BlockSpec((ne, d), lambda g: (act(g)[1], 0)),
                  pl.BlockSpec((d, ne), lambda g: (0, val(g)[1])),
                  pl.BlockSpec(rows, lambda g: (0, gate(g)[1], gate(g)[0])),
                  pl.BlockSpec(full, lambda g: (0, 0, gate(g)[0])),
                  pl.BlockSpec(rows, lambda g: (0, gate(g)[1], gate(g)[0])),
                  pl.BlockSpec(full, lambda g: (0, 0, gate(g)[0])),
                  pl.BlockSpec((1, d), lambda g: (0, 0))],
        out_specs=pl.BlockSpec((tm, d), lambda g: (val(g)[0], 0)),
        out_shape=jax.ShapeDtypeStruct((n_rows, d), F32),
        scratch_shapes=[pltpu.VMEM((tm // PEER_SUB, d, PEER_SUB), F32),
                        pltpu.VMEM((tm // PEER_SUB, ne, PEER_SUB), F32),
                        pltpu.VMEM((tm // PEER_SUB, ne, PEER_SUB), F32),
                        pltpu.VMEM((tm // PEER_SUB, ne, PEER_SUB), BF16),
                        pltpu.VMEM((tm // PEER_SUB, ne, PEER_SUB), BF16),
                        pltpu.VMEM((PEER_HEADS, n_i, 2, 8, tm + LANE), F32)],
        compiler_params=_cparams(("arbitrary",)),
        name="peer_experts",
    )(h, x, mods, u, vt, beta, s2, e1, e2, fg)


def _rope_tables(l_lat, nb, n_ctx_rows):
    rows = l_lat // GRID_W
    r = jnp.repeat(jnp.arange(rows), GRID_W)
    col = jnp.tile(jnp.arange(GRID_W), rows)
    pos = jnp.stack([r, col], axis=-1).astype(F32)
    nf = ATTN_HD // 4
    inv = 1.0 / (ROPE_BASE ** (jnp.arange(nf, dtype=F32) / nf))
    ang = pos[:, :, None] * inv
    cos, sin = jnp.cos(ang), jnp.sin(ang)
    c64 = jnp.concatenate([cos[:, 0], cos[:, 0], cos[:, 1], cos[:, 1]], axis=-1)
    s64 = jnp.concatenate([-sin[:, 0], sin[:, 0], -sin[:, 1], sin[:, 1]], axis=-1)
    c = jnp.tile(c64, (nb, LANE // ATTN_HD))
    s = jnp.tile(s64, (nb, LANE // ATTN_HD))
    c = jnp.concatenate([c, jnp.ones((n_ctx_rows, LANE), F32)], axis=0)
    s = jnp.concatenate([s, jnp.zeros((n_ctx_rows, LANE), F32)], axis=0)
    return c, s


def kernel(x, c, ctx, c_ctx, w_mod, b_mod, norm1_g, norm2_g, w_in, w_out, lam_q1, lam_k1, lam_q2, lam_k2, subln_g, ssm_lambda_re, ssm_lambda_im, ssm_log_step, ssm_b_re, ssm_b_im, ssm_c_re, ssm_c_im, ssm_d, ssm_w_glu, pool_w, pool_scale, peer_wq, peer_keys, peer_u, peer_v, final_g):
    nb, l_lat, d = x.shape
    l_ctx = ctx.shape[1]
    depth = w_mod.shape[0]
    n_lat = nb * l_lat
    n_all = n_lat + nb * l_ctx
    tiles_per_batch = l_lat // ROW_TILE
    assert l_lat % ROW_TILE == 0 and l_ctx == ROW_TILE and nb + 1 <= 8

    cs = jnp.concatenate([c, c_ctx[None], jnp.zeros((8 - nb - 1, d), F32)], axis=0)
    mod_all = _mod_vectors(cs, w_mod, b_mod)
    cos_t, sin_t = _rope_tables(l_lat, 1, ROW_TILE)
    ssm_mats = jax.vmap(_ssm_matrices)(ssm_lambda_re, ssm_lambda_im, ssm_log_step, ssm_b_re, ssm_b_im,
                                       ssm_c_re, ssm_c_im)
    stream = (x.reshape(n_lat, d), ctx.reshape(nb * l_ctx, d), n_lat // ROW_TILE)

    for l in range(depth):
        last = l == depth - 1
        n_rows = n_lat if last else n_all
        mods = mod_all[l, :nb + 1].reshape(nb + 1, N_MOD, d)
        q, k, v, us, up = _inproj(*stream, n_all, norm1_g[l][None], mods, w_in[l].astype(BF16),
                                  cos_t, sin_t, tiles_per_batch, nb)
        lam_init = 0.8 - 0.6 * math.exp(-0.3 * l)
        lam = (jnp.exp(jnp.sum(lam_q1[l] * lam_k1[l])) - jnp.exp(jnp.sum(lam_q2[l] * lam_k2[l]))
               + lam_init).reshape(1).astype(F32)
        attn = _attention(q, k, v, lam, subln_g[l][None], nb, l_lat, l_ctx, not last,
                          1.0 - lam_init)
        y = _ssm(us, tuple(m[l] for m in ssm_mats), ssm_d[l][None], nb, l_lat, l_ctx, not last)
        w_blk = jax.scipy.linalg.block_diag(*[pool_w[l, gi] for gi in range(len(POOL_WINDOWS))])
        pool = _pool(up, w_blk.astype(BF16), pool_scale[l][None], nb, l_lat, l_ctx, n_rows)
        xs = _mixout(*stream, attn, y, pool, mods, ssm_w_glu[l].astype(BF16), w_out[l].astype(BF16),
                     tiles_per_batch, nb, n_rows)
        h2, beta, s2, e1, e2 = _peer_scores(xs, norm2_g[l][None], mods, peer_wq[l].astype(BF16),
                                               peer_keys[l].astype(BF16), tiles_per_batch, nb, n_rows)
        xs = _peer_dense(h2, xs, mods, peer_u[l].astype(BF16), _transpose_cast(peer_v[l], BF16),
                         beta, s2, e1, e2, final_g[None], tiles_per_batch, nb, n_rows, last)
        stream = (xs, xs, n_rows // ROW_TILE)
    return xs.reshape(nb, l_lat, d)
```

```python
import functools
import math

import numpy as np
import jax
import jax.numpy as jnp
from jax import lax
from jax.experimental import pallas as pl
from jax.experimental.pallas import tpu as pltpu

F32 = jnp.float32
BF16 = jnp.bfloat16

EPS = 1e-6
GRID_W = 64
N_MOD = 6
ATTN_HD = 64
ATTN_VD = 128
ATTN_VROWS = ATTN_VD + 16
LOG2E = float(np.log2(np.e))
ATTN_HEADS = 4
ATTN_W = 512
ROPE_BASE = 10000.0
SSM_W = 256
SSM_GROUP = 16
SSM_GROUPS = 16
SSM_STATE = 64
SSM_CHUNK = 16
POOL_W = 256
POOL_WINDOWS = (2, 4, 8, 16)
POOL_GROUP = 64
POOL_HALO = 8
SSM_OFF = 3 * ATTN_W
POOL_OFF = SSM_OFF + SSM_W
IN_W = POOL_OFF + POOL_W
PEER_HEADS = 8
PEER_NKEYS = 128
PEER_KDIM = 128
PEER_TOPK = 16
NEG_BIG = -3.0e38
SQRT_HALF = float(np.sqrt(0.5).astype(np.float32))

LANE = 128
ROW_TILE = 256
VMEM_LIMIT = 56 * 1024 * 1024


def _cparams(sem):
    return pltpu.CompilerParams(dimension_semantics=sem, vmem_limit_bytes=VMEM_LIMIT)


def _gelu(x):
    return 0.5 * x * (1.0 + lax.erf(x * SQRT_HALF))


def _norm_mod(x, g, shift, scale):
    ms = jnp.mean(x * x, axis=-1, keepdims=True)
    y = x * lax.rsqrt(ms + EPS) * g
    return y * (1.0 + scale) + shift


def _mod_kernel(s_ref, w_ref, b_ref, o_ref):
    s = s_ref[...]
    s = s * jax.nn.sigmoid(s)
    o_ref[0] = jnp.dot(s.astype(BF16), w_ref[0].astype(BF16), preferred_element_type=F32) + b_ref[0]


def _mod_vectors(cs, w_mod, b_mod):
    depth, d, n = w_mod.shape
    tn = 1536
    return pl.pallas_call(
        _mod_kernel,
        grid=(depth, n // tn),
        in_specs=[pl.BlockSpec((8, d), lambda l, j: (0, 0)),
                  pl.BlockSpec((1, d, tn), lambda l, j: (l, 0, j)),
                  pl.BlockSpec((1, 1, tn), lambda l, j: (l, 0, j))],
        out_specs=pl.BlockSpec((1, 8, tn), lambda l, j: (l, 0, j)),
        out_shape=jax.ShapeDtypeStruct((depth, 8, n), F32),
        compiler_params=_cparams(("arbitrary", "arbitrary")),
        name="mod_vectors",
    )(cs, w_mod, b_mod.reshape(depth, 1, n))


def _stream_tile(xa_ref, xb_ref, n_a):
    return jnp.where(pl.program_id(0) < n_a, xa_ref[...], xb_ref[...])


def _stream_specs(xa, xb, n_a, tm):
    d = xa.shape[1]
    return [pl.BlockSpec((tm, d), lambda i: (jnp.minimum(i, n_a - 1), 0)),
            pl.BlockSpec((tm, d), lambda i: (jnp.maximum(i - n_a, 0), 0))]


def _inproj_kernel(xa_ref, xb_ref, g_ref, mod_ref, w_ref, cos_ref, sin_ref,
                   q_ref, k_ref, v_ref, us_ref, up_ref, *, n_a):
    tm = xa_ref.shape[0]
    h = _norm_mod(_stream_tile(xa_ref, xb_ref, n_a), g_ref[...], mod_ref[0, 0:1, :], mod_ref[0, 1:2, :])
    p = jnp.dot(h.astype(BF16), w_ref[...], preferred_element_type=F32)
    c = cos_ref[...]
    s = sin_ref[...]
    lane = lax.broadcasted_iota(jnp.int32, (tm, LANE), 1)
    first = (lane % 32) < 16
    for off, ref, sc in ((0, q_ref, ATTN_HD ** -0.5 * LOG2E), (ATTN_W, k_ref, 1.0)):
        for blk in range(ATTN_W // LANE):
            xb = p[:, off + LANE * blk: off + LANE * (blk + 1)]
            partner = jnp.where(first, pltpu.roll(xb, LANE - 16, 1), pltpu.roll(xb, 16, 1))
            ref[:, LANE * blk: LANE * (blk + 1)] = ((xb * c + partner * s) * sc).astype(BF16)
    vt = p[:, 2 * ATTN_W:SSM_OFF].T.astype(BF16)
    ones_rows = (lax.broadcasted_iota(jnp.int32, (ATTN_VROWS - ATTN_VD, tm), 0) == 0).astype(BF16)
    for hd in range(ATTN_HEADS):
        v_ref[0, hd * ATTN_VROWS:hd * ATTN_VROWS + ATTN_VD, :] = vt[hd * ATTN_VD:(hd + 1) * ATTN_VD]
        v_ref[0, hd * ATTN_VROWS + ATTN_VD:(hd + 1) * ATTN_VROWS, :] = ones_rows
    us_ref[...] = p[:, SSM_OFF:POOL_OFF]
    up_ref[...] = p[:, POOL_OFF:IN_W]


def _seg_map(tiles_per_batch, nb):
    def seg(i):
        return jnp.minimum(i // tiles_per_batch, nb)
    return seg


def _inproj(xa, xb, n_a, t, g, mods, w, cos_t, sin_t, tiles_per_batch, nb):
    d = xa.shape[1]
    tm = ROW_TILE
    seg = _seg_map(tiles_per_batch, nb)
    row = lambda i: (i, 0)
    rope_row = lambda i: (jnp.where(i < nb * tiles_per_batch, i % tiles_per_batch, tiles_per_batch), 0)
    return pl.pallas_call(
        functools.partial(_inproj_kernel, n_a=n_a),
        grid=(t // tm,),
        in_specs=_stream_specs(xa, xb, n_a, tm) + [
                  pl.BlockSpec((1, d), lambda i: (0, 0)),
                  pl.BlockSpec((1, N_MOD, d), lambda i: (seg(i), 0, 0)),
                  pl.BlockSpec((d, IN_W), lambda i: (0, 0)),
                  pl.BlockSpec((tm, LANE), rope_row),
                  pl.BlockSpec((tm, LANE), rope_row)],
        out_specs=[pl.BlockSpec((tm, ATTN_W), row),
                   pl.BlockSpec((tm, ATTN_W), row),
                   pl.BlockSpec((1, ATTN_HEADS * ATTN_VROWS, tm), lambda i: (i, 0, 0)),
                   pl.BlockSpec((tm, SSM_W), row),
                   pl.BlockSpec((tm, POOL_W), row)],
        out_shape=[jax.ShapeDtypeStruct((t, ATTN_W), BF16),
                   jax.ShapeDtypeStruct((t, ATTN_W), BF16),
                   jax.ShapeDtypeStruct((t // tm, ATTN_HEADS * ATTN_VROWS, tm), BF16),
                   jax.ShapeDtypeStruct((t, SSM_W), F32),
                   jax.ShapeDtypeStruct((t, POOL_W), F32)],
        compiler_params=_cparams(("arbitrary",)),
        name="inproj",
    )(xa, xb, g, mods, w, cos_t, sin_t)


def _attn_stages(q, tq):
    lane = lax.broadcasted_iota(jnp.int32, (tq, LANE), 1)
    zero = jnp.zeros_like(q)
    qs = jnp.concatenate([jnp.where(lane < ATTN_HD, q, zero),
                          jnp.where(lane >= ATTN_HD, q, zero)], axis=0)

    def scores(kb):
        return lax.dot_general(kb, qs, (((1,), (1,)), ((), ())), preferred_element_type=F32)

    def softmax(m, s):
        m_new = jnp.maximum(m, jnp.max(s, axis=0, keepdims=True))
        return m_new, jnp.exp2(m - m_new), jnp.exp2(s - m_new).astype(BF16)

    def weighted(acc, alpha, pb, vts):
        acc = alpha * acc
        rows = pb.shape[0] // len(vts)
        for c, vt in enumerate(vts):
            acc = acc + jnp.dot(vt, pb[c * rows:(c + 1) * rows], preferred_element_type=F32)
        return acc

    return scores, softmax, weighted


def _attn_finish(lam, acc, g, tq, out_scale):
    o = acc[:ATTN_VD] / acc[ATTN_VD:ATTN_VD + 1]
    o = (o[:, :tq] - lam * o[:, tq:]).T
    ms = jnp.mean(o * o, axis=-1, keepdims=True)
    return (o * lax.rsqrt(ms + EPS) * g * out_scale).astype(BF16)


def _attn_ctx_kernel(lam_ref, q_ref, kc_ref, vc_ref, g_ref, o_ref, *, out_scale):
    tq = q_ref.shape[0]
    scores, softmax, weighted = _attn_stages(q_ref[...], tq)
    m = jnp.full((1, 2 * tq), NEG_BIG, F32)
    acc = jnp.zeros((ATTN_VROWS, 2 * tq), F32)
    m, alpha, pb = softmax(m, scores(kc_ref[...]))
    acc = weighted(acc, alpha, pb, [vc_ref[0]])
    o_ref[...] = _attn_finish(lam_ref[0], acc, g_ref[...], tq, out_scale)


def _attn_kernel(lam_ref, q_ref, kl_ref, kc_ref, vl_ref, vc_ref, g_ref, o_ref,
                 s_a, s_b, p_a, p_b, acc_ref, *, n_lat_k, tk, out_scale):
    tq = q_ref.shape[0]
    scores, softmax, weighted = _attn_stages(q_ref[...], tq)
    n_sub = tk // vc_ref.shape[-1]

    def stage_scores(t, s_buf):
        start = pl.multiple_of(t * tk, tk)
        s_buf[...] = scores(kl_ref[pl.ds(start, tk), :])

    def stage_softmax(m, s_buf, p_buf):
        m, alpha, pb = softmax(m, s_buf[...])
        p_buf[...] = pb
        return m, alpha

    def stage_values(alpha, p_buf, t):
        acc_ref[...] = weighted(acc_ref[...], alpha, p_buf[...],
                                [vl_ref[t * n_sub + c] for c in range(n_sub)])

    m = jnp.full((1, 2 * tq), NEG_BIG, F32)
    m, alpha, pb = softmax(m, scores(kc_ref[...]))
    acc_ref[...] = weighted(jnp.zeros((ATTN_VROWS, 2 * tq), F32), alpha, pb, [vc_ref[0]])

    if n_lat_k % 2 == 0:
        stage_scores(0, s_a)
        stage_scores(1, s_b)
        m, alpha = stage_softmax(m, s_a, p_a)

        def pair(i, carry):
            m, alpha = carry
            k = 2 * i
            stage_scores(k + 2, s_a)
            m, alpha_n = stage_softmax(m, s_b, p_b)
            stage_values(alpha, p_a, k)
            stage_scores(k + 3, s_b)
            m, alpha_nn = stage_softmax(m, s_a, p_a)
            stage_values(alpha_n, p_b, k + 1)
            return m, alpha_nn

        m, alpha = lax.fori_loop(0, n_lat_k // 2 - 1, pair, (m, alpha))
        m, alpha_n = stage_softmax(m, s_b, p_b)
        stage_values(alpha, p_a, n_lat_k - 2)
        stage_values(alpha_n, p_b, n_lat_k - 1)
    else:
        for t in range(n_lat_k):
            stage_scores(t, s_a)
            m, alpha = stage_softmax(m, s_a, p_a)
            stage_values(alpha, p_a, t)
    o_ref[...] = _attn_finish(lam_ref[0], acc_ref[...], g_ref[...], tq, out_scale)


def _attention(q, k, vt, lam, g, nb, l_lat, l_ctx, with_ctx_queries, out_scale):
    tq_ctx = ROW_TILE
    tq = 512 if l_lat % 512 == 0 else ROW_TILE
    tk = 1024 if l_lat % 2048 == 0 else (512 if l_lat % 512 == 0 else l_lat)
    vt_tile = vt.shape[-1]
    assert l_ctx == tq_ctx and l_lat % tq == 0 and vt_tile == l_ctx and tk % vt_tile == 0
    n_lat_q = l_lat // tq
    n_vt_lat = l_lat // vt_tile
    ctx_blk0 = nb * l_lat // l_ctx
    smem = pl.BlockSpec(memory_space=pltpu.SMEM)
    gain = pl.BlockSpec((1, LANE), lambda b, h, i: (0, 0))
    kc_spec = pl.BlockSpec((l_ctx, LANE), lambda b, h, i: (ctx_blk0 + b, h))
    vc_spec = pl.BlockSpec((1, ATTN_VROWS, vt_tile), lambda b, h, i: (nb * n_vt_lat + b, h, 0))
    sem = _cparams(("arbitrary", "arbitrary", "arbitrary"))
    lat = pl.pallas_call(
        functools.partial(_attn_kernel, n_lat_k=l_lat // tk, tk=tk, out_scale=out_scale),
        grid=(nb, ATTN_HEADS, n_lat_q),
        in_specs=[smem,
                  pl.BlockSpec((tq, LANE), lambda b, h, i: (b * n_lat_q + i, h)),
                  pl.BlockSpec((l_lat, LANE), lambda b, h, i: (b, h)),
                  kc_spec,
                  pl.BlockSpec((n_vt_lat, ATTN_VROWS, vt_tile), lambda b, h, i: (b, h, 0)),
                  vc_spec, gain],
        out_specs=pl.BlockSpec((tq, LANE), lambda b, h, i: (b * n_lat_q + i, h)),
        out_shape=jax.ShapeDtypeStruct((nb * l_lat, ATTN_W), BF16),
        scratch_shapes=[pltpu.VMEM((tk, 2 * tq), F32), pltpu.VMEM((tk, 2 * tq), F32),
                        pltpu.VMEM((tk, 2 * tq), BF16), pltpu.VMEM((tk, 2 * tq), BF16),
                        pltpu.VMEM((ATTN_VROWS, 2 * tq), F32)],
        compiler_params=sem,
        name="diff_attention",
    )(lam, q, k, k, vt, vt, g)
    if not with_ctx_queries:
        return lat, None
    ctx = pl.pallas_call(
        functools.partial(_attn_ctx_kernel, out_scale=out_scale),
        grid=(nb, ATTN_HEADS, 1),
        in_specs=[smem,
                  pl.BlockSpec((tq_ctx, LANE), lambda b, h, i: (ctx_blk0 + b, h)),
                  kc_spec, vc_spec, gain],
        out_specs=pl.BlockSpec((tq_ctx, LANE), lambda b, h, i: (b, h)),
        out_shape=jax.ShapeDtypeStruct((nb * l_ctx, ATTN_W), BF16),
        compiler_params=sem,
        name="diff_attention_ctx",
    )(lam, q, k, vt, g)
    return lat, ctx


def _ssm_kernel(ul_ref, uc_ref, m_ref, bm_ref, cm_ref, a_ref, d_ref, yl_ref, yc_ref,
                s_scr, h_scr, *, ctx_out):
    tc, pw = SSM_CHUNK, 2 * SSM_GROUP
    nl = ul_ref.shape[0] // tc
    nc = uc_ref.shape[0] // tc
    w = m_ref.shape[-1]
    half = s_scr.shape[-1] // 2
    sub = pl.program_id(1) % (LANE // pw)
    nt_dims = (((1,), (1,)), ((), ()))
    ch = lax.broadcasted_iota(jnp.int32, (LANE, 1), 0) - sub * pw
    valid = (ch >= 0) & (ch < pw)
    tgt0 = (ch >> 4) * (tc * SSM_GROUP) + (ch & (SSM_GROUP - 1))
    col = lax.broadcasted_iota(jnp.int32, (1, w), 1)

    def perm(t):
        return jnp.where(valid & (col == tgt0 + t * SSM_GROUP), 1.0, 0.0).astype(BF16)

    def fold(ref, n):
        acc = jnp.zeros((n, w), F32)
        for t in range(tc):
            acc = acc + jnp.dot(ref[pl.ds(t, n, stride=tc), :].astype(BF16), perm(t),
                                preferred_element_type=F32)
        return acc.astype(BF16)

    def unfold_add(y, ref, n):
        hi = y.astype(BF16)
        r1 = y - hi.astype(F32)
        mid = r1.astype(BF16)
        lo = (r1 - mid.astype(F32)).astype(BF16)
        for t in range(tc):
            p = perm(t)
            o = sum(lax.dot_general(part, p, nt_dims, preferred_element_type=F32)
                    for part in (hi, mid, lo))
            ref[pl.ds(t, n, stride=tc), :] += o

    @pl.when(sub == 0)
    def _():
        yl_ref[...] = ul_ref[...] * d_ref[...]
        yc_ref[...] = uc_ref[...] * d_ref[...] if ctx_out else jnp.zeros_like(yc_ref)

    ulb = fold(ul_ref, nl)
    ucb = fold(uc_ref, nc)
    yl = jnp.dot(ulb, m_ref[0], preferred_element_type=F32)
    if ctx_out:
        yc = jnp.dot(ucb, m_ref[0], preferred_element_type=F32)
    for dr in range(2):
        s_scr[dr, 0:nc, :] = jnp.dot(ucb, bm_ref[dr, 0], preferred_element_type=F32)
        s_scr[dr, nc:nc + nl, :] = jnp.dot(ulb, bm_ref[dr, 0], preferred_element_type=F32)
    a = a_ref[0]
    afr, afi, arr, ari = a[0:1], a[1:2], a[2:3], a[3:4]

    def make_body(base, n):
        def body(t, carry):
            fr, fi, rr, ri = carry
            rf = base + t
            rv = base + n - 1 - t
            h_scr[0, pl.ds(rf, 1), :] = jnp.concatenate([fr, fi], axis=-1)
            h_scr[1, pl.ds(rv, 1), :] = jnp.concatenate([rr, ri], axis=-1)
            sf = s_scr[0, pl.ds(rf, 1), :]
            sv = s_scr[1, pl.ds(rv, 1), :]
            nfr = afr * fr - afi * fi + sf[:, :half]
            nfi = afr * fi + afi * fr + sf[:, half:]
            nrr = arr * rr - ari * ri + sv[:, :half]
            nri = arr * ri + ari * rr + sv[:, half:]
            return nfr, nfi, nrr, nri
        return body

    z = jnp.zeros((1, half), F32)
    carry = lax.fori_loop(0, nc, make_body(0, nc), (z, z, z, z))
    lax.fori_loop(0, nl, make_body(nc, nl), carry)
    for dr in range(2):
        yl = yl + jnp.dot(h_scr[dr, nc:nc + nl, :].astype(BF16), cm_ref[dr, 0],
                          preferred_element_type=F32)
        if ctx_out:
            yc = yc + jnp.dot(h_scr[dr, 0:nc, :].astype(BF16), cm_ref[dr, 0],
                              preferred_element_type=F32)
    unfold_add(yl, yl_ref, nl)
    if ctx_out:
        unfold_add(yc, yc_ref, nc)


def _ssm_matrices(lam_re, lam_im, log_step, b_re, b_im, c_re, c_im):
    tc = SSM_CHUNK
    g, p, hh = SSM_GROUPS, SSM_STATE, SSM_GROUP
    npair = g // 2
    step = jnp.exp(log_step)[..., None]
    den = lam_re * lam_re + lam_im * lam_im

    def power(k):
        er = jnp.exp(lam_re * step * k)
        return er * jnp.cos(lam_im * step * k), er * jnp.sin(lam_im * step * k)

    ar, ai = power(1.0)
    nr = ar - 1.0
    cr_ = (nr * lam_re + ai * lam_im) / den
    ci_ = (ai * lam_re - nr * lam_im) / den
    bbr = cr_[..., None] * b_re - ci_[..., None] * b_im
    bbi = cr_[..., None] * b_im + ci_[..., None] * b_re
    ks = jnp.arange(tc + 1, dtype=F32)
    pw = jax.vmap(power)(ks)
    pwr, pwi = pw
    ct_re = jnp.swapaxes(c_re, -1, -2)
    ct_im = jnp.swapaxes(c_im, -1, -2)
    cb_re = (ct_re[..., :, None] * bbr[..., None, :] - ct_im[..., :, None] * bbi[..., None, :])
    cb_im = (ct_im[..., :, None] * bbr[..., None, :] + ct_re[..., :, None] * bbi[..., None, :])
    hp = lax.Precision.HIGHEST
    kk = (jnp.einsum('kdgp,dgpq->kdgq', pwr[:tc], cb_re.reshape(2, g, p, hh * hh), precision=hp)
          - jnp.einsum('kdgp,dgpq->kdgq', pwi[:tc], cb_im.reshape(2, g, p, hh * hh), precision=hp))
    kk = kk.reshape(tc, 2, g, hh, hh)
    zpad = jnp.zeros((tc - 1,) + kk.shape[2:], F32)
    lagtab = (jnp.concatenate([zpad, kk[:, 0]], 0)
              + jnp.concatenate([kk[::-1, 1], zpad], 0))
    lagtab = jnp.transpose(lagtab, (1, 3, 0, 2)).reshape(g, hh, (2 * tc - 1) * hh)
    mm = jnp.stack([lagtab[:, :, (tc - 1 - s) * hh:(2 * tc - 1 - s) * hh] for s in range(tc)], axis=1)
    mm = mm.reshape(g, tc * hh, tc * hh)
    zero = jnp.zeros_like(mm[0::2])
    m2 = jnp.concatenate([jnp.concatenate([mm[0::2], zero], -1),
                          jnp.concatenate([zero, mm[1::2]], -1)], -2)
    def bmat(dr, exps):
        pr = pwr[exps, dr]
        pi = pwi[exps, dr]
        re = pr[..., None] * bbr[dr][None] - pi[..., None] * bbi[dr][None]
        im = pr[..., None] * bbi[dr][None] + pi[..., None] * bbr[dr][None]
        re = jnp.transpose(re, (1, 0, 3, 2)).reshape(g, tc * hh, p)
        im = jnp.transpose(im, (1, 0, 3, 2)).reshape(g, tc * hh, p)
        z = jnp.zeros_like(re[0::2])
        top = jnp.concatenate([re[0::2], z, im[0::2], z], -1)
        bot = jnp.concatenate([z, re[1::2], z, im[1::2]], -1)
        return jnp.concatenate([top, bot], -2)
    bm = jnp.stack([bmat(0, jnp.arange(tc - 1, -1, -1)), bmat(1, jnp.arange(tc))])
    def cmat(dr, exps):
        pr = jnp.transpose(pwr[exps, dr], (1, 2, 0))[..., None]
        pi = jnp.transpose(pwi[exps, dr], (1, 2, 0))[..., None]
        cr, ci = ct_re[dr][:, :, None, :], ct_im[dr][:, :, None, :]
        re = (cr * pr - ci * pi).reshape(g, p, tc * hh)
        im = (cr * pi + ci * pr).reshape(g, p, tc * hh)
        z = jnp.zeros_like(re[0::2])
        return jnp.concatenate([jnp.concatenate([re[0::2], z], -1),
                                jnp.concatenate([z, re[1::2]], -1),
                                jnp.concatenate([-im[0::2], z], -1),
                                jnp.concatenate([z, -im[1::2]], -1)], -2)
    cm = jnp.stack([cmat(0, jnp.arange(1, tc + 1)), cmat(1, jnp.arange(tc, 0, -1))])
    a16r = pwr[tc].reshape(2, npair, 2 * p)
    a16i = pwi[tc].reshape(2, npair, 2 * p)
    a16 = jnp.stack([a16r[0], a16i[0], a16r[1], a16i[1]], axis=1)
    return m2.astype(BF16), bm.astype(BF16), cm.astype(BF16), a16


def _ssm(us, mats, d, nb, l_lat, l_ctx, ctx_out):
    m2, bm, cm, a16 = mats
    tc, hh = SSM_CHUNK, SSM_GROUP
    npair = SSM_GROUPS // 2
    w = 2 * tc * hh
    per_col = LANE // (2 * hh)
    nl, nc = l_lat // tc, l_ctx // tc
    ctx_blk0 = nb * l_lat // l_ctx
    yl, yc = pl.pallas_call(
        functools.partial(_ssm_kernel, ctx_out=ctx_out),
        grid=(nb, npair),
        in_specs=[pl.BlockSpec((l_lat, LANE), lambda b, q: (b, q // per_col)),
                  pl.BlockSpec((l_ctx, LANE), lambda b, q: (ctx_blk0 + b, q // per_col)),
                  pl.BlockSpec((1, w, w), lambda b, q: (q, 0, 0)),
                  pl.BlockSpec((2, 1, w, 4 * SSM_STATE), lambda b, q: (0, q, 0, 0)),
                  pl.BlockSpec((2, 1, 4 * SSM_STATE, w), lambda b, q: (0, q, 0, 0)),
                  pl.BlockSpec((1, 4, 2 * SSM_STATE), lambda b, q: (q, 0, 0)),
                  pl.BlockSpec((1, LANE), lambda b, q: (0, q // per_col))],
        out_specs=[pl.BlockSpec((l_lat, LANE), lambda b, q: (b, q // per_col)),
                   pl.BlockSpec((l_ctx, LANE), lambda b, q: (b, q // per_col))],
        out_shape=[jax.ShapeDtypeStruct((nb * l_lat, SSM_W), F32),
                   jax.ShapeDtypeStruct((nb * l_ctx, SSM_W), F32)],
        scratch_shapes=[pltpu.VMEM((2, nc + nl, 4 * SSM_STATE), F32),
                        pltpu.VMEM((2, nc + nl, 4 * SSM_STATE), F32)],
        compiler_params=_cparams(("arbitrary", "arbitrary")),
        name="s5_scan",
    )(us, us, m2, bm, cm, a16, d)
    return yl, (yc if ctx_out else None)


def _pool_kernel(prev_ref, cur_ref, next_ref, w_ref, scale_ref, o_ref, ext,
                 *, n_lat_tiles, tiles_per_lat, l_lat, l_ctx):
    r = cur_ref.shape[0]
    hl = POOL_HALO
    i = pl.program_id(0)
    is_lat = i < n_lat_tiles
    pos = jnp.where(is_lat, i % tiles_per_lat, 0)
    n_tiles = jnp.where(is_lat, tiles_per_lat, l_ctx // r)
    seq_len = jnp.where(is_lat, l_lat, l_ctx)
    zeros = jnp.zeros((hl, POOL_W), F32)
    ext[0:hl, :] = jnp.where(pos > 0, prev_ref[...], zeros)
    ext[hl:hl + r, :] = cur_ref[...]
    ext[hl + r:hl + r + hl, :] = jnp.where(pos < n_tiles - 1, next_ref[...], zeros)
    u = cur_ref[...]
    t = pos * r + lax.broadcasted_iota(jnp.int32, (r, 1), 0)
    lane = lax.broadcasted_iota(jnp.int32, (r, POOL_W), 1)

    def win(k):
        return ext[hl + k:hl + k + r, :]

    acc = win(-1) + u
    mean = jnp.zeros((r, POOL_W), F32)
    lo_k, hi_k = -1, 0
    for gi, wn in enumerate(POOL_WINDOWS):
        hw = wn // 2
        while lo_k > -hw:
            lo_k -= 1
            acc = acc + win(lo_k)
        while hi_k < hw - 1:
            hi_k += 1
            acc = acc + win(hi_k)
        cnt = (jnp.minimum(t + hw, seq_len) - jnp.maximum(t - hw, 0)).astype(F32)
        sel = (lane >= gi * POOL_GROUP) & (lane < (gi + 1) * POOL_GROUP)
        mean = jnp.where(sel, acc / cnt, mean)
    dlt = (mean - u).astype(BF16)
    o_ref[...] = jnp.dot(dlt, w_ref[...], preferred_element_type=F32) * scale_ref[...]


def _pool(up, w_blk, scale, nb, l_lat, l_ctx, n_rows):
    r = ROW_TILE
    hl = POOL_HALO
    n_tiles = n_rows // r
    last8 = up.shape[0] // hl - 1
    kern = functools.partial(_pool_kernel, n_lat_tiles=nb * l_lat // r, tiles_per_lat=l_lat // r,
                             l_lat=l_lat, l_ctx=l_ctx)
    return pl.pallas_call(
        kern,
        grid=(n_tiles,),
        in_specs=[pl.BlockSpec((hl, POOL_W), lambda i: (jnp.maximum(i * (r // hl) - 1, 0), 0)),
                  pl.BlockSpec((r, POOL_W), lambda i: (i, 0)),
                  pl.BlockSpec((hl, POOL_W), lambda i: (jnp.minimum((i + 1) * (r // hl), last8), 0)),
                  pl.BlockSpec((POOL_W, POOL_W), lambda i: (0, 0)),
                  pl.BlockSpec((1, POOL_W), lambda i: (0, 0))],
        out_specs=pl.BlockSpec((r, POOL_W), lambda i: (i, 0)),
        out_shape=jax.ShapeDtypeStruct((n_rows, POOL_W), F32),
        scratch_shapes=[pltpu.VMEM((r + 2 * hl, POOL_W), F32)],
        compiler_params=_cparams(("arbitrary",)),
        name="pool_mix",
    )(up, up, up, w_blk, scale)


def _mixout_kernel(xa_ref, xb_ref, aa_ref, ab_ref, ya_ref, yb_ref, pool_ref, mod_ref, wglu_ref,
                   wout_ref, o_ref, *, n_a, n_lat):
    yg = _gelu(_stream_tile(ya_ref, yb_ref, n_lat))
    z = yg * jax.nn.sigmoid(jnp.dot(yg.astype(BF16), wglu_ref[...], preferred_element_type=F32))
    r = jnp.dot(_stream_tile(aa_ref, ab_ref, n_lat), wout_ref[0:ATTN_W, :],
                preferred_element_type=F32)
    r = r + jnp.dot(z.astype(BF16), wout_ref[ATTN_W:ATTN_W + SSM_W, :], preferred_element_type=F32)
    r = r + jnp.dot(pool_ref[...].astype(BF16), wout_ref[ATTN_W + SSM_W:, :],
                    preferred_element_type=F32)
    o_ref[...] = _stream_tile(xa_ref, xb_ref, n_a) + mod_ref[0, 2:3, :] * r


def _mixout(xa, xb, n_a, attn, y, pool, mods, wglu, wout, tiles_per_batch, nb, n_rows):
    d = xa.shape[1]
    tm = ROW_TILE
    seg = _seg_map(tiles_per_batch, nb)
    row = lambda i: (i, 0)
    n_lat = nb * tiles_per_batch
    attn = (attn[0], attn[0] if attn[1] is None else attn[1])
    y = (y[0], y[0] if y[1] is None else y[1])
    return pl.pallas_call(
        functools.partial(_mixout_kernel, n_a=n_a, n_lat=n_lat),
        grid=(n_rows // tm,),
        in_specs=_stream_specs(xa, xb, n_a, tm) + _stream_specs(*attn, n_lat, tm)
                 + _stream_specs(*y, n_lat, tm) + [
                  pl.BlockSpec((tm, POOL_W), row),
                  pl.BlockSpec((1, N_MOD, d), lambda i: (seg(i), 0, 0)),
                  pl.BlockSpec((SSM_W, SSM_W), lambda i: (0, 0)),
                  pl.BlockSpec((d, d), lambda i: (0, 0))],
        out_specs=pl.BlockSpec((tm, d), row),
        out_shape=jax.ShapeDtypeStruct((n_rows, d), F32),
        compiler_params=_cparams(("arbitrary",)),
        name="mix_out",
    )(xa, xb, *attn, *y, pool, mods, wglu, wout)


def _oddeven_merge(lo, hi, r):
    step = r * 2
    if step < hi - lo:
        yield from _oddeven_merge(lo, hi, step)
        yield from _oddeven_merge(lo + r, hi, step)
        yield from [(i, i + r) for i in range(lo + r, hi - r, step)]
    else:
        yield (lo, lo + r)


def _oddeven_sort(lo, hi):
    if hi - lo >= 1:
        mid = lo + (hi - lo) // 2
        yield from _oddeven_sort(lo, mid)
        yield from _oddeven_sort(mid + 1, hi)
        yield from _oddeven_merge(lo, hi, 1)


_SORT16 = tuple(_oddeven_sort(0, PEER_TOPK - 1))
_BITONIC16 = tuple((i, i + d) for d in (8, 4, 2, 1) for i in range(PEER_TOPK) if not i & d)


def _compare_exchange(rows, net):
    rows = list(rows)
    for i, j in net:
        hi, lo = jnp.maximum(rows[i], rows[j]), jnp.minimum(rows[i], rows[j])
        rows[i], rows[j] = hi, lo
    return rows


def _top_sorted(s, k):
    assert k == PEER_TOPK and s.shape[0] == 8 * PEER_TOPK
    rows = _compare_exchange([s[8 * v:8 * v + 8] for v in range(PEER_TOPK)], _SORT16)
    for shift in (4, 2, 1):
        other = [pltpu.roll(r, shift, 0) for r in rows]
        rows = _compare_exchange([jnp.maximum(rows[i], other[PEER_TOPK - 1 - i])
                                  for i in range(PEER_TOPK)], _BITONIC16)
    return [r[0:1] for r in rows]


def _peer_score_kernel(x_ref, g_ref, mod_ref, wq_ref, keys_ref,
                       h_ref, beta_ref, s2_ref, e1_ref, e2_ref, q_scr):
    tm = x_ref.shape[0]
    h = _norm_mod(x_ref[...], g_ref[...], mod_ref[0, 3:4, :], mod_ref[0, 4:5, :]).astype(BF16)
    h_ref[...] = h
    q = jnp.dot(h, wq_ref[...], preferred_element_type=F32)
    for hc in range(2 * PEER_HEADS):
        q_scr[hc] = q[:, hc * PEER_KDIM:(hc + 1) * PEER_KDIM].astype(BF16)
    row8 = lax.broadcasted_iota(jnp.int32, (8, tm), 0)

    def head(hd, _):
        nt = (((1,), (1,)), ((), ()))
        s1 = lax.dot_general(keys_ref[hd, 0], q_scr[2 * hd], nt,
                             preferred_element_type=F32)
        s2 = lax.dot_general(keys_ref[hd, 1], q_scr[2 * hd + 1], nt,
                             preferred_element_type=F32)
        a = _top_sorted(s1, PEER_TOPK)
        b = _top_sorted(s2, PEER_TOPK)
        acat = jnp.concatenate(a, axis=0)
        bcat = jnp.concatenate(b, axis=0)
        pieces = [a[0] + bcat]
        for i in range(2, 9):
            piece = a[i - 1] + bcat[0:8]
            n_valid = PEER_TOPK // i
            pieces.append(piece if n_valid >= 8 else jnp.where(row8 < n_valid, piece, NEG_BIG))
        pieces.append(acat[8:16] + b[0])
        cand = jnp.concatenate(pieces, axis=0)
        work = cand
        cum = jnp.zeros((1, tm), F32)
        tau = jnp.full((1, tm), NEG_BIG, F32)
        for _ in range(PEER_TOPK):
            m = jnp.max(work, axis=0, keepdims=True)
            eq = work == m
            new = cum + jnp.sum(eq.astype(F32), axis=0, keepdims=True)
            tau = jnp.where((cum < PEER_TOPK) & (new >= PEER_TOPK), m, tau)
            work = jnp.where(eq, NEG_BIG, work)
            cum = new
        top = a[0] + b[0]
        z = jnp.sum(jnp.where(cand >= tau, jnp.exp(cand - top), 0.0), axis=0, keepdims=True)
        big = -NEG_BIG
        b_rows = [bcat] + [bcat[0:8]] * 7
        beta_r = [jnp.min(jnp.where(pc >= tau, br, big), axis=0, keepdims=True)
                  for pc, br in zip(pieces[:8], b_rows)]
        tail = jnp.where(pieces[8] >= tau, b[0], big)
        beta_r += [tail[r:r + 1] for r in range(8)]
        beta = jnp.full_like(s1, big)
        for r in range(PEER_TOPK):
            beta = jnp.where(s1 == a[r], beta_r[r], beta)
        beta_ref[hd] = beta
        s2_ref[hd] = s2
        e1_ref[hd] = jnp.exp(s1 - a[0])
        e2_ref[hd] = jnp.exp(s2 - b[0]) * (1.0 / z)
        return 0

    lax.fori_loop(0, PEER_HEADS, head, 0, unroll=4)


def _peer_scores(x, g, mods, wq, keys, tiles_per_batch, nb, n_rows):
    d = x.shape[1]
    tm = ROW_TILE
    seg = _seg_map(tiles_per_batch, nb)
    nq = wq.shape[1]
    col = lambda i: (0, 0, i)
    big = jax.ShapeDtypeStruct((PEER_HEADS, PEER_NKEYS, n_rows), F32)
    return pl.pallas_call(
        _peer_score_kernel,
        grid=(n_rows // tm,),
        in_specs=[pl.BlockSpec((tm, d), lambda i: (i, 0)),
                  pl.BlockSpec((1, d), lambda i: (0, 0)),
                  pl.BlockSpec((1, N_MOD, d), lambda i: (seg(i), 0, 0)),
                  pl.BlockSpec((d, nq), lambda i: (0, 0)),
                  pl.BlockSpec((PEER_HEADS, 2, PEER_NKEYS, PEER_KDIM), lambda i: (0, 0, 0, 0))],
        out_specs=[pl.BlockSpec((tm, d), lambda i: (i, 0)),
                   pl.BlockSpec((PEER_HEADS, PEER_NKEYS, tm), col),
                   pl.BlockSpec((PEER_HEADS, PEER_NKEYS, tm), col),
                   pl.BlockSpec((PEER_HEADS, PEER_NKEYS, tm), col),
                   pl.BlockSpec((PEER_HEADS, PEER_NKEYS, tm), col)],
        out_shape=[jax.ShapeDtypeStruct((n_rows, d), BF16), big, big, big, big],
        scratch_shapes=[pltpu.VMEM((2 * PEER_HEADS, tm, PEER_KDIM), BF16)],
        compiler_params=_cparams(("arbitrary",)),
        name="peer_scores",
    )(x, g, mods, wq, keys)


def _transpose_cast_kernel(x_ref, o_ref):
    o_ref[...] = x_ref[...].T.astype(o_ref.dtype)


def _transpose_cast(x, dtype, rows_per_step=512):
    n, d = x.shape
    return pl.pallas_call(
        _transpose_cast_kernel,
        grid=(n // rows_per_step,),
        in_specs=[pl.BlockSpec((rows_per_step, d), lambda i: (i, 0))],
        out_specs=pl.BlockSpec((d, rows_per_step), lambda i: (0, i)),
        out_shape=jax.ShapeDtypeStruct((d, n), dtype),
        compiler_params=_cparams(("arbitrary",)),
        name="transpose_cast",
    )(x)


PEER_CHUNK_KEYS = 8
PEER_TOK_TILE = 512
PEER_SUB = 256


def _peer_dense_kernel(h_ref, x_ref, mod_ref, u_ref, vt_ref, beta_ref, s2_ref, e1_ref, e2_ref,
                       fg_ref, o_ref, acc_ref, a0, a1, w0, w1, row_scr,
                       *, n_chunks, final_norm):
    n_i = PEER_CHUNK_KEYS
    tm = h_ref.shape[0]
    g = pl.program_id(0)
    nt_dims = (((1,), (1,)), ((), ()))
    n_k = u_ref.shape[1] // PEER_SUB
    n_lb = tm // LANE
    assert n_k == n_lb == vt_ref.shape[1] // PEER_SUB and n_i * PEER_NKEYS == u_ref.shape[0]
    n_mt = u_ref.shape[0] // PEER_SUB

    @pl.when(g == 0)
    def _():
        acc_ref[...] = jnp.zeros_like(acc_ref)
        a1[...] = jnp.zeros_like(a1)
        w0[...] = jnp.zeros_like(w0)

    for hd in range(PEER_HEADS):
        betab = beta_ref[hd]
        e1b = e1_ref[hd]
        for ii in range(n_i):
            row_scr[hd, ii, 0, :, 0:tm] = jnp.broadcast_to(betab[ii:ii + 1], (8, tm))
            row_scr[hd, ii, 1, :, 0:tm] = jnp.broadcast_to(e1b[ii:ii + 1], (8, tm))

    def run(a_wr, a_rd, w_wr, w_rd):
        n_trips = tm // PEER_SUB
        slabs_per_trip = PEER_NKEYS // 16 // n_trips
        for lb in range(n_lb):
            th, lo = divmod(lb * LANE, PEER_SUB)
            ls = slice(lb * LANE, (lb + 1) * LANE)
            lh = slice(lo, lo + LANE)
            ms = slice(lb * PEER_SUB, (lb + 1) * PEER_SUB)

            def trip(tr, _):
                t0 = pl.multiple_of(tr * PEER_SUB, PEER_SUB)
                part_a = lax.dot_general(u_ref[ms, :], h_ref[pl.ds(t0, PEER_SUB), :], nt_dims,
                                         preferred_element_type=F32)
                part_v = jnp.dot(vt_ref[ms, :], w_rd[tr], preferred_element_type=F32)
                for sl in range(slabs_per_trip):
                    j0 = pl.multiple_of((tr * slabs_per_trip + sl) * 16, 16)
                    gs = [[jnp.zeros((8, LANE), F32), jnp.zeros((8, LANE), F32)] for _ in range(n_i)]
                    for hd in range(PEER_HEADS):
                        s2v = s2_ref[hd, pl.ds(j0, 16), ls]
                        e2v = e2_ref[hd, pl.ds(j0, 16), ls]
                        for ii in range(n_i):
                            bs = row_scr[hd, ii, 0, :, ls]
                            be = row_scr[hd, ii, 1, :, ls]
                            for hf in range(2):
                                hs = slice(hf * 8, (hf + 1) * 8)
                                gs[ii][hf] = gs[ii][hf] + jnp.where(s2v[hs] >= bs, e2v[hs], 0.0) * be
                    for ii in range(n_i):
                        r0 = pl.multiple_of(ii * PEER_NKEYS + j0, 16)
                        g16 = jnp.concatenate(gs[ii], axis=0)
                        w_wr[th, pl.ds(r0, 16), lh] = (_gelu(a_rd[th, pl.ds(r0, 16), lh]) * g16).astype(BF16)
                a_wr[tr, ms, :] = part_a
                acc_ref[tr, ms, :] += part_v
                return 0

            lax.fori_loop(0, n_trips, trip, 0, unroll=True)

    @pl.when(g % 2 == 0)
    def _():
        run(a0, a1, w1, w0)

    @pl.when(g % 2 == 1)
    def _():
        run(a1, a0, w0, w1)

    @pl.when((g >= 2) & ((g - 2) % n_chunks == n_chunks - 1))
    def _():
        out = x_ref[...] + mod_ref[0, 5:6, :] * jnp.concatenate(
            [acc_ref[t].T for t in range(acc_ref.shape[0])], axis=0)
        if final_norm:
            ms = jnp.mean(out * out, axis=-1, keepdims=True)
            out = out * lax.rsqrt(ms + EPS) * fg_ref[...]
        o_ref[...] = out
        acc_ref[...] = jnp.zeros_like(acc_ref)


def _peer_dense(h, x, mods, u, vt, beta, s2, e1, e2, fg, tiles_per_batch, nb, n_rows, final_norm):
    d = x.shape[1]
    tm = PEER_TOK_TILE
    n_i = PEER_CHUNK_KEYS
    ne = n_i * PEER_NKEYS
    tiles_per_batch = tiles_per_batch * ROW_TILE // tm
    n_chunks = u.shape[0] // ne
    n_steps = (n_rows // tm) * n_chunks
    assert n_rows % tm == 0 and d == ne
    assert PEER_NKEYS // 16 == (tm // PEER_SUB) * (ne // PEER_SUB)

    def stage(lag):
        def split(g):
            n = jnp.clip(g - lag, 0, n_steps - 1)
            return n // n_chunks, n % n_chunks
        return split

    act, gate, val = stage(0), stage(1), stage(2)
    seg = lambda i: jnp.minimum(i // tiles_per_batch, nb)
    kern = functools.partial(_peer_dense_kernel, n_chunks=n_chunks, final_norm=final_norm)
    full = (PEER_HEADS, PEER_NKEYS, tm)
    rows = (PEER_HEADS, n_i, tm)
    return pl.pallas_call(
        kern,
        grid=(n_steps + 2,),
        in_specs=[pl.BlockSpec((tm, d), lambda g: (act(g)[0], 0)),
                  pl.BlockSpec((tm, d), lambda g: (val(g)[0], 0)),
                  pl.BlockSpec((1, N_MOD, d), lambda g: (seg(val(g)[0]), 0, 0)),
                  pl.BlockSpec((ne, d), lambda g: (act(g)[1], 0)),
                  pl.BlockSpec((d, ne), lambda g: (0, val(g)[1])),
                  pl.BlockSpec(rows, lambda g: (0, gate(g)[1], gate(g)[0])),
                  pl.BlockSpec(full, lambda g: (0, 0, gate(g)[0])),
                  pl.BlockSpec(rows, lambda g: (0, gate(g)[1], gate(g)[0])),
                  pl.BlockSpec(full, lambda g: (0, 0, gate(g)[0])),
                  pl.BlockSpec((1, d), lambda g: (0, 0))],
        out_specs=pl.BlockSpec((tm, d), lambda g: (val(g)[0], 0)),
        out_shape=jax.ShapeDtypeStruct((n_rows, d), F32),
        scratch_shapes=[pltpu.VMEM((tm // PEER_SUB, d, PEER_SUB), F32),
                        pltpu.VMEM((tm // PEER_SUB, ne, PEER_SUB), F32),
                        pltpu.VMEM((tm // PEER_SUB, ne, PEER_SUB), F32),
                        pltpu.VMEM((tm // PEER_SUB, ne, PEER_SUB), BF16),
                        pltpu.VMEM((tm // PEER_SUB, ne, PEER_SUB), BF16),
                        pltpu.VMEM((PEER_HEADS, n_i, 2, 8, tm + LANE), F32)],
        compiler_params=_cparams(("arbitrary",)),
        name="peer_experts",
    )(h, x, mods, u, vt, beta, s2, e1, e2, fg)


def _rope_tables(l_lat, nb, n_ctx_rows):
    rows = l_lat // GRID_W
    r = jnp.repeat(jnp.arange(rows), GRID_W)
    col = jnp.tile(jnp.arange(GRID_W), rows)
    pos = jnp.stack([r, col], axis=-1).astype(F32)
    nf = ATTN_HD // 4
    inv = 1.0 / (ROPE_BASE ** (jnp.arange(nf, dtype=F32) / nf))
    ang = pos[:, :, None] * inv
    cos, sin = jnp.cos(ang), jnp.sin(ang)
    c64 = jnp.concatenate([cos[:, 0], cos[:, 0], cos[:, 1], cos[:, 1]], axis=-1)
    s64 = jnp.concatenate([-sin[:, 0], sin[:, 0], -sin[:, 1], sin[:, 1]], axis=-1)
    c = jnp.tile(c64, (nb, LANE // ATTN_HD))
    s = jnp.tile(s64, (nb, LANE // ATTN_HD))
    c = jnp.concatenate([c, jnp.ones((n_ctx_rows, LANE), F32)], axis=0)
    s = jnp.concatenate([s, jnp.zeros((n_ctx_rows, LANE), F32)], axis=0)
    return c, s


def kernel(x, c, ctx, c_ctx, w_mod, b_mod, norm1_g, norm2_g, w_in, w_out, lam_q1, lam_k1, lam_q2, lam_k2, subln_g, ssm_lambda_re, ssm_lambda_im, ssm_log_step, ssm_b_re, ssm_b_im, ssm_c_re, ssm_c_im, ssm_d, ssm_w_glu, pool_w, pool_scale, peer_wq, peer_keys, peer_u, peer_v, final_g):
    nb, l_lat, d = x.shape
    l_ctx = ctx.shape[1]
    depth = w_mod.shape[0]
    n_lat = nb * l_lat
    n_all = n_lat + nb * l_ctx
    tiles_per_batch = l_lat // ROW_TILE
    assert l_lat % ROW_TILE == 0 and l_ctx == ROW_TILE and nb + 1 <= 8

    cs = jnp.concatenate([c, c_ctx[None], jnp.zeros((8 - nb - 1, d), F32)], axis=0)
    mod_all = _mod_vectors(cs, w_mod, b_mod)
    cos_t, sin_t = _rope_tables(l_lat, 1, ROW_TILE)
    ssm_mats = jax.vmap(_ssm_matrices)(ssm_lambda_re, ssm_lambda_im, ssm_log_step, ssm_b_re, ssm_b_im,
                                       ssm_c_re, ssm_c_im)
    stream = (x.reshape(n_lat, d), ctx.reshape(nb * l_ctx, d), n_lat // ROW_TILE)

    for l in range(depth):
        last = l == depth - 1
        n_rows = n_lat if last else n_all
        mods = mod_all[l, :nb + 1].reshape(nb + 1, N_MOD, d)
        q, k, v, us, up = _inproj(*stream, n_all, norm1_g[l][None], mods, w_in[l].astype(BF16),
                                  cos_t, sin_t, tiles_per_batch, nb)
        lam_init = 0.8 - 0.6 * math.exp(-0.3 * l)
        lam = (jnp.exp(jnp.sum(lam_q1[l] * lam_k1[l])) - jnp.exp(jnp.sum(lam_q2[l] * lam_k2[l]))
               + lam_init).reshape(1).astype(F32)
        attn = _attention(q, k, v, lam, subln_g[l][None], nb, l_lat, l_ctx, not last,
                          1.0 - lam_init)
        y = _ssm(us, tuple(m[l] for m in ssm_mats), ssm_d[l][None], nb, l_lat, l_ctx, not last)
        w_blk = jax.scipy.linalg.block_diag(*[pool_w[l, gi] for gi in range(len(POOL_WINDOWS))])
        pool = _pool(up, w_blk.astype(BF16), pool_scale[l][None], nb, l_lat, l_ctx, n_rows)
        xs = _mixout(*stream, attn, y, pool, mods, ssm_w_glu[l].astype(BF16), w_out[l].astype(BF16),
                     tiles_per_batch, nb, n_rows)
        h2, beta, s2, e1, e2 = _peer_scores(xs, norm2_g[l][None], mods, peer_wq[l].astype(BF16),
                                               peer_keys[l].astype(BF16), tiles_per_batch, nb, n_rows)
        xs = _peer_dense(h2, xs, mods, peer_u[l].astype(BF16), _transpose_cast(peer_v[l], BF16),
                         beta, s2, e1, e2, final_g[None], tiles_per_batch, nb, n_rows, last)
        stream = (xs, xs, n_rows // ROW_TILE)
    return xs.reshape(nb, l_lat, d)
```

```python
import functools
import math

import numpy as np
import jax
import jax.numpy as jnp
from jax import lax
from jax.experimental import pallas as pl
from jax.experimental.pallas import tpu as pltpu

F32 = jnp.float32
BF16 = jnp.bfloat16

EPS = 1e-6
GRID_W = 64
N_MOD = 6
ATTN_HD = 64
ATTN_VD = 128
ATTN_VROWS = ATTN_VD + 16
LOG2E = float(np.log2(np.e))
ATTN_HEADS = 4
ATTN_W = 512
ROPE_BASE = 10000.0
SSM_W = 256
SSM_GROUP = 16
SSM_GROUPS = 16
SSM_STATE = 64
SSM_CHUNK = 16
POOL_W = 256
POOL_WINDOWS = (2, 4, 8, 16)
POOL_GROUP = 64
POOL_HALO = 8
SSM_OFF = 3 * ATTN_W
POOL_OFF = SSM_OFF + SSM_W
IN_W = POOL_OFF + POOL_W
PEER_HEADS = 8
PEER_NKEYS = 128
PEER_KDIM = 128
PEER_TOPK = 16
NEG_BIG = -3.0e38
SQRT_HALF = float(np.sqrt(0.5).astype(np.float32))

LANE = 128
ROW_TILE = 256
VMEM_LIMIT = 56 * 1024 * 1024


def _cparams(sem):
    return pltpu.CompilerParams(dimension_semantics=sem, vmem_limit_bytes=VMEM_LIMIT)


def _gelu(x):
    return 0.5 * x * (1.0 + lax.erf(x * SQRT_HALF))


def _norm_mod(x, g, shift, scale):
    ms = jnp.mean(x * x, axis=-1, keepdims=True)
    y = x * lax.rsqrt(ms + EPS) * g
    return y * (1.0 + scale) + shift


def _mod_kernel(s_ref, w_ref, b_ref, o_ref):
    s = s_ref[...]
    s = s * jax.nn.sigmoid(s)
    o_ref[0] = jnp.dot(s.astype(BF16), w_ref[0].astype(BF16), preferred_element_type=F32) + b_ref[0]


def _mod_vectors(cs, w_mod, b_mod):
    depth, d, n = w_mod.shape
    tn = 1536
    return pl.pallas_call(
        _mod_kernel,
        grid=(depth, n // tn),
        in_specs=[pl.BlockSpec((8, d), lambda l, j: (0, 0)),
                  pl.BlockSpec((1, d, tn), lambda l, j: (l, 0, j)),
                  pl.BlockSpec((1, 1, tn), lambda l, j: (l, 0, j))],
        out_specs=pl.BlockSpec((1, 8, tn), lambda l, j: (l, 0, j)),
        out_shape=jax.ShapeDtypeStruct((depth, 8, n), F32),
        compiler_params=_cparams(("arbitrary", "arbitrary")),
        name="mod_vectors",
    )(cs, w_mod, b_mod.reshape(depth, 1, n))


def _stream_tile(xa_ref, xb_ref, n_a):
    return jnp.where(pl.program_id(0) < n_a, xa_ref[...], xb_ref[...])


def _stream_specs(xa, xb, n_a, tm):
    d = xa.shape[1]
    return [pl.BlockSpec((tm, d), lambda i: (jnp.minimum(i, n_a - 1), 0)),
            pl.BlockSpec((tm, d), lambda i: (jnp.maximum(i - n_a, 0), 0))]


def _inproj_kernel(xa_ref, xb_ref, g_ref, mod_ref, w_ref, cos_ref, sin_ref,
                   q_ref, k_ref, v_ref, us_ref, up_ref, *, n_a):
    tm = xa_ref.shape[0]
    h = _norm_mod(_stream_tile(xa_ref, xb_ref, n_a), g_ref[...], mod_ref[0, 0:1, :], mod_ref[0, 1:2, :])
    p = jnp.dot(h.astype(BF16), w_ref[...], preferred_element_type=F32)
    c = cos_ref[...]
    s = sin_ref[...]
    lane = lax.broadcasted_iota(jnp.int32, (tm, LANE), 1)
    first = (lane % 32) < 16
    for off, ref, sc in ((0, q_ref, ATTN_HD ** -0.5 * LOG2E), (ATTN_W, k_ref, 1.0)):
        for blk in range(ATTN_W // LANE):
            xb = p[:, off + LANE * blk: off + LANE * (blk + 1)]
            partner = jnp.where(first, pltpu.roll(xb, LANE - 16, 1), pltpu.roll(xb, 16, 1))
            ref[:, LANE * blk: LANE * (blk + 1)] = ((xb * c + partner * s) * sc).astype(BF16)
    vt = p[:, 2 * ATTN_W:SSM_OFF].T.astype(BF16)
    ones_rows = (lax.broadcasted_iota(jnp.int32, (ATTN_VROWS - ATTN_VD, tm), 0) == 0).astype(BF16)
    for hd in range(ATTN_HEADS):
        v_ref[0, hd * ATTN_VROWS:hd * ATTN_VROWS + ATTN_VD, :] = vt[hd * ATTN_VD:(hd + 1) * ATTN_VD]
        v_ref[0, hd * ATTN_VROWS + ATTN_VD:(hd + 1) * ATTN_VROWS, :] = ones_rows
    us_ref[...] = p[:, SSM_OFF:POOL_OFF]
    up_ref[...] = p[:, POOL_OFF:IN_W]


def _seg_map(tiles_per_batch, nb):
    def seg(i):
        return jnp.minimum(i // tiles_per_batch, nb)
    return seg


def _inproj(xa, xb, n_a, t, g, mods, w, cos_t, sin_t, tiles_per_batch, nb):
    d = xa.shape[1]
    tm = ROW_TILE
    seg = _seg_map(tiles_per_batch, nb)
    row = lambda i: (i, 0)
    rope_row = lambda i: (jnp.where(i < nb * tiles_per_batch, i % tiles_per_batch, tiles_per_batch), 0)
    return pl.pallas_call(
        functools.partial(_inproj_kernel, n_a=n_a),
        grid=(t // tm,),
        in_specs=_stream_specs(xa, xb, n_a, tm) + [
                  pl.BlockSpec((1, d), lambda i: (0, 0)),
                  pl.BlockSpec((1, N_MOD, d), lambda i: (seg(i), 0, 0)),
                  pl.BlockSpec((d, IN_W), lambda i: (0, 0)),
                  pl.BlockSpec((tm, LANE), rope_row),
                  pl.BlockSpec((tm, LANE), rope_row)],
        out_specs=[pl.BlockSpec((tm, ATTN_W), row),
                   pl.BlockSpec((tm, ATTN_W), row),
                   pl.BlockSpec((1, ATTN_HEADS * ATTN_VROWS, tm), lambda i: (i, 0, 0)),
                   pl.BlockSpec((tm, SSM_W), row),
                   pl.BlockSpec((tm, POOL_W), row)],
        out_shape=[jax.ShapeDtypeStruct((t, ATTN_W), BF16),
                   jax.ShapeDtypeStruct((t, ATTN_W), BF16),
                   jax.ShapeDtypeStruct((t // tm, ATTN_HEADS * ATTN_VROWS, tm), BF16),
                   jax.ShapeDtypeStruct((t, SSM_W), F32),
                   jax.ShapeDtypeStruct((t, POOL_W), F32)],
        compiler_params=_cparams(("arbitrary",)),
        name="inproj",
    )(xa, xb, g, mods, w, cos_t, sin_t)


def _attn_stages(q, tq):
    lane = lax.broadcasted_iota(jnp.int32, (tq, LANE), 1)
    zero = jnp.zeros_like(q)
    qs = jnp.concatenate([jnp.where(lane < ATTN_HD, q, zero),
                          jnp.where(lane >= ATTN_HD, q, zero)], axis=0)

    def scores(kb):
        return lax.dot_general(kb, qs, (((1,), (1,)), ((), ())), preferred_element_type=F32)

    def softmax(m, s):
        m_new = jnp.maximum(m, jnp.max(s, axis=0, keepdims=True))
        return m_new, jnp.exp2(m - m_new), jnp.exp2(s - m_new).astype(BF16)

    def weighted(acc, alpha, pb, vts):
        acc = alpha * acc
        rows = pb.shape[0] // len(vts)
        for c, vt in enumerate(vts):
            acc = acc + jnp.dot(vt, pb[c * rows:(c + 1) * rows], preferred_element_type=F32)
        return acc

    return scores, softmax, weighted


def _attn_finish(lam, acc, g, tq, out_scale):
    o = acc[:ATTN_VD] / acc[ATTN_VD:ATTN_VD + 1]
    o = (o[:, :tq] - lam * o[:, tq:]).T
    ms = jnp.mean(o * o, axis=-1, keepdims=True)
    return (o * lax.rsqrt(ms + EPS) * g * out_scale).astype(BF16)


def _attn_ctx_kernel(lam_ref, q_ref, kc_ref, vc_ref, g_ref, o_ref, *, out_scale):
    tq = q_ref.shape[0]
    scores, softmax, weighted = _attn_stages(q_ref[...], tq)
    m = jnp.full((1, 2 * tq), NEG_BIG, F32)
    acc = jnp.zeros((ATTN_VROWS, 2 * tq), F32)
    m, alpha, pb = softmax(m, scores(kc_ref[...]))
    acc = weighted(acc, alpha, pb, [vc_ref[0]])
    o_ref[...] = _attn_finish(lam_ref[0], acc, g_ref[...], tq, out_scale)


def _attn_kernel(lam_ref, q_ref, kl_ref, kc_ref, vl_ref, vc_ref, g_ref, o_ref,
                 s_a, s_b, p_a, p_b, acc_ref, *, n_lat_k, tk, out_scale):
    tq = q_ref.shape[0]
    scores, softmax, weighted = _attn_stages(q_ref[...], tq)
    n_sub = tk // vc_ref.shape[-1]

    def stage_scores(t, s_buf):
        start = pl.multiple_of(t * tk, tk)
        s_buf[...] = scores(kl_ref[pl.ds(start, tk), :])

    def stage_softmax(m, s_buf, p_buf):
        m, alpha, pb = softmax(m, s_buf[...])
        p_buf[...] = pb
        return m, alpha

    def stage_values(alpha, p_buf, t):
        acc_ref[...] = weighted(acc_ref[...], alpha, p_buf[...],
                                [vl_ref[t * n_sub + c] for c in range(n_sub)])

    m = jnp.full((1, 2 * tq), NEG_BIG, F32)
    m, alpha, pb = softmax(m, scores(kc_ref[...]))
    acc_ref[...] = weighted(jnp.zeros((ATTN_VROWS, 2 * tq), F32), alpha, pb, [vc_ref[0]])

    if n_lat_k % 2 == 0:
        stage_scores(0, s_a)
        stage_scores(1, s_b)
        m, alpha = stage_softmax(m, s_a, p_a)

        def pair(i, carry):
            m, alpha = carry
            k = 2 * i
            stage_scores(k + 2, s_a)
            m, alpha_n = stage_softmax(m, s_b, p_b)
            stage_values(alpha, p_a, k)
            stage_scores(k + 3, s_b)
            m, alpha_nn = stage_softmax(m, s_a, p_a)
            stage_values(alpha_n, p_b, k + 1)
            return m, alpha_nn

        m, alpha = lax.fori_loop(0, n_lat_k // 2 - 1, pair, (m, alpha))
        m, alpha_n = stage_softmax(m, s_b, p_b)
        stage_values(alpha, p_a, n_lat_k - 2)
        stage_values(alpha_n, p_b, n_lat_k - 1)
    else:
        for t in range(n_lat_k):
            stage_scores(t, s_a)
            m, alpha = stage_softmax(m, s_a, p_a)
            stage_values(alpha, p_a, t)
    o_ref[...] = _attn_finish(lam_ref[0], acc_ref[...], g_ref[...], tq, out_scale)


def _attention(q, k, vt, lam, g, nb, l_lat, l_ctx, with_ctx_queries, out_scale):
    tq_ctx = ROW_TILE
    tq = 1024 if l_lat % 1024 == 0 else ROW_TILE
    tk = 1024 if l_lat % 2048 == 0 else (512 if l_lat % 512 == 0 else l_lat)
    vt_tile = vt.shape[-1]
    assert l_ctx == tq_ctx and l_lat % tq == 0 and vt_tile == l_ctx and tk % vt_tile == 0
    n_lat_q = l_lat // tq
    n_vt_lat = l_lat // vt_tile
    ctx_blk0 = nb * l_lat // l_ctx
    smem = pl.BlockSpec(memory_space=pltpu.SMEM)
    gain = pl.BlockSpec((1, LANE), lambda b, h, i: (0, 0))
    kc_spec = pl.BlockSpec((l_ctx, LANE), lambda b, h, i: (ctx_blk0 + b, h))
    vc_spec = pl.BlockSpec((1, ATTN_VROWS, vt_tile), lambda b, h, i: (nb * n_vt_lat + b, h, 0))
    sem = _cparams(("arbitrary", "arbitrary", "arbitrary"))
    lat = pl.pallas_call(
        functools.partial(_attn_kernel, n_lat_k=l_lat // tk, tk=tk, out_scale=out_scale),
        grid=(nb, ATTN_HEADS, n_lat_q),
        in_specs=[smem,
                  pl.BlockSpec((tq, LANE), lambda b, h, i: (b * n_lat_q + i, h)),
                  pl.BlockSpec((l_lat, LANE), lambda b, h, i: (b, h)),
                  kc_spec,
                  pl.BlockSpec((n_vt_lat, ATTN_VROWS, vt_tile), lambda b, h, i: (b, h, 0)),
                  vc_spec, gain],
        out_specs=pl.BlockSpec((tq, LANE), lambda b, h, i: (b * n_lat_q + i, h)),
        out_shape=jax.ShapeDtypeStruct((nb * l_lat, ATTN_W), BF16),
        scratch_shapes=[pltpu.VMEM((tk, 2 * tq), F32), pltpu.VMEM((tk, 2 * tq), F32),
                        pltpu.VMEM((tk, 2 * tq), BF16), pltpu.VMEM((tk, 2 * tq), BF16),
                        pltpu.VMEM((ATTN_VROWS, 2 * tq), F32)],
        compiler_params=sem,
        name="diff_attention",
    )(lam, q, k, k, vt, vt, g)
    if not with_ctx_queries:
        return lat, None
    ctx = pl.pallas_call(
        functools.partial(_attn_ctx_kernel, out_scale=out_scale),
        grid=(nb, ATTN_HEADS, 1),
        in_specs=[smem,
                  pl.BlockSpec((tq_ctx, LANE), lambda b, h, i: (ctx_blk0 + b, h)),
                  kc_spec, vc_spec, gain],
        out_specs=pl.BlockSpec((tq_ctx, LANE), lambda b, h, i: (b, h)),
        out_shape=jax.ShapeDtypeStruct((nb * l_ctx, ATTN_W), BF16),
        compiler_params=sem,
        name="diff_attention_ctx",
    )(lam, q, k, vt, g)
    return lat, ctx


def _ssm_kernel(ul_ref, uc_ref, m_ref, bm_ref, cm_ref, a_ref, d_ref, yl_ref, yc_ref,
                s_scr, h_scr, *, ctx_out):
    tc, pw = SSM_CHUNK, 2 * SSM_GROUP
    nl = ul_ref.shape[0] // tc
    nc = uc_ref.shape[0] // tc
    w = m_ref.shape[-1]
    half = s_scr.shape[-1] // 2
    sub = pl.program_id(1) % (LANE // pw)
    nt_dims = (((1,), (1,)), ((), ()))
    ch = lax.broadcasted_iota(jnp.int32, (LANE, 1), 0) - sub * pw
    valid = (ch >= 0) & (ch < pw)
    tgt0 = (ch >> 4) * (tc * SSM_GROUP) + (ch & (SSM_GROUP - 1))
    col = lax.broadcasted_iota(jnp.int32, (1, w), 1)

    def perm(t):
        return jnp.where(valid & (col == tgt0 + t * SSM_GROUP), 1.0, 0.0).astype(BF16)

    def fold(ref, n):
        acc = jnp.zeros((n, w), F32)
        for t in range(tc):
            acc = acc + jnp.dot(ref[pl.ds(t, n, stride=tc), :].astype(BF16), perm(t),
                                preferred_element_type=F32)
        return acc.astype(BF16)

    def unfold_add(y, ref, n):
        hi = y.astype(BF16)
        r1 = y - hi.astype(F32)
        mid = r1.astype(BF16)
        lo = (r1 - mid.astype(F32)).astype(BF16)
        for t in range(tc):
            p = perm(t)
            o = sum(lax.dot_general(part, p, nt_dims, preferred_element_type=F32)
                    for part in (hi, mid, lo))
            ref[pl.ds(t, n, stride=tc), :] += o

    @pl.when(sub == 0)
    def _():
        yl_ref[...] = ul_ref[...] * d_ref[...]
        yc_ref[...] = uc_ref[...] * d_ref[...] if ctx_out else jnp.zeros_like(yc_ref)

    ulb = fold(ul_ref, nl)
    ucb = fold(uc_ref, nc)
    yl = jnp.dot(ulb, m_ref[0], preferred_element_type=F32)
    if ctx_out:
        yc = jnp.dot(ucb, m_ref[0], preferred_element_type=F32)
    for dr in range(2):
        s_scr[dr, 0:nc, :] = jnp.dot(ucb, bm_ref[dr, 0], preferred_element_type=F32)
        s_scr[dr, nc:nc + nl, :] = jnp.dot(ulb, bm_ref[dr, 0], preferred_element_type=F32)
    a = a_ref[0]
    afr, afi, arr, ari = a[0:1], a[1:2], a[2:3], a[3:4]

    def make_body(base, n):
        def body(t, carry):
            fr, fi, rr, ri = carry
            rf = base + t
            rv = base + n - 1 - t
            h_scr[0, pl.ds(rf, 1), :] = jnp.concatenate([fr, fi], axis=-1)
            h_scr[1, pl.ds(rv, 1), :] = jnp.concatenate([rr, ri], axis=-1)
            sf = s_scr[0, pl.ds(rf, 1), :]
            sv = s_scr[1, pl.ds(rv, 1), :]
            nfr = afr * fr - afi * fi + sf[:, :half]
            nfi = afr * fi + afi * fr + sf[:, half:]
            nrr = arr * rr - ari * ri + sv[:, :half]
            nri = arr * ri + ari * rr + sv[:, half:]
            return nfr, nfi, nrr, nri
        return body

    z = jnp.zeros((1, half), F32)
    carry = lax.fori_loop(0, nc, make_body(0, nc), (z, z, z, z))
    lax.fori_loop(0, nl, make_body(nc, nl), carry)
    for dr in range(2):
        yl = yl + jnp.dot(h_scr[dr, nc:nc + nl, :].astype(BF16), cm_ref[dr, 0],
                          preferred_element_type=F32)
        if ctx_out:
            yc = yc + jnp.dot(h_scr[dr, 0:nc, :].astype(BF16), cm_ref[dr, 0],
                              preferred_element_type=F32)
    unfold_add(yl, yl_ref, nl)
    if ctx_out:
        unfold_add(yc, yc_ref, nc)


def _ssm_matrices(lam_re, lam_im, log_step, b_re, b_im, c_re, c_im):
    tc = SSM_CHUNK
    g, p, hh = SSM_GROUPS, SSM_STATE, SSM_GROUP
    npair = g // 2
    step = jnp.exp(log_step)[..., None]
    den = lam_re * lam_re + lam_im * lam_im

    def power(k):
        er = jnp.exp(lam_re * step * k)
        return er * jnp.cos(lam_im * step * k), er * jnp.sin(lam_im * step * k)

    ar, ai = power(1.0)
    nr = ar - 1.0
    cr_ = (nr * lam_re + ai * lam_im) / den
    ci_ = (ai * lam_re - nr * lam_im) / den
    bbr = cr_[..., None] * b_re - ci_[..., None] * b_im
    bbi = cr_[..., None] * b_im + ci_[..., None] * b_re
    ks = jnp.arange(tc + 1, dtype=F32)
    pw = jax.vmap(power)(ks)
    pwr, pwi = pw
    ct_re = jnp.swapaxes(c_re, -1, -2)
    ct_im = jnp.swapaxes(c_im, -1, -2)
    cb_re = (ct_re[..., :, None] * bbr[..., None, :] - ct_im[..., :, None] * bbi[..., None, :])
    cb_im = (ct_im[..., :, None] * bbr[..., None, :] + ct_re[..., :, None] * bbi[..., None, :])
    hp = lax.Precision.HIGHEST
    kk = (jnp.einsum('kdgp,dgpq->kdgq', pwr[:tc], cb_re.reshape(2, g, p, hh * hh), precision=hp)
          - jnp.einsum('kdgp,dgpq->kdgq', pwi[:tc], cb_im.reshape(2, g, p, hh * hh), precision=hp))
    kk = kk.reshape(tc, 2, g, hh, hh)
    zpad = jnp.zeros((tc - 1,) + kk.shape[2:], F32)
    lagtab = (jnp.concatenate([zpad, kk[:, 0]], 0)
              + jnp.concatenate([kk[::-1, 1], zpad], 0))
    lagtab = jnp.transpose(lagtab, (1, 3, 0, 2)).reshape(g, hh, (2 * tc - 1) * hh)
    mm = jnp.stack([lagtab[:, :, (tc - 1 - s) * hh:(2 * tc - 1 - s) * hh] for s in range(tc)], axis=1)
    mm = mm.reshape(g, tc * hh, tc * hh)
    zero = jnp.zeros_like(mm[0::2])
    m2 = jnp.concatenate([jnp.concatenate([mm[0::2], zero], -1),
                          jnp.concatenate([zero, mm[1::2]], -1)], -2)
    def bmat(dr, exps):
        pr = pwr[exps, dr]
        pi = pwi[exps, dr]
        re = pr[..., None] * bbr[dr][None] - pi[..., None] * bbi[dr][None]
        im = pr[..., None] * bbi[dr][None] + pi[..., None] * bbr[dr][None]
        re = jnp.transpose(re, (1, 0, 3, 2)).reshape(g, tc * hh, p)
        im = jnp.transpose(im, (1, 0, 3, 2)).reshape(g, tc * hh, p)
        z = jnp.zeros_like(re[0::2])
        top = jnp.concatenate([re[0::2], z, im[0::2], z], -1)
        bot = jnp.concatenate([z, re[1::2], z, im[1::2]], -1)
        return jnp.concatenate([top, bot], -2)
    bm = jnp.stack([bmat(0, jnp.arange(tc - 1, -1, -1)), bmat(1, jnp.arange(tc))])
    def cmat(dr, exps):
        pr = jnp.transpose(pwr[exps, dr], (1, 2, 0))[..., None]
        pi = jnp.transpose(pwi[exps, dr], (1, 2, 0))[..., None]
        cr, ci = ct_re[dr][:, :, None, :], ct_im[dr][:, :, None, :]
        re = (cr * pr - ci * pi).reshape(g, p, tc * hh)
        im = (cr * pi + ci * pr).reshape(g, p, tc * hh)
        z = jnp.zeros_like(re[0::2])
        return jnp.concatenate([jnp.concatenate([re[0::2], z], -1),
                                jnp.concatenate([z, re[1::2]], -1),
                                jnp.concatenate([-im[0::2], z], -1),
                                jnp.concatenate([z, -im[1::2]], -1)], -2)
    cm = jnp.stack([cmat(0, jnp.arange(1, tc + 1)), cmat(1, jnp.arange(tc, 0, -1))])
    a16r = pwr[tc].reshape(2, npair, 2 * p)
    a16i = pwi[tc].reshape(2, npair, 2 * p)
    a16 = jnp.stack([a16r[0], a16i[0], a16r[1], a16i[1]], axis=1)
    return m2.astype(BF16), bm.astype(BF16), cm.astype(BF16), a16


def _ssm(us, mats, d, nb, l_lat, l_ctx, ctx_out):
    m2, bm, cm, a16 = mats
    tc, hh = SSM_CHUNK, SSM_GROUP
    npair = SSM_GROUPS // 2
    w = 2 * tc * hh
    per_col = LANE // (2 * hh)
    nl, nc = l_lat // tc, l_ctx // tc
    ctx_blk0 = nb * l_lat // l_ctx
    yl, yc = pl.pallas_call(
        functools.partial(_ssm_kernel, ctx_out=ctx_out),
        grid=(nb, npair),
        in_specs=[pl.BlockSpec((l_lat, LANE), lambda b, q: (b, q // per_col)),
                  pl.BlockSpec((l_ctx, LANE), lambda b, q: (ctx_blk0 + b, q // per_col)),
                  pl.BlockSpec((1, w, w), lambda b, q: (q, 0, 0)),
                  pl.BlockSpec((2, 1, w, 4 * SSM_STATE), lambda b, q: (0, q, 0, 0)),
                  pl.BlockSpec((2, 1, 4 * SSM_STATE, w), lambda b, q: (0, q, 0, 0)),
                  pl.BlockSpec((1, 4, 2 * SSM_STATE), lambda b, q: (q, 0, 0)),
                  pl.BlockSpec((1, LANE), lambda b, q: (0, q // per_col))],
        out_specs=[pl.BlockSpec((l_lat, LANE), lambda b, q: (b, q // per_col)),
                   pl.BlockSpec((l_ctx, LANE), lambda b, q: (b, q // per_col))],
        out_shape=[jax.ShapeDtypeStruct((nb * l_lat, SSM_W), F32),
                   jax.ShapeDtypeStruct((nb * l_ctx, SSM_W), F32)],
        scratch_shapes=[pltpu.VMEM((2, nc + nl, 4 * SSM_STATE), F32),
                        pltpu.VMEM((2, nc + nl, 4 * SSM_STATE), F32)],
        compiler_params=_cparams(("arbitrary", "arbitrary")),
        name="s5_scan",
    )(us, us, m2, bm, cm, a16, d)
    return yl, (yc if ctx_out else None)


def _pool_kernel(prev_ref, cur_ref, next_ref, w_ref, scale_ref, o_ref, ext,
                 *, n_lat_tiles, tiles_per_lat, l_lat, l_ctx):
    r = cur_ref.shape[0]
    hl = POOL_HALO
    i = pl.program_id(0)
    is_lat = i < n_lat_tiles
    pos = jnp.where(is_lat, i % tiles_per_lat, 0)
    n_tiles = jnp.where(is_lat, tiles_per_lat, l_ctx // r)
    seq_len = jnp.where(is_lat, l_lat, l_ctx)
    zeros = jnp.zeros((hl, POOL_W), F32)
    ext[0:hl, :] = jnp.where(pos > 0, prev_ref[...], zeros)
    ext[hl:hl + r, :] = cur_ref[...]
    ext[hl + r:hl + r + hl, :] = jnp.where(pos < n_tiles - 1, next_ref[...], zeros)
    u = cur_ref[...]
    t = pos * r + lax.broadcasted_iota(jnp.int32, (r, 1), 0)
    lane = lax.broadcasted_iota(jnp.int32, (r, POOL_W), 1)

    def win(k):
        return ext[hl + k:hl + k + r, :]

    acc = win(-1) + u
    mean = jnp.zeros((r, POOL_W), F32)
    lo_k, hi_k = -1, 0
    for gi, wn in enumerate(POOL_WINDOWS):
        hw = wn // 2
        while lo_k > -hw:
            lo_k -= 1
            acc = acc + win(lo_k)
        while hi_k < hw - 1:
            hi_k += 1
            acc = acc + win(hi_k)
        cnt = (jnp.minimum(t + hw, seq_len) - jnp.maximum(t - hw, 0)).astype(F32)
        sel = (lane >= gi * POOL_GROUP) & (lane < (gi + 1) * POOL_GROUP)
        mean = jnp.where(sel, acc / cnt, mean)
    dlt = (mean - u).astype(BF16)
    o_ref[...] = jnp.dot(dlt, w_ref[...], preferred_element_type=F32) * scale_ref[...]


def _pool(up, w_blk, scale, nb, l_lat, l_ctx, n_rows):
    r = ROW_TILE
    hl = POOL_HALO
    n_tiles = n_rows // r
    last8 = up.shape[0] // hl - 1
    kern = functools.partial(_pool_kernel, n_lat_tiles=nb * l_lat // r, tiles_per_lat=l_lat // r,
                             l_lat=l_lat, l_ctx=l_ctx)
    return pl.pallas_call(
        kern,
        grid=(n_tiles,),
        in_specs=[pl.BlockSpec((hl, POOL_W), lambda i: (jnp.maximum(i * (r // hl) - 1, 0), 0)),
                  pl.BlockSpec((r, POOL_W), lambda i: (i, 0)),
                  pl.BlockSpec((hl, POOL_W), lambda i: (jnp.minimum((i + 1) * (r // hl), last8), 0)),
                  pl.BlockSpec((POOL_W, POOL_W), lambda i: (0, 0)),
                  pl.BlockSpec((1, POOL_W), lambda i: (0, 0))],
        out_specs=pl.BlockSpec((r, POOL_W), lambda i: (i, 0)),
        out_shape=jax.ShapeDtypeStruct((n_rows, POOL_W), F32),
        scratch_shapes=[pltpu.VMEM((r + 2 * hl, POOL_W), F32)],
        compiler_params=_cparams(("arbitrary",)),
        name="pool_mix",
    )(up, up, up, w_blk, scale)


def _mixout_kernel(xa_ref, xb_ref, aa_ref, ab_ref, ya_ref, yb_ref, pool_ref, mod_ref, wglu_ref,
                   wout_ref, o_ref, *, n_a, n_lat):
    yg = _gelu(_stream_tile(ya_ref, yb_ref, n_lat))
    z = yg * jax.nn.sigmoid(jnp.dot(yg.astype(BF16), wglu_ref[...], preferred_element_type=F32))
    r = jnp.dot(_stream_tile(aa_ref, ab_ref, n_lat), wout_ref[0:ATTN_W, :],
                preferred_element_type=F32)
    r = r + jnp.dot(z.astype(BF16), wout_ref[ATTN_W:ATTN_W + SSM_W, :], preferred_element_type=F32)
    r = r + jnp.dot(pool_ref[...].astype(BF16), wout_ref[ATTN_W + SSM_W:, :],
                    preferred_element_type=F32)
    o_ref[...] = _stream_tile(xa_ref, xb_ref, n_a) + mod_ref[0, 2:3, :] * r


def _mixout(xa, xb, n_a, attn, y, pool, mods, wglu, wout, tiles_per_batch, nb, n_rows):
    d = xa.shape[1]
    tm = ROW_TILE
    seg = _seg_map(tiles_per_batch, nb)
    row = lambda i: (i, 0)
    n_lat = nb * tiles_per_batch
    attn = (attn[0], attn[0] if attn[1] is None else attn[1])
    y = (y[0], y[0] if y[1] is None else y[1])
    return pl.pallas_call(
        functools.partial(_mixout_kernel, n_a=n_a, n_lat=n_lat),
        grid=(n_rows // tm,),
        in_specs=_stream_specs(xa, xb, n_a, tm) + _stream_specs(*attn, n_lat, tm)
                 + _stream_specs(*y, n_lat, tm) + [
                  pl.BlockSpec((tm, POOL_W), row),
                  pl.BlockSpec((1, N_MOD, d), lambda i: (seg(i), 0, 0)),
                  pl.BlockSpec((SSM_W, SSM_W), lambda i: (0, 0)),
                  pl.BlockSpec((d, d), lambda i: (0, 0))],
        out_specs=pl.BlockSpec((tm, d), row),
        out_shape=jax.ShapeDtypeStruct((n_rows, d), F32),
        compiler_params=_cparams(("arbitrary",)),
        name="mix_out",
    )(xa, xb, *attn, *y, pool, mods, wglu, wout)


def _oddeven_merge(lo, hi, r):
    step = r * 2
    if step < hi - lo:
        yield from _oddeven_merge(lo, hi, step)
        yield from _oddeven_merge(lo + r, hi, step)
        yield from [(i, i + r) for i in range(lo + r, hi - r, step)]
    else:
        yield (lo, lo + r)


def _oddeven_sort(lo, hi):
    if hi - lo >= 1:
        mid = lo + (hi - lo) // 2
        yield from _oddeven_sort(lo, mid)
        yield from _oddeven_sort(mid + 1, hi)
        yield from _oddeven_merge(lo, hi, 1)


_SORT16 = tuple(_oddeven_sort(0, PEER_TOPK - 1))
_BITONIC16 = tuple((i, i + d) for d in (8, 4, 2, 1) for i in range(PEER_TOPK) if not i & d)


def _compare_exchange(rows, net):
    rows = list(rows)
    for i, j in net:
        hi, lo = jnp.maximum(rows[i], rows[j]), jnp.minimum(rows[i], rows[j])
        rows[i], rows[j] = hi, lo
    return rows


def _top_sorted(s, k):
    assert k == PEER_TOPK and s.shape[0] == 8 * PEER_TOPK
    rows = _compare_exchange([s[8 * v:8 * v + 8] for v in range(PEER_TOPK)], _SORT16)
    for shift in (4, 2, 1):
        other = [pltpu.roll(r, shift, 0) for r in rows]
        rows = _compare_exchange([jnp.maximum(rows[i], other[PEER_TOPK - 1 - i])
                                  for i in range(PEER_TOPK)], _BITONIC16)
    return [r[0:1] for r in rows]


def _peer_score_kernel(x_ref, g_ref, mod_ref, wq_ref, keys_ref,
                       h_ref, beta_ref, s2_ref, e1_ref, e2_ref, q_scr):
    tm = x_ref.shape[0]
    h = _norm_mod(x_ref[...], g_ref[...], mod_ref[0, 3:4, :], mod_ref[0, 4:5, :]).astype(BF16)
    h_ref[...] = h
    q = jnp.dot(h, wq_ref[...], preferred_element_type=F32)
    for hc in range(2 * PEER_HEADS):
        q_scr[hc] = q[:, hc * PEER_KDIM:(hc + 1) * PEER_KDIM].astype(BF16)
    row8 = lax.broadcasted_iota(jnp.int32, (8, tm), 0)

    def head(hd, _):
        nt = (((1,), (1,)), ((), ()))
        s1 = lax.dot_general(keys_ref[hd, 0], q_scr[2 * hd], nt,
                             preferred_element_type=F32)
        s2 = lax.dot_general(keys_ref[hd, 1], q_scr[2 * hd + 1], nt,
                             preferred_element_type=F32)
        a = _top_sorted(s1, PEER_TOPK)
        b = _top_sorted(s2, PEER_TOPK)
        acat = jnp.concatenate(a, axis=0)
        bcat = jnp.concatenate(b, axis=0)
        pieces = [a[0] + bcat]
        for i in range(2, 9):
            piece = a[i - 1] + bcat[0:8]
            n_valid = PEER_TOPK // i
            pieces.append(piece if n_valid >= 8 else jnp.where(row8 < n_valid, piece, NEG_BIG))
        pieces.append(acat[8:16] + b[0])
        cand = jnp.concatenate(pieces, axis=0)
        work = cand
        cum = jnp.zeros((1, tm), F32)
        tau = jnp.full((1, tm), NEG_BIG, F32)
        for _ in range(PEER_TOPK):
            m = jnp.max(work, axis=0, keepdims=True)
            eq = work == m
            new = cum + jnp.sum(eq.astype(F32), axis=0, keepdims=True)
            tau = jnp.where((cum < PEER_TOPK) & (new >= PEER_TOPK), m, tau)
            work = jnp.where(eq, NEG_BIG, work)
            cum = new
        top = a[0] + b[0]
        z = jnp.sum(jnp.where(cand >= tau, jnp.exp(cand - top), 0.0), axis=0, keepdims=True)
        big = -NEG_BIG
        b_rows = [bcat] + [bcat[0:8]] * 7
        beta_r = [jnp.min(jnp.where(pc >= tau, br, big), axis=0, keepdims=True)
                  for pc, br in zip(pieces[:8], b_rows)]
        tail = jnp.where(pieces[8] >= tau, b[0], big)
        beta_r += [tail[r:r + 1] for r in range(8)]
        beta = jnp.full_like(s1, big)
        for r in range(PEER_TOPK):
            beta = jnp.where(s1 == a[r], beta_r[r], beta)
        beta_ref[hd] = beta
        s2_ref[hd] = s2
        e1_ref[hd] = jnp.exp(s1 - a[0])
        e2_ref[hd] = jnp.exp(s2 - b[0]) * (1.0 / z)
        return 0

    lax.fori_loop(0, PEER_HEADS, head, 0, unroll=4)


def _peer_scores(x, g, mods, wq, keys, tiles_per_batch, nb, n_rows):
    d = x.shape[1]
    tm = ROW_TILE
    seg = _seg_map(tiles_per_batch, nb)
    nq = wq.shape[1]
    col = lambda i: (0, 0, i)
    big = jax.ShapeDtypeStruct((PEER_HEADS, PEER_NKEYS, n_rows), F32)
    return pl.pallas_call(
        _peer_score_kernel,
        grid=(n_rows // tm,),
        in_specs=[pl.BlockSpec((tm, d), lambda i: (i, 0)),
                  pl.BlockSpec((1, d), lambda i: (0, 0)),
                  pl.BlockSpec((1, N_MOD, d), lambda i: (seg(i), 0, 0)),
                  pl.BlockSpec((d, nq), lambda i: (0, 0)),
                  pl.BlockSpec((PEER_HEADS, 2, PEER_NKEYS, PEER_KDIM), lambda i: (0, 0, 0, 0))],
        out_specs=[pl.BlockSpec((tm, d), lambda i: (i, 0)),
                   pl.BlockSpec((PEER_HEADS, PEER_NKEYS, tm), col),
                   pl.BlockSpec((PEER_HEADS, PEER_NKEYS, tm), col),
                   pl.BlockSpec((PEER_HEADS, PEER_NKEYS, tm), col),
                   pl.BlockSpec((PEER_HEADS, PEER_NKEYS, tm), col)],
        out_shape=[jax.ShapeDtypeStruct((n_rows, d), BF16), big, big, big, big],
        scratch_shapes=[pltpu.VMEM((2 * PEER_HEADS, tm, PEER_KDIM), BF16)],
        compiler_params=_cparams(("arbitrary",)),
        name="peer_scores",
    )(x, g, mods, wq, keys)


def _transpose_cast_kernel(x_ref, o_ref):
    o_ref[...] = x_ref[...].T.astype(o_ref.dtype)


def _transpose_cast(x, dtype, rows_per_step=512):
    n, d = x.shape
    return pl.pallas_call(
        _transpose_cast_kernel,
        grid=(n // rows_per_step,),
        in_specs=[pl.BlockSpec((rows_per_step, d), lambda i: (i, 0))],
        out_specs=pl.BlockSpec((d, rows_per_step), lambda i: (0, i)),
        out_shape=jax.ShapeDtypeStruct((d, n), dtype),
        compiler_params=_cparams(("arbitrary",)),
        name="transpose_cast",
    )(x)


PEER_CHUNK_KEYS = 8
PEER_TOK_TILE = 512
PEER_SUB = 256


def _peer_dense_kernel(h_ref, x_ref, mod_ref, u_ref, vt_ref, beta_ref, s2_ref, e1_ref, e2_ref,
                       fg_ref, o_ref, acc_ref, a0, a1, w0, w1, row_scr,
                       *, n_chunks, final_norm):
    n_i = PEER_CHUNK_KEYS
    tm = h_ref.shape[0]
    g = pl.program_id(0)
    nt_dims = (((1,), (1,)), ((), ()))
    n_k = u_ref.shape[1] // PEER_SUB
    n_lb = tm // LANE
    assert n_k == n_lb == vt_ref.shape[1] // PEER_SUB and n_i * PEER_NKEYS == u_ref.shape[0]
    n_mt = u_ref.shape[0] // PEER_SUB

    @pl.when(g == 0)
    def _():
        acc_ref[...] = jnp.zeros_like(acc_ref)
        a1[...] = jnp.zeros_like(a1)
        w0[...] = jnp.zeros_like(w0)

    for hd in range(PEER_HEADS):
        betab = beta_ref[hd]
        e1b = e1_ref[hd]
        for ii in range(n_i):
            row_scr[hd, ii, 0, :, 0:tm] = jnp.broadcast_to(betab[ii:ii + 1], (8, tm))
            row_scr[hd, ii, 1, :, 0:tm] = jnp.broadcast_to(e1b[ii:ii + 1], (8, tm))

    def run(a_wr, a_rd, w_wr, w_rd):
        n_trips = tm // PEER_SUB
        slabs_per_trip = PEER_NKEYS // 16 // n_trips
        for lb in range(n_lb):
            th, lo = divmod(lb * LANE, PEER_SUB)
            ls = slice(lb * LANE, (lb + 1) * LANE)
            lh = slice(lo, lo + LANE)
            ms = slice(lb * PEER_SUB, (lb + 1) * PEER_SUB)

            def trip(tr, _):
                t0 = pl.multiple_of(tr * PEER_SUB, PEER_SUB)
                part_a = lax.dot_general(u_ref[ms, :], h_ref[pl.ds(t0, PEER_SUB), :], nt_dims,
                                         preferred_element_type=F32)
                part_v = jnp.dot(vt_ref[ms, :], w_rd[tr], preferred_element_type=F32)
                for sl in range(slabs_per_trip):
                    j0 = pl.multiple_of((tr * slabs_per_trip + sl) * 16, 16)
                    gs = [[jnp.zeros((8, LANE), F32), jnp.zeros((8, LANE), F32)] for _ in range(n_i)]
                    for hd in range(PEER_HEADS):
                        s2v = s2_ref[hd, pl.ds(j0, 16), ls]
                        e2v = e2_ref[hd, pl.ds(j0, 16), ls]
                        for ii in range(n_i):
                            bs = row_scr[hd, ii, 0, :, ls]
                            be = row_scr[hd, ii, 1, :, ls]
                            for hf in range(2):
                                hs = slice(hf * 8, (hf + 1) * 8)
                                gs[ii][hf] = gs[ii][hf] + jnp.where(s2v[hs] >= bs, e2v[hs], 0.0) * be
                    for ii in range(n_i):
                        r0 = pl.multiple_of(ii * PEER_NKEYS + j0, 16)
                        g16 = jnp.concatenate(gs[ii], axis=0)
                        w_wr[th, pl.ds(r0, 16), lh] = (_gelu(a_rd[th, pl.ds(r0, 16), lh]) * g16).astype(BF16)
                a_wr[tr, ms, :] = part_a
                acc_ref[tr, ms, :] += part_v
                return 0

            lax.fori_loop(0, n_trips, trip, 0, unroll=True)

    @pl.when(g % 2 == 0)
    def _():
        run(a0, a1, w1, w0)

    @pl.when(g % 2 == 1)
    def _():
        run(a1, a0, w0, w1)

    @pl.when((g >= 2) & ((g - 2) % n_chunks == n_chunks - 1))
    def _():
        out = x_ref[...] + mod_ref[0, 5:6, :] * jnp.concatenate(
            [acc_ref[t].T for t in range(acc_ref.shape[0])], axis=0)
        if final_norm:
            ms = jnp.mean(out * out, axis=-1, keepdims=True)
            out = out * lax.rsqrt(ms + EPS) * fg_ref[...]
        o_ref[...] = out
        acc_ref[...] = jnp.zeros_like(acc_ref)


def _peer_dense(h, x, mods, u, vt, beta, s2, e1, e2, fg, tiles_per_batch, nb, n_rows, final_norm):
    d = x.shape[1]
    tm = PEER_TOK_TILE
    n_i = PEER_CHUNK_KEYS
    ne = n_i * PEER_NKEYS
    tiles_per_batch = tiles_per_batch * ROW_TILE // tm
    n_chunks = u.shape[0] // ne
    n_steps = (n_rows // tm) * n_chunks
    assert n_rows % tm == 0 and d == ne
    assert PEER_NKEYS // 16 == (tm // PEER_SUB) * (ne // PEER_SUB)

    def stage(lag):
        def split(g):
            n = jnp.clip(g - lag, 0, n_steps - 1)
            return n // n_chunks, n % n_chunks
        return split

    act, gate, val = stage(0), stage(1), stage(2)
    seg = lambda i: jnp.minimum(i // tiles_per_batch, nb)
    kern = functools.partial(_peer_dense_kernel, n_chunks=n_chunks, final_norm=final_norm)
    full = (PEER_HEADS, PEER_NKEYS, tm)
    rows = (PEER_HEADS, n_i, tm)
    return pl.pallas_call(
        kern,
        grid=(n_steps + 2,),
        in_specs=[pl.BlockSpec((tm, d), lambda g: (act(g)[0], 0)),
                  pl.BlockSpec((tm, d), lambda g: (val(g)[0], 0)),
                  pl.BlockSpec((1, N_MOD, d), lambda g: (seg(val(g)[0]), 0, 0)),
                  pl.BlockSpec((ne, d), lambda g: (act(g)[1], 0)),
                  pl.BlockSpec((d, ne), lambda g: (0, val(g)[1])),
                  pl.BlockSpec(rows, lambda g: (0, gate(g)[1], gate(g)[0])),
                  pl.BlockSpec(full, lambda g: (0, 0, gate(g)[0])),
                  pl.BlockSpec(rows, lambda g: (0, gate(g)[1], gate(g)[0])),
                  pl.BlockSpec(full, lambda g: (0, 0, gate(g)[0])),
                  pl.BlockSpec((1, d), lambda g: (0, 0))],
        out_specs=pl.BlockSpec((tm, d), lambda g: (val(g)[0], 0)),
        out_shape=jax.ShapeDtypeStruct((n_rows, d), F32),
        scratch_shapes=[pltpu.VMEM((tm // PEER_SUB, d, PEER_SUB), F32),
                        pltpu.VMEM((tm // PEER_SUB, ne, PEER_SUB), F32),
                        pltpu.VMEM((tm // PEER_SUB, ne, PEER_SUB), F32),
                        pltpu.VMEM((tm // PEER_SUB, ne, PEER_SUB), BF16),
                        pltpu.VMEM((tm // PEER_SUB, ne, PEER_SUB), BF16),
                        pltpu.VMEM((PEER_HEADS, n_i, 2, 8, tm + LANE), F32)],
        compiler_params=_cparams(("arbitrary",)),
        name="peer_experts",
    )(h, x, mods, u, vt, beta, s2, e1, e2, fg)


def _rope_tables(l_lat, nb, n_ctx_rows):
    rows = l_lat // GRID_W
    r = jnp.repeat(jnp.arange(rows), GRID_W)
    col = jnp.tile(jnp.arange(GRID_W), rows)
    pos = jnp.stack([r, col], axis=-1).astype(F32)
    nf = ATTN_HD // 4
    inv = 1.0 / (ROPE_BASE ** (jnp.arange(nf, dtype=F32) / nf))
    ang = pos[:, :, None] * inv
    cos, sin = jnp.cos(ang), jnp.sin(ang)
    c64 = jnp.concatenate([cos[:, 0], cos[:, 0], cos[:, 1], cos[:, 1]], axis=-1)
    s64 = jnp.concatenate([-sin[:, 0], sin[:, 0], -sin[:, 1], sin[:, 1]], axis=-1)
    c = jnp.tile(c64, (nb, LANE // ATTN_HD))
    s = jnp.tile(s64, (nb, LANE // ATTN_HD))
    c = jnp.concatenate([c, jnp.ones((n_ctx_rows, LANE), F32)], axis=0)
    s = jnp.concatenate([s, jnp.zeros((n_ctx_rows, LANE), F32)], axis=0)
    return c, s


def kernel(x, c, ctx, c_ctx, w_mod, b_mod, norm1_g, norm2_g, w_in, w_out, lam_q1, lam_k1, lam_q2, lam_k2, subln_g, ssm_lambda_re, ssm_lambda_im, ssm_log_step, ssm_b_re, ssm_b_im, ssm_c_re, ssm_c_im, ssm_d, ssm_w_glu, pool_w, pool_scale, peer_wq, peer_keys, peer_u, peer_v, final_g):
    nb, l_lat, d = x.shape
    l_ctx = ctx.shape[1]
    depth = w_mod.shape[0]
    n_lat = nb * l_lat
    n_all = n_lat + nb * l_ctx
    tiles_per_batch = l_lat // ROW_TILE
    assert l_lat % ROW_TILE == 0 and l_ctx == ROW_TILE and nb + 1 <= 8

    cs = jnp.concatenate([c, c_ctx[None], jnp.zeros((8 - nb - 1, d), F32)], axis=0)
    mod_all = _mod_vectors(cs, w_mod, b_mod)
    cos_t, sin_t = _rope_tables(l_lat, 1, ROW_TILE)
    ssm_mats = jax.vmap(_ssm_matrices)(ssm_lambda_re, ssm_lambda_im, ssm_log_step, ssm_b_re, ssm_b_im,
                                       ssm_c_re, ssm_c_im)
    stream = (x.reshape(n_lat, d), ctx.reshape(nb * l_ctx, d), n_lat // ROW_TILE)

    for l in range(depth):
        last = l == depth - 1
        n_rows = n_lat if last else n_all
        mods = mod_all[l, :nb + 1].reshape(nb + 1, N_MOD, d)
        q, k, v, us, up = _inproj(*stream, n_all, norm1_g[l][None], mods, w_in[l].astype(BF16),
                                  cos_t, sin_t, tiles_per_batch, nb)
        lam_init = 0.8 - 0.6 * math.exp(-0.3 * l)
        lam = (jnp.exp(jnp.sum(lam_q1[l] * lam_k1[l])) - jnp.exp(jnp.sum(lam_q2[l] * lam_k2[l]))
               + lam_init).reshape(1).astype(F32)
        attn = _attention(q, k, v, lam, subln_g[l][None], nb, l_lat, l_ctx, not last,
                          1.0 - lam_init)
        y = _ssm(us, tuple(m[l] for m in ssm_mats), ssm_d[l][None], nb, l_lat, l_ctx, not last)
        w_blk = jax.scipy.linalg.block_diag(*[pool_w[l, gi] for gi in range(len(POOL_WINDOWS))])
        pool = _pool(up, w_blk.astype(BF16), pool_scale[l][None], nb, l_lat, l_ctx, n_rows)
        xs = _mixout(*stream, attn, y, pool, mods, ssm_w_glu[l].astype(BF16), w_out[l].astype(BF16),
                     tiles_per_batch, nb, n_rows)
        h2, beta, s2, e1, e2 = _peer_scores(xs, norm2_g[l][None], mods, peer_wq[l].astype(BF16),
                                               peer_keys[l].astype(BF16), tiles_per_batch, nb, n_rows)
        xs = _peer_dense(h2, xs, mods, peer_u[l].astype(BF16), _transpose_cast(peer_v[l], BF16),
                         beta, s2, e1, e2, final_g[None], tiles_per_batch, nb, n_rows, last)
        stream = (xs, xs, n_rows // ROW_TILE)
    return xs.reshape(nb, l_lat, d)
```

```python
import functools
import math

import numpy as np
import jax
import jax.numpy as jnp
from jax import lax
from jax.experimental import pallas as pl
from jax.experimental.pallas import tpu as pltpu

F32 = jnp.float32
BF16 = jnp.bfloat16

EPS = 1e-6
GRID_W = 64
N_MOD = 6
ATTN_HD = 64
ATTN_VD = 128
ATTN_VROWS = ATTN_VD + 16
LOG2E = float(np.log2(np.e))
ATTN_HEADS = 4
ATTN_W = 512
ROPE_BASE = 10000.0
SSM_W = 256
SSM_GROUP = 16
SSM_GROUPS = 16
SSM_STATE = 64
SSM_CHUNK = 16
POOL_W = 256
POOL_WINDOWS = (2, 4, 8, 16)
POOL_GROUP = 64
POOL_HALO = 8
SSM_OFF = 3 * ATTN_W
POOL_OFF = SSM_OFF + SSM_W
IN_W = POOL_OFF + POOL_W
PEER_HEADS = 8
PEER_NKEYS = 128
PEER_KDIM = 128
PEER_TOPK = 16
NEG_BIG = -3.0e38
SQRT_HALF = float(np.sqrt(0.5).astype(np.float32))

LANE = 128
ROW_TILE = 256
VMEM_LIMIT = 56 * 1024 * 1024


def _cparams(sem):
    return pltpu.CompilerParams(dimension_semantics=sem, vmem_limit_bytes=VMEM_LIMIT)


def _gelu(x):
    return 0.5 * x * (1.0 + lax.erf(x * SQRT_HALF))


def _norm_mod(x, g, shift, scale):
    ms = jnp.mean(x * x, axis=-1, keepdims=True)
    y = x * lax.rsqrt(ms + EPS) * g
    return y * (1.0 + scale) + shift


def _mod_kernel(s_ref, w_ref, b_ref, o_ref):
    s = s_ref[...]
    s = s * jax.nn.sigmoid(s)
    o_ref[0] = jnp.dot(s.astype(BF16), w_ref[0].astype(BF16), preferred_element_type=F32) + b_ref[0]


def _mod_vectors(cs, w_mod, b_mod):
    depth, d, n = w_mod.shape
    tn = 1536
    return pl.pallas_call(
        _mod_kernel,
        grid=(depth, n // tn),
        in_specs=[pl.BlockSpec((8, d), lambda l, j: (0, 0)),
                  pl.BlockSpec((1, d, tn), lambda l, j: (l, 0, j)),
                  pl.BlockSpec((1, 1, tn), lambda l, j: (l, 0, j))],
        out_specs=pl.BlockSpec((1, 8, tn), lambda l, j: (l, 0, j)),
        out_shape=jax.ShapeDtypeStruct((depth, 8, n), F32),
        compiler_params=_cparams(("arbitrary", "arbitrary")),
        name="mod_vectors",
    )(cs, w_mod, b_mod.reshape(depth, 1, n))


def _stream_tile(xa_ref, xb_ref, n_a):
    return jnp.where(pl.program_id(0) < n_a, xa_ref[...], xb_ref[...])


def _stream_specs(xa, xb, n_a, tm):
    d = xa.shape[1]
    return [pl.BlockSpec((tm, d), lambda i: (jnp.minimum(i, n_a - 1), 0)),
            pl.BlockSpec((tm, d), lambda i: (jnp.maximum(i - n_a, 0), 0))]


def _inproj_kernel(xa_ref, xb_ref, g_ref, mod_ref, w_ref, cos_ref, sin_ref,
                   q_ref, k_ref, v_ref, us_ref, up_ref, *, n_a):
    tm = xa_ref.shape[0]
    h = _norm_mod(_stream_tile(xa_ref, xb_ref, n_a), g_ref[...], mod_ref[0, 0:1, :], mod_ref[0, 1:2, :])
    p = jnp.dot(h.astype(BF16), w_ref[...], preferred_element_type=F32)
    c = cos_ref[...]
    s = sin_ref[...]
    lane = lax.broadcasted_iota(jnp.int32, (tm, LANE), 1)
    first = (lane % 32) < 16
    for off, ref, sc in ((0, q_ref, ATTN_HD ** -0.5 * LOG2E), (ATTN_W, k_ref, 1.0)):
        for blk in range(ATTN_W // LANE):
            xb = p[:, off + LANE * blk: off + LANE * (blk + 1)]
            partner = jnp.where(first, pltpu.roll(xb, LANE - 16, 1), pltpu.roll(xb, 16, 1))
            ref[:, LANE * blk: LANE * (blk + 1)] = ((xb * c + partner * s) * sc).astype(BF16)
    vt = p[:, 2 * ATTN_W:SSM_OFF].T.astype(BF16)
    ones_rows = (lax.broadcasted_iota(jnp.int32, (ATTN_VROWS - ATTN_VD, tm), 0) == 0).astype(BF16)
    for hd in range(ATTN_HEADS):
        v_ref[0, hd * ATTN_VROWS:hd * ATTN_VROWS + ATTN_VD, :] = vt[hd * ATTN_VD:(hd + 1) * ATTN_VD]
        v_ref[0, hd * ATTN_VROWS + ATTN_VD:(hd + 1) * ATTN_VROWS, :] = ones_rows
    us_ref[...] = p[:, SSM_OFF:POOL_OFF]
    up_ref[...] = p[:, POOL_OFF:IN_W]


def _seg_map(tiles_per_batch, nb):
    def seg(i):
        return jnp.minimum(i // tiles_per_batch, nb)
    return seg


def _inproj(xa, xb, n_a, t, g, mods, w, cos_t, sin_t, tiles_per_batch, nb):
    d = xa.shape[1]
    tm = ROW_TILE
    seg = _seg_map(tiles_per_batch, nb)
    row = lambda i: (i, 0)
    rope_row = lambda i: (jnp.where(i < nb * tiles_per_batch, i % tiles_per_batch, tiles_per_batch), 0)
    return pl.pallas_call(
        functools.partial(_inproj_kernel, n_a=n_a),
        grid=(t // tm,),
        in_specs=_stream_specs(xa, xb, n_a, tm) + [
                  pl.BlockSpec((1, d), lambda i: (0, 0)),
                  pl.BlockSpec((1, N_MOD, d), lambda i: (seg(i), 0, 0)),
                  pl.BlockSpec((d, IN_W), lambda i: (0, 0)),
                  pl.BlockSpec((tm, LANE), rope_row),
                  pl.BlockSpec((tm, LANE), rope_row)],
        out_specs=[pl.BlockSpec((tm, ATTN_W), row),
                   pl.BlockSpec((tm, ATTN_W), row),
                   pl.BlockSpec((1, ATTN_HEADS * ATTN_VROWS, tm), lambda i: (i, 0, 0)),
                   pl.BlockSpec((tm, SSM_W), row),
                   pl.BlockSpec((tm, POOL_W), row)],
        out_shape=[jax.ShapeDtypeStruct((t, ATTN_W), BF16),
                   jax.ShapeDtypeStruct((t, ATTN_W), BF16),
                   jax.ShapeDtypeStruct((t // tm, ATTN_HEADS * ATTN_VROWS, tm), BF16),
                   jax.ShapeDtypeStruct((t, SSM_W), F32),
                   jax.ShapeDtypeStruct((t, POOL_W), F32)],
        compiler_params=_cparams(("arbitrary",)),
        name="inproj",
    )(xa, xb, g, mods, w, cos_t, sin_t)


def _attn_stages(q, tq):
    lane = lax.broadcasted_iota(jnp.int32, (tq, LANE), 1)
    zero = jnp.zeros_like(q)
    qs = jnp.concatenate([jnp.where(lane < ATTN_HD, q, zero),
                          jnp.where(lane >= ATTN_HD, q, zero)], axis=0)

    def scores(kb):
        return lax.dot_general(kb, qs, (((1,), (1,)), ((), ())), preferred_element_type=F32)

    def softmax(m, s):
        m_new = jnp.maximum(m, jnp.max(s, axis=0, keepdims=True))
        return m_new, jnp.exp2(m - m_new), jnp.exp2(s - m_new).astype(BF16)

    def weighted(acc, alpha, pb, vts):
        acc = alpha * acc
        rows = pb.shape[0] // len(vts)
        for c, vt in enumerate(vts):
            acc = acc + jnp.dot(vt, pb[c * rows:(c + 1) * rows], preferred_element_type=F32)
        return acc

    return scores, softmax, weighted


def _attn_finish(lam, acc, g, tq, out_scale):
    o = acc[:ATTN_VD] / acc[ATTN_VD:ATTN_VD + 1]
    o = (o[:, :tq] - lam * o[:, tq:]).T
    ms = jnp.mean(o * o, axis=-1, keepdims=True)
    return (o * lax.rsqrt(ms + EPS) * g * out_scale).astype(BF16)


def _attn_ctx_kernel(lam_ref, q_ref, kc_ref, vc_ref, g_ref, o_ref, *, out_scale):
    tq = q_ref.shape[0]
    scores, softmax, weighted = _attn_stages(q_ref[...], tq)
    m = jnp.full((1, 2 * tq), NEG_BIG, F32)
    acc = jnp.zeros((ATTN_VROWS, 2 * tq), F32)
    m, alpha, pb = softmax(m, scores(kc_ref[...]))
    acc = weighted(acc, alpha, pb, [vc_ref[0]])
    o_ref[...] = _attn_finish(lam_ref[0], acc, g_ref[...], tq, out_scale)


def _attn_kernel(lam_ref, q_ref, kl_ref, kc_ref, vl_ref, vc_ref, g_ref, o_ref,
                 s_a, s_b, p_a, p_b, acc_ref, *, n_lat_k, tk, out_scale):
    tq = q_ref.shape[0]
    scores, softmax, weighted = _attn_stages(q_ref[...], tq)
    n_sub = tk // vc_ref.shape[-1]

    def stage_scores(t, s_buf):
        start = pl.multiple_of(t * tk, tk)
        s_buf[...] = scores(kl_ref[pl.ds(start, tk), :])

    def stage_softmax(m, s_buf, p_buf):
        m, alpha, pb = softmax(m, s_buf[...])
        p_buf[...] = pb
        return m, alpha

    def stage_values(alpha, p_buf, t):
        acc_ref[...] = weighted(acc_ref[...], alpha, p_buf[...],
                                [vl_ref[t * n_sub + c] for c in range(n_sub)])

    m = jnp.full((1, 2 * tq), NEG_BIG, F32)
    m, alpha, pb = softmax(m, scores(kc_ref[...]))
    acc_ref[...] = weighted(jnp.zeros((ATTN_VROWS, 2 * tq), F32), alpha, pb, [vc_ref[0]])

    if n_lat_k % 2 == 0:
        stage_scores(0, s_a)
        stage_scores(1, s_b)
        m, alpha = stage_softmax(m, s_a, p_a)

        def pair(i, carry):
            m, alpha = carry
            k = 2 * i
            stage_scores(k + 2, s_a)
            m, alpha_n = stage_softmax(m, s_b, p_b)
            stage_values(alpha, p_a, k)
            stage_scores(k + 3, s_b)
            m, alpha_nn = stage_softmax(m, s_a, p_a)
            stage_values(alpha_n, p_b, k + 1)
            return m, alpha_nn

        m, alpha = lax.fori_loop(0, n_lat_k // 2 - 1, pair, (m, alpha))
        m, alpha_n = stage_softmax(m, s_b, p_b)
        stage_values(alpha, p_a, n_lat_k - 2)
        stage_values(alpha_n, p_b, n_lat_k - 1)
    else:
        for t in range(n_lat_k):
            stage_scores(t, s_a)
            m, alpha = stage_softmax(m, s_a, p_a)
            stage_values(alpha, p_a, t)
    o_ref[...] = _attn_finish(lam_ref[0], acc_ref[...], g_ref[...], tq, out_scale)


def _attention(q, k, vt, lam, g, nb, l_lat, l_ctx, with_ctx_queries, out_scale):
    tq_ctx = ROW_TILE
    tq = 1024 if l_lat % 1024 == 0 else ROW_TILE
    tk = 1024 if l_lat % 2048 == 0 else (512 if l_lat % 512 == 0 else l_lat)
    vt_tile = vt.shape[-1]
    assert l_ctx == tq_ctx and l_lat % tq == 0 and vt_tile == l_ctx and tk % vt_tile == 0
    n_lat_q = l_lat // tq
    n_vt_lat = l_lat // vt_tile
    ctx_blk0 = nb * l_lat // l_ctx
    smem = pl.BlockSpec(memory_space=pltpu.SMEM)
    gain = pl.BlockSpec((1, LANE), lambda b, h, i: (0, 0))
    kc_spec = pl.BlockSpec((l_ctx, LANE), lambda b, h, i: (ctx_blk0 + b, h))
    vc_spec = pl.BlockSpec((1, ATTN_VROWS, vt_tile), lambda b, h, i: (nb * n_vt_lat + b, h, 0))
    sem = _cparams(("arbitrary", "arbitrary", "arbitrary"))
    lat = pl.pallas_call(
        functools.partial(_attn_kernel, n_lat_k=l_lat // tk, tk=tk, out_scale=out_scale),
        grid=(nb, ATTN_HEADS, n_lat_q),
        in_specs=[smem,
                  pl.BlockSpec((tq, LANE), lambda b, h, i: (b * n_lat_q + i, h)),
                  pl.BlockSpec((l_lat, LANE), lambda b, h, i: (b, h)),
                  kc_spec,
                  pl.BlockSpec((n_vt_lat, ATTN_VROWS, vt_tile), lambda b, h, i: (b, h, 0)),
                  vc_spec, gain],
        out_specs=pl.BlockSpec((tq, LANE), lambda b, h, i: (b * n_lat_q + i, h)),
        out_shape=jax.ShapeDtypeStruct((nb * l_lat, ATTN_W), BF16),
        scratch_shapes=[pltpu.VMEM((tk, 2 * tq), F32), pltpu.VMEM((tk, 2 * tq), F32),
                        pltpu.VMEM((tk, 2 * tq), BF16), pltpu.VMEM((tk, 2 * tq), BF16),
                        pltpu.VMEM((ATTN_VROWS, 2 * tq), F32)],
        compiler_params=sem,
        name="diff_attention",
    )(lam, q, k, k, vt, vt, g)
    if not with_ctx_queries:
        return lat, None
    ctx = pl.pallas_call(
        functools.partial(_attn_ctx_kernel, out_scale=out_scale),
        grid=(nb, ATTN_HEADS, 1),
        in_specs=[smem,
                  pl.BlockSpec((tq_ctx, LANE), lambda b, h, i: (ctx_blk0 + b, h)),
                  kc_spec, vc_spec, gain],
        out_specs=pl.BlockSpec((tq_ctx, LANE), lambda b, h, i: (b, h)),
        out_shape=jax.ShapeDtypeStruct((nb * l_ctx, ATTN_W), BF16),
        compiler_params=sem,
        name="diff_attention_ctx",
    )(lam, q, k, vt, g)
    return lat, ctx


def _ssm_kernel(ul_ref, uc_ref, m_ref, bm_ref, cm_ref, a_ref, d_ref, yl_ref, yc_ref,
                s_scr, h_scr, *, ctx_out):
    tc, pw = SSM_CHUNK, 2 * SSM_GROUP
    nl = ul_ref.shape[0] // tc
    nc = uc_ref.shape[0] // tc
    w = m_ref.shape[-1]
    half = s_scr.shape[-1] // 2
    sub = pl.program_id(1) % (LANE // pw)
    nt_dims = (((1,), (1,)), ((), ()))
    ch = lax.broadcasted_iota(jnp.int32, (LANE, 1), 0) - sub * pw
    valid = (ch >= 0) & (ch < pw)
    tgt0 = (ch >> 4) * (tc * SSM_GROUP) + (ch & (SSM_GROUP - 1))
    col = lax.broadcasted_iota(jnp.int32, (1, w), 1)

    def perm(t):
        return jnp.where(valid & (col == tgt0 + t * SSM_GROUP), 1.0, 0.0).astype(BF16)

    def fold(ref, n):
        acc = jnp.zeros((n, w), F32)
        for t in range(tc):
            acc = acc + jnp.dot(ref[pl.ds(t, n, stride=tc), :].astype(BF16), perm(t),
                                preferred_element_type=F32)
        return acc.astype(BF16)

    def unfold_add(y, ref, n):
        hi = y.astype(BF16)
        r1 = y - hi.astype(F32)
        mid = r1.astype(BF16)
        lo = (r1 - mid.astype(F32)).astype(BF16)
        for t in range(tc):
            p = perm(t)
            o = sum(lax.dot_general(part, p, nt_dims, preferred_element_type=F32)
                    for part in (hi, mid, lo))
            ref[pl.ds(t, n, stride=tc), :] += o

    @pl.when(sub == 0)
    def _():
        yl_ref[...] = ul_ref[...] * d_ref[...]
        yc_ref[...] = uc_ref[...] * d_ref[...] if ctx_out else jnp.zeros_like(yc_ref)

    ulb = fold(ul_ref, nl)
    ucb = fold(uc_ref, nc)
    yl = jnp.dot(ulb, m_ref[0], preferred_element_type=F32)
    if ctx_out:
        yc = jnp.dot(ucb, m_ref[0], preferred_element_type=F32)
    for dr in range(2):
        s_scr[dr, 0:nc, :] = jnp.dot(ucb, bm_ref[dr, 0], preferred_element_type=F32)
        s_scr[dr, nc:nc + nl, :] = jnp.dot(ulb, bm_ref[dr, 0], preferred_element_type=F32)
    a = a_ref[0]
    afr, afi, arr, ari = a[0:1], a[1:2], a[2:3], a[3:4]

    def make_body(base, n):
        def body(t, carry):
            fr, fi, rr, ri = carry
            rf = base + t
            rv = base + n - 1 - t
            h_scr[0, pl.ds(rf, 1), :] = jnp.concatenate([fr, fi], axis=-1)
            h_scr[1, pl.ds(rv, 1), :] = jnp.concatenate([rr, ri], axis=-1)
            sf = s_scr[0, pl.ds(rf, 1), :]
            sv = s_scr[1, pl.ds(rv, 1), :]
            nfr = afr * fr - afi * fi + sf[:, :half]
            nfi = afr * fi + afi * fr + sf[:, half:]
            nrr = arr * rr - ari * ri + sv[:, :half]
            nri = arr * ri + ari * rr + sv[:, half:]
            return nfr, nfi, nrr, nri
        return body

    z = jnp.zeros((1, half), F32)
    carry = lax.fori_loop(0, nc, make_body(0, nc), (z, z, z, z))
    lax.fori_loop(0, nl, make_body(nc, nl), carry)
    for dr in range(2):
        yl = yl + jnp.dot(h_scr[dr, nc:nc + nl, :].astype(BF16), cm_ref[dr, 0],
                          preferred_element_type=F32)
        if ctx_out:
            yc = yc + jnp.dot(h_scr[dr, 0:nc, :].astype(BF16), cm_ref[dr, 0],
                              preferred_element_type=F32)
    unfold_add(yl, yl_ref, nl)
    if ctx_out:
        unfold_add(yc, yc_ref, nc)


def _ssm_matrices(lam_re, lam_im, log_step, b_re, b_im, c_re, c_im):
    tc = SSM_CHUNK
    g, p, hh = SSM_GROUPS, SSM_STATE, SSM_GROUP
    npair = g // 2
    step = jnp.exp(log_step)[..., None]
    den = lam_re * lam_re + lam_im * lam_im

    def power(k):
        er = jnp.exp(lam_re * step * k)
        return er * jnp.cos(lam_im * step * k), er * jnp.sin(lam_im * step * k)

    ar, ai = power(1.0)
    nr = ar - 1.0
    cr_ = (nr * lam_re + ai * lam_im) / den
    ci_ = (ai * lam_re - nr * lam_im) / den
    bbr = cr_[..., None] * b_re - ci_[..., None] * b_im
    bbi = cr_[..., None] * b_im + ci_[..., None] * b_re
    ks = jnp.arange(tc + 1, dtype=F32)
    pw = jax.vmap(power)(ks)
    pwr, pwi = pw
    ct_re = jnp.swapaxes(c_re, -1, -2)
    ct_im = jnp.swapaxes(c_im, -1, -2)
    cb_re = (ct_re[..., :, None] * bbr[..., None, :] - ct_im[..., :, None] * bbi[..., None, :])
    cb_im = (ct_im[..., :, None] * bbr[..., None, :] + ct_re[..., :, None] * bbi[..., None, :])
    hp = lax.Precision.HIGHEST
    kk = (jnp.einsum('kdgp,dgpq->kdgq', pwr[:tc], cb_re.reshape(2, g, p, hh * hh), precision=hp)
          - jnp.einsum('kdgp,dgpq->kdgq', pwi[:tc], cb_im.reshape(2, g, p, hh * hh), precision=hp))
    kk = kk.reshape(tc, 2, g, hh, hh)
    zpad = jnp.zeros((tc - 1,) + kk.shape[2:], F32)
    lagtab = (jnp.concatenate([zpad, kk[:, 0]], 0)
              + jnp.concatenate([kk[::-1, 1], zpad], 0))
    lagtab = jnp.transpose(lagtab, (1, 3, 0, 2)).reshape(g, hh, (2 * tc - 1) * hh)
    mm = jnp.stack([lagtab[:, :, (tc - 1 - s) * hh:(2 * tc - 1 - s) * hh] for s in range(tc)], axis=1)
    mm = mm.reshape(g, tc * hh, tc * hh)
    zero = jnp.zeros_like(mm[0::2])
    m2 = jnp.concatenate([jnp.concatenate([mm[0::2], zero], -1),
                          jnp.concatenate([zero, mm[1::2]], -1)], -2)
    def bmat(dr, exps):
        pr = pwr[exps, dr]
        pi = pwi[exps, dr]
        re = pr[..., None] * bbr[dr][None] - pi[..., None] * bbi[dr][None]
        im = pr[..., None] * bbi[dr][None] + pi[..., None] * bbr[dr][None]
        re = jnp.transpose(re, (1, 0, 3, 2)).reshape(g, tc * hh, p)
        im = jnp.transpose(im, (1, 0, 3, 2)).reshape(g, tc * hh, p)
        z = jnp.zeros_like(re[0::2])
        top = jnp.concatenate([re[0::2], z, im[0::2], z], -1)
        bot = jnp.concatenate([z, re[1::2], z, im[1::2]], -1)
        return jnp.concatenate([top, bot], -2)
    bm = jnp.stack([bmat(0, jnp.arange(tc - 1, -1, -1)), bmat(1, jnp.arange(tc))])
    def cmat(dr, exps):
        pr = jnp.transpose(pwr[exps, dr], (1, 2, 0))[..., None]
        pi = jnp.transpose(pwi[exps, dr], (1, 2, 0))[..., None]
        cr, ci = ct_re[dr][:, :, None, :], ct_im[dr][:, :, None, :]
        re = (cr * pr - ci * pi).reshape(g, p, tc * hh)
        im = (cr * pi + ci * pr).reshape(g, p, tc * hh)
        z = jnp.zeros_like(re[0::2])
        return jnp.concatenate([jnp.concatenate([re[0::2], z], -1),
                                jnp.concatenate([z, re[1::2]], -1),
                                jnp.concatenate([-im[0::2], z], -1),
                                jnp.concatenate([z, -im[1::2]], -1)], -2)
    cm = jnp.stack([cmat(0, jnp.arange(1, tc + 1)), cmat(1, jnp.arange(tc, 0, -1))])
    a16r = pwr[tc].reshape(2, npair, 2 * p)
    a16i = pwi[tc].reshape(2, npair, 2 * p)
    a16 = jnp.stack([a16r[0], a16i[0], a16r[1], a16i[1]], axis=1)
    return m2.astype(BF16), bm.astype(BF16), cm.astype(BF16), a16


def _ssm(us, mats, d, nb, l_lat, l_ctx, ctx_out):
    m2, bm, cm, a16 = mats
    tc, hh = SSM_CHUNK, SSM_GROUP
    npair = SSM_GROUPS // 2
    w = 2 * tc * hh
    per_col = LANE // (2 * hh)
    nl, nc = l_lat // tc, l_ctx // tc
    ctx_blk0 = nb * l_lat // l_ctx
    yl, yc = pl.pallas_call(
        functools.partial(_ssm_kernel, ctx_out=ctx_out),
        grid=(nb, npair),
        in_specs=[pl.BlockSpec((l_lat, LANE), lambda b, q: (b, q // per_col)),
                  pl.BlockSpec((l_ctx, LANE), lambda b, q: (ctx_blk0 + b, q // per_col)),
                  pl.BlockSpec((1, w, w), lambda b, q: (q, 0, 0)),
                  pl.BlockSpec((2, 1, w, 4 * SSM_STATE), lambda b, q: (0, q, 0, 0)),
                  pl.BlockSpec((2, 1, 4 * SSM_STATE, w), lambda b, q: (0, q, 0, 0)),
                  pl.BlockSpec((1, 4, 2 * SSM_STATE), lambda b, q: (q, 0, 0)),
                  pl.BlockSpec((1, LANE), lambda b, q: (0, q // per_col))],
        out_specs=[pl.BlockSpec((l_lat, LANE), lambda b, q: (b, q // per_col)),
                   pl.BlockSpec((l_ctx, LANE), lambda b, q: (b, q // per_col))],
        out_shape=[jax.ShapeDtypeStruct((nb * l_lat, SSM_W), F32),
                   jax.ShapeDtypeStruct((nb * l_ctx, SSM_W), F32)],
        scratch_shapes=[pltpu.VMEM((2, nc + nl, 4 * SSM_STATE), F32),
                        pltpu.VMEM((2, nc + nl, 4 * SSM_STATE), F32)],
        compiler_params=_cparams(("arbitrary", "arbitrary")),
        name="s5_scan",
    )(us, us, m2, bm, cm, a16, d)
    return yl, (yc if ctx_out else None)


def _pool_value(prev_ref, cur_ref, next_ref, w_ref, scale_ref, ext,
                *, n_lat_tiles, tiles_per_lat, l_lat, l_ctx):
    r = cur_ref.shape[0]
    hl = POOL_HALO
    i = pl.program_id(0)
    is_lat = i < n_lat_tiles
    pos = jnp.where(is_lat, i % tiles_per_lat, 0)
    n_tiles = jnp.where(is_lat, tiles_per_lat, l_ctx // r)
    seq_len = jnp.where(is_lat, l_lat, l_ctx)
    zeros = jnp.zeros((hl, POOL_W), F32)
    ext[0:hl, :] = jnp.where(pos > 0, prev_ref[...], zeros)
    ext[hl:hl + r, :] = cur_ref[...]
    ext[hl + r:hl + r + hl, :] = jnp.where(pos < n_tiles - 1, next_ref[...], zeros)
    u = cur_ref[...]
    t = pos * r + lax.broadcasted_iota(jnp.int32, (r, 1), 0)
    lane = lax.broadcasted_iota(jnp.int32, (r, POOL_W), 1)

    def win(k):
        return ext[hl + k:hl + k + r, :]

    acc = win(-1) + u
    mean = jnp.zeros((r, POOL_W), F32)
    lo_k, hi_k = -1, 0
    for gi, wn in enumerate(POOL_WINDOWS):
        hw = wn // 2
        while lo_k > -hw:
            lo_k -= 1
            acc = acc + win(lo_k)
        while hi_k < hw - 1:
            hi_k += 1
            acc = acc + win(hi_k)
        cnt = (jnp.minimum(t + hw, seq_len) - jnp.maximum(t - hw, 0)).astype(F32)
        sel = (lane >= gi * POOL_GROUP) & (lane < (gi + 1) * POOL_GROUP)
        mean = jnp.where(sel, acc / cnt, mean)
    dlt = (mean - u).astype(BF16)
    return jnp.dot(dlt, w_ref[...], preferred_element_type=F32) * scale_ref[...]


def _pool_specs(up):
    r = ROW_TILE
    hl = POOL_HALO
    last8 = up.shape[0] // hl - 1
    return [pl.BlockSpec((hl, POOL_W), lambda i: (jnp.maximum(i * (r // hl) - 1, 0), 0)),
            pl.BlockSpec((r, POOL_W), lambda i: (i, 0)),
            pl.BlockSpec((hl, POOL_W), lambda i: (jnp.minimum((i + 1) * (r // hl), last8), 0)),
            pl.BlockSpec((POOL_W, POOL_W), lambda i: (0, 0)),
            pl.BlockSpec((1, POOL_W), lambda i: (0, 0))]


def _mixout_kernel(xa_ref, xb_ref, aa_ref, ab_ref, ya_ref, yb_ref, prev_ref, cur_ref, next_ref,
                   wpool_ref, pscale_ref, mod_ref, wglu_ref, wout_ref, o_ref, ext,
                   *, n_a, n_lat, pool_kw):
    pool = _pool_value(prev_ref, cur_ref, next_ref, wpool_ref, pscale_ref, ext, **pool_kw)
    yg = _gelu(_stream_tile(ya_ref, yb_ref, n_lat))
    z = yg * jax.nn.sigmoid(jnp.dot(yg.astype(BF16), wglu_ref[...], preferred_element_type=F32))
    r = jnp.dot(_stream_tile(aa_ref, ab_ref, n_lat), wout_ref[0:ATTN_W, :],
                preferred_element_type=F32)
    r = r + jnp.dot(z.astype(BF16), wout_ref[ATTN_W:ATTN_W + SSM_W, :], preferred_element_type=F32)
    r = r + jnp.dot(pool.astype(BF16), wout_ref[ATTN_W + SSM_W:, :],
                    preferred_element_type=F32)
    o_ref[...] = _stream_tile(xa_ref, xb_ref, n_a) + mod_ref[0, 2:3, :] * r


def _mixout(xa, xb, n_a, attn, y, up, wpool, pscale, mods, wglu, wout, tiles_per_batch, nb, n_rows,
            l_lat, l_ctx):
    d = xa.shape[1]
    tm = ROW_TILE
    seg = _seg_map(tiles_per_batch, nb)
    row = lambda i: (i, 0)
    n_lat = nb * tiles_per_batch
    attn = (attn[0], attn[0] if attn[1] is None else attn[1])
    y = (y[0], y[0] if y[1] is None else y[1])
    return pl.pallas_call(
        functools.partial(_mixout_kernel, n_a=n_a, n_lat=n_lat,
                          pool_kw=dict(n_lat_tiles=n_lat, tiles_per_lat=tiles_per_batch,
                                       l_lat=l_lat, l_ctx=l_ctx)),
        grid=(n_rows // tm,),
        in_specs=_stream_specs(xa, xb, n_a, tm) + _stream_specs(*attn, n_lat, tm)
                 + _stream_specs(*y, n_lat, tm) + _pool_specs(up) + [
                  pl.BlockSpec((1, N_MOD, d), lambda i: (seg(i), 0, 0)),
                  pl.BlockSpec((SSM_W, SSM_W), lambda i: (0, 0)),
                  pl.BlockSpec((d, d), lambda i: (0, 0))],
        out_specs=pl.BlockSpec((tm, d), row),
        out_shape=jax.ShapeDtypeStruct((n_rows, d), F32),
        scratch_shapes=[pltpu.VMEM((tm + 2 * POOL_HALO, POOL_W), F32)],
        compiler_params=_cparams(("arbitrary",)),
        name="mix_out",
    )(xa, xb, *attn, *y, up, up, up, wpool, pscale, mods, wglu, wout)


def _oddeven_merge(lo, hi, r):
    step = r * 2
    if step < hi - lo:
        yield from _oddeven_merge(lo, hi, step)
        yield from _oddeven_merge(lo + r, hi, step)
        yield from [(i, i + r) for i in range(lo + r, hi - r, step)]
    else:
        yield (lo, lo + r)


def _oddeven_sort(lo, hi):
    if hi - lo >= 1:
        mid = lo + (hi - lo) // 2
        yield from _oddeven_sort(lo, mid)
        yield from _oddeven_sort(mid + 1, hi)
        yield from _oddeven_merge(lo, hi, 1)


_SORT16 = tuple(_oddeven_sort(0, PEER_TOPK - 1))
_BITONIC16 = tuple((i, i + d) for d in (8, 4, 2, 1) for i in range(PEER_TOPK) if not i & d)


def _compare_exchange(rows, net):
    rows = list(rows)
    for i, j in net:
        hi, lo = jnp.maximum(rows[i], rows[j]), jnp.minimum(rows[i], rows[j])
        rows[i], rows[j] = hi, lo
    return rows


def _top_sorted(s, k):
    assert k == PEER_TOPK and s.shape[0] == 8 * PEER_TOPK
    rows = _compare_exchange([s[8 * v:8 * v + 8] for v in range(PEER_TOPK)], _SORT16)
    for shift in (4, 2, 1):
        other = [pltpu.roll(r, shift, 0) for r in rows]
        rows = _compare_exchange([jnp.maximum(rows[i], other[PEER_TOPK - 1 - i])
                                  for i in range(PEER_TOPK)], _BITONIC16)
    return [r[0:1] for r in rows]


def _peer_score_kernel(x_ref, g_ref, mod_ref, wq_ref, keys_ref,
                       h_ref, beta_ref, s2_ref, e1_ref, e2_ref, q_scr):
    tm = x_ref.shape[0]
    h = _norm_mod(x_ref[...], g_ref[...], mod_ref[0, 3:4, :], mod_ref[0, 4:5, :]).astype(BF16)
    h_ref[...] = h
    q = jnp.dot(h, wq_ref[...], preferred_element_type=F32)
    for hc in range(2 * PEER_HEADS):
        q_scr[hc] = q[:, hc * PEER_KDIM:(hc + 1) * PEER_KDIM].astype(BF16)
    row8 = lax.broadcasted_iota(jnp.int32, (8, tm), 0)

    def head(hd, _):
        nt = (((1,), (1,)), ((), ()))
        s1 = lax.dot_general(keys_ref[hd, 0], q_scr[2 * hd], nt,
                             preferred_element_type=F32)
        s2 = lax.dot_general(keys_ref[hd, 1], q_scr[2 * hd + 1], nt,
                             preferred_element_type=F32)
        a = _top_sorted(s1, PEER_TOPK)
        b = _top_sorted(s2, PEER_TOPK)
        acat = jnp.concatenate(a, axis=0)
        bcat = jnp.concatenate(b, axis=0)
        pieces = [a[0] + bcat]
        for i in range(2, 9):
            piece = a[i - 1] + bcat[0:8]
            n_valid = PEER_TOPK // i
            pieces.append(piece if n_valid >= 8 else jnp.where(row8 < n_valid, piece, NEG_BIG))
        pieces.append(acat[8:16] + b[0])
        cand = jnp.concatenate(pieces, axis=0)
        work = cand
        cum = jnp.zeros((1, tm), F32)
        tau = jnp.full((1, tm), NEG_BIG, F32)
        for _ in range(PEER_TOPK):
            m = jnp.max(work, axis=0, keepdims=True)
            eq = work == m
            new = cum + jnp.sum(eq.astype(F32), axis=0, keepdims=True)
            tau = jnp.where((cum < PEER_TOPK) & (new >= PEER_TOPK), m, tau)
            work = jnp.where(eq, NEG_BIG, work)
            cum = new
        top = a[0] + b[0]
        z = jnp.sum(jnp.where(cand >= tau, jnp.exp(cand - top), 0.0), axis=0, keepdims=True)
        big = -NEG_BIG
        b_rows = [bcat] + [bcat[0:8]] * 7
        beta_r = [jnp.min(jnp.where(pc >= tau, br, big), axis=0, keepdims=True)
                  for pc, br in zip(pieces[:8], b_rows)]
        tail = jnp.where(pieces[8] >= tau, b[0], big)
        beta_r += [tail[r:r + 1] for r in range(8)]
        beta = jnp.full_like(s1, big)
        for r in range(PEER_TOPK):
            beta = jnp.where(s1 == a[r], beta_r[r], beta)
        beta_ref[hd] = beta
        s2_ref[hd] = s2
        e1_ref[hd] = jnp.exp(s1 - a[0])
        e2_ref[hd] = jnp.exp(s2 - b[0]) * (1.0 / z)
        return 0

    lax.fori_loop(0, PEER_HEADS, head, 0, unroll=4)


def _peer_scores(x, g, mods, wq, keys, tiles_per_batch, nb, n_rows):
    d = x.shape[1]
    tm = ROW_TILE
    seg = _seg_map(tiles_per_batch, nb)
    nq = wq.shape[1]
    col = lambda i: (0, 0, i)
    big = jax.ShapeDtypeStruct((PEER_HEADS, PEER_NKEYS, n_rows), F32)
    return pl.pallas_call(
        _peer_score_kernel,
        grid=(n_rows // tm,),
        in_specs=[pl.BlockSpec((tm, d), lambda i: (i, 0)),
                  pl.BlockSpec((1, d), lambda i: (0, 0)),
                  pl.BlockSpec((1, N_MOD, d), lambda i: (seg(i), 0, 0)),
                  pl.BlockSpec((d, nq), lambda i: (0, 0)),
                  pl.BlockSpec((PEER_HEADS, 2, PEER_NKEYS, PEER_KDIM), lambda i: (0, 0, 0, 0))],
        out_specs=[pl.BlockSpec((tm, d), lambda i: (i, 0)),
                   pl.BlockSpec((PEER_HEADS, PEER_NKEYS, tm), col),
                   pl.BlockSpec((PEER_HEADS, PEER_NKEYS, tm), col),
                   pl.BlockSpec((PEER_HEADS, PEER_NKEYS, tm), col),
                   pl.BlockSpec((PEER_HEADS, PEER_NKEYS, tm), col)],
        out_shape=[jax.ShapeDtypeStruct((n_rows, d), BF16), big, big, big, big],
        scratch_shapes=[pltpu.VMEM((2 * PEER_HEADS, tm, PEER_KDIM), BF16)],
        compiler_params=_cparams(("arbitrary",)),
        name="peer_scores",
    )(x, g, mods, wq, keys)


def _transpose_cast_kernel(x_ref, o_ref):
    o_ref[...] = x_ref[...].T.astype(o_ref.dtype)


def _transpose_cast(x, dtype, rows_per_step=512):
    n, d = x.shape
    return pl.pallas_call(
        _transpose_cast_kernel,
        grid=(n // rows_per_step,),
        in_specs=[pl.BlockSpec((rows_per_step, d), lambda i: (i, 0))],
        out_specs=pl.BlockSpec((d, rows_per_step), lambda i: (0, i)),
        out_shape=jax.ShapeDtypeStruct((d, n), dtype),
        compiler_params=_cparams(("arbitrary",)),
        name="transpose_cast",
    )(x)


PEER_CHUNK_KEYS = 8
PEER_TOK_TILE = 512
PEER_SUB = 256


def _peer_dense_kernel(h_ref, x_ref, mod_ref, u_ref, vt_ref, beta_ref, s2_ref, e1_ref, e2_ref,
                       fg_ref, o_ref, acc_ref, a0, a1, w0, w1, row_scr,
                       *, n_chunks, final_norm):
    n_i = PEER_CHUNK_KEYS
    tm = h_ref.shape[0]
    g = pl.program_id(0)
    nt_dims = (((1,), (1,)), ((), ()))
    n_k = u_ref.shape[1] // PEER_SUB
    n_lb = tm // LANE
    assert n_k == n_lb == vt_ref.shape[1] // PEER_SUB and n_i * PEER_NKEYS == u_ref.shape[0]
    n_mt = u_ref.shape[0] // PEER_SUB

    @pl.when(g == 0)
    def _():
        acc_ref[...] = jnp.zeros_like(acc_ref)
        a1[...] = jnp.zeros_like(a1)
        w0[...] = jnp.zeros_like(w0)

    for hd in range(PEER_HEADS):
        betab = beta_ref[hd]
        e1b = e1_ref[hd]
        for ii in range(n_i):
            row_scr[hd, ii, 0, :, 0:tm] = jnp.broadcast_to(betab[ii:ii + 1], (8, tm))
            row_scr[hd, ii, 1, :, 0:tm] = jnp.broadcast_to(e1b[ii:ii + 1], (8, tm))

    def run(a_wr, a_rd, w_wr, w_rd):
        n_trips = tm // PEER_SUB
        slabs_per_trip = PEER_NKEYS // 16 // n_trips
        for lb in range(n_lb):
            th, lo = divmod(lb * LANE, PEER_SUB)
            ls = slice(lb * LANE, (lb + 1) * LANE)
            lh = slice(lo, lo + LANE)
            ms = slice(lb * PEER_SUB, (lb + 1) * PEER_SUB)

            def trip(tr, _):
                t0 = pl.multiple_of(tr * PEER_SUB, PEER_SUB)
                part_a = lax.dot_general(u_ref[ms, :], h_ref[pl.ds(t0, PEER_SUB), :], nt_dims,
                                         preferred_element_type=F32)
                part_v = jnp.dot(vt_ref[ms, :], w_rd[tr], preferred_element_type=F32)
                for sl in range(slabs_per_trip):
                    j0 = pl.multiple_of((tr * slabs_per_trip + sl) * 16, 16)
                    gs = [[jnp.zeros((8, LANE), F32), jnp.zeros((8, LANE), F32)] for _ in range(n_i)]
                    for hd in range(PEER_HEADS):
                        s2v = s2_ref[hd, pl.ds(j0, 16), ls]
                        e2v = e2_ref[hd, pl.ds(j0, 16), ls]
                        for ii in range(n_i):
                            bs = row_scr[hd, ii, 0, :, ls]
                            be = row_scr[hd, ii, 1, :, ls]
                            for hf in range(2):
                                hs = slice(hf * 8, (hf + 1) * 8)
                                gs[ii][hf] = gs[ii][hf] + jnp.where(s2v[hs] >= bs, e2v[hs], 0.0) * be
                    for ii in range(n_i):
                        r0 = pl.multiple_of(ii * PEER_NKEYS + j0, 16)
                        g16 = jnp.concatenate(gs[ii], axis=0)
                        w_wr[th, pl.ds(r0, 16), lh] = (_gelu(a_rd[th, pl.ds(r0, 16), lh]) * g16).astype(BF16)
                a_wr[tr, ms, :] = part_a
                acc_ref[tr, ms, :] += part_v
                return 0

            lax.fori_loop(0, n_trips, trip, 0, unroll=True)

    @pl.when(g % 2 == 0)
    def _():
        run(a0, a1, w1, w0)

    @pl.when(g % 2 == 1)
    def _():
        run(a1, a0, w0, w1)

    @pl.when((g >= 2) & ((g - 2) % n_chunks == n_chunks - 1))
    def _():
        out = x_ref[...] + mod_ref[0, 5:6, :] * jnp.concatenate(
            [acc_ref[t].T for t in range(acc_ref.shape[0])], axis=0)
        if final_norm:
            ms = jnp.mean(out * out, axis=-1, keepdims=True)
            out = out * lax.rsqrt(ms + EPS) * fg_ref[...]
        o_ref[...] = out
        acc_ref[...] = jnp.zeros_like(acc_ref)


def _peer_dense(h, x, mods, u, vt, beta, s2, e1, e2, fg, tiles_per_batch, nb, n_rows, final_norm):
    d = x.shape[1]
    tm = PEER_TOK_TILE
    n_i = PEER_CHUNK_KEYS
    ne = n_i * PEER_NKEYS
    tiles_per_batch = tiles_per_batch * ROW_TILE // tm
    n_chunks = u.shape[0] // ne
    n_steps = (n_rows // tm) * n_chunks
    assert n_rows % tm == 0 and d == ne
    assert PEER_NKEYS // 16 == (tm // PEER_SUB) * (ne // PEER_SUB)

    def stage(lag):
        def split(g):
            n = jnp.clip(g - lag, 0, n_steps - 1)
            return n // n_chunks, n % n_chunks
        return split

    act, gate, val = stage(0), stage(1), stage(2)
    seg = lambda i: jnp.minimum(i // tiles_per_batch, nb)
    kern = functools.partial(_peer_dense_kernel, n_chunks=n_chunks, final_norm=final_norm)
    full = (PEER_HEADS, PEER_NKEYS, tm)
    rows = (PEER_HEADS, n_i, tm)
    return pl.pallas_call(
        kern,
        grid=(n_steps + 2,),
        in_specs=[pl.BlockSpec((tm, d), lambda g: (act(g)[0], 0)),
                  pl.BlockSpec((tm, d), lambda g: (val(g)[0], 0)),
                  pl.BlockSpec((1, N_MOD, d), lambda g: (seg(val(g)[0]), 0, 0)),
                  pl.BlockSpec((ne, d), lambda g: (act(g)[1], 0)),
                  pl.BlockSpec((d, ne), lambda g: (0, val(g)[1])),
                  pl.BlockSpec(rows, lambda g: (0, gate(g)[1], gate(g)[0])),
                  pl.BlockSpec(full, lambda g: (0, 0, gate(g)[0])),
                  pl.BlockSpec(rows, lambda g: (0, gate(g)[1], gate(g)[0])),
                  pl.BlockSpec(full, lambda g: (0, 0, gate(g)[0])),
                  pl.BlockSpec((1, d), lambda g: (0, 0))],
        out_specs=pl.BlockSpec((tm, d), lambda g: (val(g)[0], 0)),
        out_shape=jax.ShapeDtypeStruct((n_rows, d), F32),
        scratch_shapes=[pltpu.VMEM((tm // PEER_SUB, d, PEER_SUB), F32),
                        pltpu.VMEM((tm // PEER_SUB, ne, PEER_SUB), F32),
                        pltpu.VMEM((tm // PEER_SUB, ne, PEER_SUB), F32),
                        pltpu.VMEM((tm // PEER_SUB, ne, PEER_SUB), BF16),
                        pltpu.VMEM((tm // PEER_SUB, ne, PEER_SUB), BF16),
                        pltpu.VMEM((PEER_HEADS, n_i, 2, 8, tm + LANE), F32)],
        compiler_params=_cparams(("arbitrary",)),
        name="peer_experts",
    )(h, x, mods, u, vt, beta, s2, e1, e2, fg)


def _rope_tables(l_lat, nb, n_ctx_rows):
    rows = l_lat // GRID_W
    r = jnp.repeat(jnp.arange(rows), GRID_W)
    col = jnp.tile(jnp.arange(GRID_W), rows)
    pos = jnp.stack([r, col], axis=-1).astype(F32)
    nf = ATTN_HD // 4
    inv = 1.0 / (ROPE_BASE ** (jnp.arange(nf, dtype=F32) / nf))
    ang = pos[:, :, None] * inv
    cos, sin = jnp.cos(ang), jnp.sin(ang)
    c64 = jnp.concatenate([cos[:, 0], cos[:, 0], cos[:, 1], cos[:, 1]], axis=-1)
    s64 = jnp.concatenate([-sin[:, 0], sin[:, 0], -sin[:, 1], sin[:, 1]], axis=-1)
    c = jnp.tile(c64, (nb, LANE // ATTN_HD))
    s = jnp.tile(s64, (nb, LANE // ATTN_HD))
    c = jnp.concatenate([c, jnp.ones((n_ctx_rows, LANE), F32)], axis=0)
    s = jnp.concatenate([s, jnp.zeros((n_ctx_rows, LANE), F32)], axis=0)
    return c, s


def kernel(x, c, ctx, c_ctx, w_mod, b_mod, norm1_g, norm2_g, w_in, w_out, lam_q1, lam_k1, lam_q2, lam_k2, subln_g, ssm_lambda_re, ssm_lambda_im, ssm_log_step, ssm_b_re, ssm_b_im, ssm_c_re, ssm_c_im, ssm_d, ssm_w_glu, pool_w, pool_scale, peer_wq, peer_keys, peer_u, peer_v, final_g):
    nb, l_lat, d = x.shape
    l_ctx = ctx.shape[1]
    depth = w_mod.shape[0]
    n_lat = nb * l_lat
    n_all = n_lat + nb * l_ctx
    tiles_per_batch = l_lat // ROW_TILE
    assert l_lat % ROW_TILE == 0 and l_ctx == ROW_TILE and nb + 1 <= 8

    cs = jnp.concatenate([c, c_ctx[None], jnp.zeros((8 - nb - 1, d), F32)], axis=0)
    mod_all = _mod_vectors(cs, w_mod, b_mod)
    cos_t, sin_t = _rope_tables(l_lat, 1, ROW_TILE)
    ssm_mats = jax.vmap(_ssm_matrices)(ssm_lambda_re, ssm_lambda_im, ssm_log_step, ssm_b_re, ssm_b_im,
                                       ssm_c_re, ssm_c_im)
    stream = (x.reshape(n_lat, d), ctx.reshape(nb * l_ctx, d), n_lat // ROW_TILE)

    for l in range(depth):
        last = l == depth - 1
        n_rows = n_lat if last else n_all
        mods = mod_all[l, :nb + 1].reshape(nb + 1, N_MOD, d)
        q, k, v, us, up = _inproj(*stream, n_all, norm1_g[l][None], mods, w_in[l].astype(BF16),
                                  cos_t, sin_t, tiles_per_batch, nb)
        lam_init = 0.8 - 0.6 * math.exp(-0.3 * l)
        lam = (jnp.exp(jnp.sum(lam_q1[l] * lam_k1[l])) - jnp.exp(jnp.sum(lam_q2[l] * lam_k2[l]))
               + lam_init).reshape(1).astype(F32)
        attn = _attention(q, k, v, lam, subln_g[l][None], nb, l_lat, l_ctx, not last,
                          1.0 - lam_init)
        y = _ssm(us, tuple(m[l] for m in ssm_mats), ssm_d[l][None], nb, l_lat, l_ctx, not last)
        w_blk = jax.scipy.linalg.block_diag(*[pool_w[l, gi] for gi in range(len(POOL_WINDOWS))])
        xs = _mixout(*stream, attn, y, up, w_blk.astype(BF16), pool_scale[l][None], mods,
                     ssm_w_glu[l].astype(BF16), w_out[l].astype(BF16), tiles_per_batch, nb, n_rows,
                     l_lat, l_ctx)
        h2, beta, s2, e1, e2 = _peer_scores(xs, norm2_g[l][None], mods, peer_wq[l].astype(BF16),
                                               peer_keys[l].astype(BF16), tiles_per_batch, nb, n_rows)
        xs = _peer_dense(h2, xs, mods, peer_u[l].astype(BF16), _transpose_cast(peer_v[l], BF16),
                         beta, s2, e1, e2, final_g[None], tiles_per_batch, nb, n_rows, last)
        stream = (xs, xs, n_rows // ROW_TILE)
    return xs.reshape(nb, l_lat, d)
```
